```python
import jax
import jax.numpy as jnp
from jax import lax
import numpy as np

D_MODEL = 1024
BATCH = 8
SEQ = 4096
DEPTH = 4

CHUNK = 64
N_EVEN = (DEPTH + 1) // 2
N_ODD = DEPTH // 2
DEEPNORM_ALPHA = (2 * DEPTH) ** 0.25
DEEPNORM_BETA = (8 * DEPTH) ** -0.25
LN_EPS = 1e-5
RMS_EPS = 1e-6

A_BLOCK = 128
A_GROUPS = 4
A_WIDTH = D_MODEL // 2
A_GROUP_DIM = A_WIDTH // A_GROUPS

B_HEADS = 4
B_HEAD_DIM = D_MODEL // 8
B_WIDTH = B_HEADS * B_HEAD_DIM
CONV_K = 4

EVEN_SPLITS = (A_WIDTH, A_WIDTH, B_WIDTH, B_WIDTH, B_WIDTH, B_WIDTH, B_HEADS, B_HEADS)
EVEN_IN = sum(EVEN_SPLITS)
EVEN_MIX = A_WIDTH + B_WIDTH

C_HEADS = 8
C_NOPE = 128
C_ROPE = 64
C_V = D_MODEL // C_HEADS
Q_LORA = 3 * D_MODEL // 8
KV_LORA = D_MODEL // 4
ODD_IN = Q_LORA + KV_LORA + C_ROPE
ROPE_THETA = 10000.0
Q_BLOCK = 128

N_EXPERTS = 32
TOP_K = 4
D_FF = D_MODEL
SWIGLU_LIMIT = 7.0
SWIGLU_ALPHA = 1.702
EXPERT_BLOCK = 512

kernel_name = 'hybrid_gmlp_gdn_mla_moe_deepnorm'


def layer_norm(x, g, b):
    xf = x.astype(jnp.float32)
    mu = jnp.mean(xf, axis=-1, keepdims=True)
    var = jnp.mean(jnp.square(xf - mu), axis=-1, keepdims=True)
    return ((xf - mu) * lax.rsqrt(var + LN_EPS) * g + b).astype(x.dtype)


def rms_norm(x, g):
    xf = x.astype(jnp.float32)
    return (xf * lax.rsqrt(jnp.mean(xf * xf, axis=-1, keepdims=True) + RMS_EPS) * g).astype(x.dtype)


def l2_normalize(x):
    return x * lax.rsqrt(jnp.sum(x * x, axis=-1, keepdims=True) + RMS_EPS)


def split_columns(h, sizes):
    return jnp.split(h, np.cumsum(sizes)[:-1].tolist(), axis=-1)


def spatial_gating(u, v, ln_g, ln_b, w_s, b_s):
    bsz, seq, _ = u.shape
    n_blk = seq // A_BLOCK
    u = jax.nn.gelu(u, approximate=False)
    v = jax.nn.gelu(v, approximate=False).reshape(bsz, seq, A_GROUPS, A_GROUP_DIM)
    v = layer_norm(v, ln_g.reshape(A_GROUPS, A_GROUP_DIM), ln_b.reshape(A_GROUPS, A_GROUP_DIM))
    pos_chunk = jnp.arange(A_BLOCK) // CHUNK
    allowed = pos_chunk[None, :] <= pos_chunk[:, None]
    w = jnp.where(allowed[None], w_s, 0)
    v = v.reshape(bsz, n_blk, A_BLOCK, A_GROUPS, A_GROUP_DIM)
    mixed = jnp.einsum('gij,bnjgc->bnigc', w, v) + b_s.T[:, :, None]
    return u * mixed.reshape(bsz, seq, A_WIDTH)


def causal_depthwise_conv(x, w):
    xp = jnp.pad(x, ((0, 0), (CONV_K - 1, 0), (0, 0)))
    return lax.conv_general_dilated(xp, w[:, None, :].astype(x.dtype), window_strides=(1,), padding='VALID',
                                    dimension_numbers=('NWC', 'WIO', 'NWC'), feature_group_count=x.shape[-1])


def chunked_gated_delta_rule(q, k, v, g, beta):
    bsz, nh, seq, dk = q.shape
    dv = v.shape[-1]
    n_chunks = seq // CHUNK

    def to_chunks(t):
        return t.reshape(bsz, nh, n_chunks, CHUNK, *t.shape[3:])

    q, k, v, g, beta = (to_chunks(t) for t in (q, k, v, g, beta))
    g = jnp.cumsum(g, axis=-1)
    idx = jnp.arange(CHUNK)
    incl = idx[:, None] >= idx[None, :]
    strict = idx[:, None] > idx[None, :]
    decay = jnp.exp(jnp.where(incl, g[..., :, None] - g[..., None, :], -jnp.inf))
    k_beta = k * beta[..., None]
    lower = jnp.where(strict, jnp.einsum('bhncd,bhnmd->bhncm', k_beta, k) * decay, 0.0)
    eye = jnp.eye(CHUNK, dtype=jnp.float32)
    t_mat = lax.linalg.triangular_solve(eye + lower, jnp.broadcast_to(eye, lower.shape),
                                        left_side=True, lower=True, unit_diagonal=True)
    u = t_mat @ (v * beta[..., None])
    w = t_mat @ (k_beta * jnp.exp(g)[..., None])
    intra = jnp.where(incl, jnp.einsum('bhncd,bhnmd->bhncm', q, k) * decay, 0.0)
    q_dec = q * jnp.exp(g)[..., None]
    g_last = g[..., -1]
    k_dec = k * jnp.exp(g_last[..., None] - g)[..., None]

    def step(state, inp):
        u_c, w_c, intra_c, q_c, k_c, gl_c = inp
        v_new = u_c - w_c @ state
        out = q_c @ state + intra_c @ v_new
        state = state * jnp.exp(gl_c)[..., None, None] + jnp.swapaxes(k_c, -1, -2) @ v_new
        return state, out

    xs = tuple(jnp.moveaxis(t, 2, 0) for t in (u, w, intra, q_dec, k_dec, g_last))
    state0 = jnp.zeros((bsz, nh, dk, dv), jnp.float32)
    _, out = lax.scan(step, state0, xs)
    return jnp.moveaxis(out, 0, 2).reshape(bsz, nh, seq, dv)


def gated_deltanet(q, k, v, a, b, z, conv_w, a_log, dt_bias, norm_w):
    bsz, seq, _ = q.shape
    qkv = jax.nn.silu(causal_depthwise_conv(jnp.concatenate([q, k, v], axis=-1), conv_w))
    q, k, v = jnp.split(qkv, 3, axis=-1)

    def heads(t):
        return t.reshape(bsz, seq, B_HEADS, B_HEAD_DIM).transpose(0, 2, 1, 3).astype(jnp.float32)

    q = l2_normalize(heads(q)) * (B_HEAD_DIM ** -0.5)
    k = l2_normalize(heads(k))
    v = heads(v)
    beta = jax.nn.sigmoid(b.astype(jnp.float32)).transpose(0, 2, 1)
    g = (-jnp.exp(a_log.astype(jnp.float32)) * jax.nn.softplus(a.astype(jnp.float32) + dt_bias)).transpose(0, 2, 1)
    o = chunked_gated_delta_rule(q, k, v, g, beta).transpose(0, 2, 1, 3)
    gate = jax.nn.silu(z.reshape(bsz, seq, B_HEADS, B_HEAD_DIM).astype(jnp.float32))
    o = rms_norm(o, norm_w) * gate
    return o.reshape(bsz, seq, B_WIDTH).astype(z.dtype)


def even_mixer(x, w_in, sgu_ln_g, sgu_ln_b, sgu_w, sgu_b, conv_w, a_log, dt_bias, gdn_norm, w_out):
    a_u, a_v, b_q, b_k, b_v, b_z, b_a, b_b = split_columns(x @ w_in, EVEN_SPLITS)
    y_a = spatial_gating(a_u, a_v, sgu_ln_g, sgu_ln_b, sgu_w, sgu_b)
    y_b = gated_deltanet(b_q, b_k, b_v, b_a, b_b, b_z, conv_w, a_log, dt_bias, gdn_norm)
    return jnp.concatenate([y_a, y_b], axis=-1) @ w_out


def rope_tables(seq):
    inv_freq = ROPE_THETA ** (-jnp.arange(0, C_ROPE, 2, dtype=jnp.float32) / C_ROPE)
    ang = jnp.arange(seq, dtype=jnp.float32)[:, None] * inv_freq[None, :]
    return jnp.cos(ang), jnp.sin(ang)


def apply_rope(x, cos, sin):
    x1, x2 = jnp.split(x, 2, axis=-1)
    c = cos[None, :, None, :]
    s = sin[None, :, None, :]
    return jnp.concatenate([x1 * c - x2 * s, x1 * s + x2 * c], axis=-1).astype(x.dtype)


def chunk_causal_attention(q, k, v, scale):
    seq = q.shape[2]
    outs = []
    for blk in range(seq // Q_BLOCK):
        q0 = blk * Q_BLOCK
        k_end = q0 + Q_BLOCK
        s = jnp.einsum('bhqd,bhkd->bhqk', q[:, :, q0:k_end], k[:, :, :k_end]).astype(jnp.float32) * scale
        q_chunk = (q0 + jnp.arange(Q_BLOCK)) // CHUNK
        k_chunk = jnp.arange(k_end) // CHUNK
        s = jnp.where(k_chunk[None, :] <= q_chunk[:, None], s, -jnp.inf)
        p = jax.nn.softmax(s, axis=-1).astype(v.dtype)
        outs.append(jnp.einsum('bhqk,bhkd->bhqd', p, v[:, :, :k_end]))
    return jnp.concatenate(outs, axis=2)


def odd_mixer(x, w_in, q_norm, w_uq, kv_norm, w_ukv, w_out, cos, sin):
    bsz, seq, _ = x.shape
    c_q, c_kv, k_pe = split_columns(x @ w_in, (Q_LORA, KV_LORA, C_ROPE))
    q = (rms_norm(c_q, q_norm) @ w_uq).reshape(bsz, seq, C_HEADS, C_NOPE + C_ROPE)
    q = jnp.concatenate([q[..., :C_NOPE], apply_rope(q[..., C_NOPE:], cos, sin)], axis=-1)
    kv = (rms_norm(c_kv, kv_norm) @ w_ukv).reshape(bsz, seq, C_HEADS, C_NOPE + C_V)
    k_pe = apply_rope(k_pe[:, :, None, :], cos, sin)
    k = jnp.concatenate([kv[..., :C_NOPE], jnp.broadcast_to(k_pe, (bsz, seq, C_HEADS, C_ROPE))], axis=-1)
    v = kv[..., C_NOPE:]
    o = chunk_causal_attention(q.transpose(0, 2, 1, 3), k.transpose(0, 2, 1, 3), v.transpose(0, 2, 1, 3),
                               (C_NOPE + C_ROPE) ** -0.5)
    return o.transpose(0, 2, 1, 3).reshape(bsz, seq, C_HEADS * C_V) @ w_out


def clamped_swiglu(h):
    glu = jnp.minimum(h[..., ::2], SWIGLU_LIMIT)
    lin = jnp.clip(h[..., 1::2], -SWIGLU_LIMIT, SWIGLU_LIMIT)
    return glu * jax.nn.sigmoid(SWIGLU_ALPHA * glu) * (lin + 1.0)


def moe_ffn(x, w_router, b_router, w_gate_up, b_gate_up, w_down, b_down):
    bsz, seq, d = x.shape
    xt = x.reshape(-1, d)
    n_tok = xt.shape[0]
    n_assign = n_tok * TOP_K
    logits = (xt @ w_router + b_router).astype(jnp.float32)
    top_logit, top_e = lax.top_k(logits, TOP_K)
    gate = jax.nn.softmax(top_logit, axis=-1)
    e_flat = top_e.reshape(-1)
    order = jnp.argsort(e_flat)
    e_sorted = e_flat[order]
    tok_sorted = (order // TOP_K).astype(jnp.int32)
    gate_sorted = gate.reshape(-1)[order]
    counts = jnp.bincount(e_flat, length=N_EXPERTS)
    padded = (counts + EXPERT_BLOCK - 1) // EXPERT_BLOCK * EXPERT_BLOCK
    start = jnp.cumsum(counts) - counts
    pad_end = jnp.cumsum(padded)
    pad_start = pad_end - padded
    dest = pad_start[e_sorted] + jnp.arange(n_assign) - start[e_sorted]
    n_blocks = -(-n_assign // EXPERT_BLOCK) + N_EXPERTS
    slots = n_blocks * EXPERT_BLOCK
    tok_buf = jnp.full((slots,), n_tok, jnp.int32).at[dest].set(tok_sorted)
    gate_buf = jnp.zeros((slots,), jnp.float32).at[dest].set(gate_sorted)
    block_expert = jnp.minimum(jnp.searchsorted(pad_end, jnp.arange(n_blocks) * EXPERT_BLOCK, side='right'),
                               N_EXPERTS - 1)
    x_pad = jnp.concatenate([xt, jnp.zeros((1, d), xt.dtype)], axis=0)

    def run_block(args):
        tok, gt, e = args
        h = x_pad[tok] @ w_gate_up[e] + b_gate_up[e]
        y = clamped_swiglu(h) @ w_down[e] + b_down[e]
        return y * gt[:, None].astype(y.dtype)

    y = lax.map(run_block, (tok_buf.reshape(n_blocks, EXPERT_BLOCK), gate_buf.reshape(n_blocks, EXPERT_BLOCK),
                            block_expert))
    out = jax.ops.segment_sum(y.reshape(slots, d), tok_buf, num_segments=n_tok + 1)[:n_tok]
    return out.reshape(bsz, seq, d)


def setup_inputs(seed: int = 0) -> dict:
    key = jax.random.key(seed)
    ks = jax.random.split(key, 32)

    def nrm(k, shape, scale):
        return jax.random.normal(k, shape, jnp.float32) * scale

    def gain(k, shape):
        return 1.0 + nrm(k, shape, 0.02)

    dt = jnp.exp(jax.random.uniform(ks[8], (N_EVEN, B_HEADS), jnp.float32, np.log(1e-3), np.log(1e-1)))
    return {
        'x': nrm(ks[0], (BATCH, SEQ, D_MODEL), 1.0),
        'even_w_in': nrm(ks[1], (N_EVEN, D_MODEL, EVEN_IN), D_MODEL ** -0.5),
        'even_sgu_ln_g': gain(ks[2], (N_EVEN, A_WIDTH)),
        'even_sgu_ln_b': nrm(ks[3], (N_EVEN, A_WIDTH), 0.02),
        'even_sgu_w': nrm(ks[4], (N_EVEN, A_GROUPS, A_BLOCK, A_BLOCK), A_BLOCK ** -0.5),
        'even_sgu_b': gain(ks[5], (N_EVEN, A_GROUPS, A_BLOCK)),
        'even_conv_w': nrm(ks[6], (N_EVEN, CONV_K, 3 * B_WIDTH), CONV_K ** -0.5),
        'even_a_log': jnp.log(jax.random.uniform(ks[7], (N_EVEN, B_HEADS), jnp.float32, 1.0, 16.0)),
        'even_dt_bias': dt + jnp.log(-jnp.expm1(-dt)),
        'even_gdn_norm': gain(ks[9], (N_EVEN, B_HEAD_DIM)),
        'even_w_out': nrm(ks[10], (N_EVEN, EVEN_MIX, D_MODEL), EVEN_MIX ** -0.5 * DEEPNORM_BETA),
        'odd_w_in': nrm(ks[11], (N_ODD, D_MODEL, ODD_IN), D_MODEL ** -0.5),
        'odd_q_norm': gain(ks[12], (N_ODD, Q_LORA)),
        'odd_w_uq': nrm(ks[13], (N_ODD, Q_LORA, C_HEADS * (C_NOPE + C_ROPE)), Q_LORA ** -0.5),
        'odd_kv_norm': gain(ks[14], (N_ODD, KV_LORA)),
        'odd_w_ukv': nrm(ks[15], (N_ODD, KV_LORA, C_HEADS * (C_NOPE + C_V)), KV_LORA ** -0.5),
        'odd_w_out': nrm(ks[16], (N_ODD, C_HEADS * C_V, D_MODEL), (C_HEADS * C_V) ** -0.5 * DEEPNORM_BETA),
        'ln_mix_g': gain(ks[17], (DEPTH, D_MODEL)),
        'ln_mix_b': nrm(ks[18], (DEPTH, D_MODEL), 0.02),
        'ln_ffn_g': gain(ks[19], (DEPTH, D_MODEL)),
        'ln_ffn_b': nrm(ks[20], (DEPTH, D_MODEL), 0.02),
        'moe_w_router': nrm(ks[21], (DEPTH, D_MODEL, N_EXPERTS), D_MODEL ** -0.5),
        'moe_b_router': nrm(ks[22], (DEPTH, N_EXPERTS), 0.01),
        'moe_w_gate_up': nrm(ks[23], (DEPTH, N_EXPERTS, D_MODEL, 2 * D_FF), D_MODEL ** -0.5),
        'moe_b_gate_up': nrm(ks[24], (DEPTH, N_EXPERTS, 2 * D_FF), 0.02),
        'moe_w_down': nrm(ks[25], (DEPTH, N_EXPERTS, D_FF, D_MODEL), D_FF ** -0.5 * DEEPNORM_BETA),
        'moe_b_down': nrm(ks[26], (DEPTH, N_EXPERTS, D_MODEL), 0.02),
    }


def reference(x, even_w_in, even_sgu_ln_g, even_sgu_ln_b, even_sgu_w, even_sgu_b, even_conv_w, even_a_log,
              even_dt_bias, even_gdn_norm, even_w_out, odd_w_in, odd_q_norm, odd_w_uq, odd_kv_norm, odd_w_ukv,
              odd_w_out, ln_mix_g, ln_mix_b, ln_ffn_g, ln_ffn_b, moe_w_router, moe_b_router, moe_w_gate_up,
              moe_b_gate_up, moe_w_down, moe_b_down):
    cos, sin = rope_tables(x.shape[1])
    for layer in range(DEPTH):
        i = layer // 2
        if layer % 2 == 0:
            mix = even_mixer(x, even_w_in[i], even_sgu_ln_g[i], even_sgu_ln_b[i], even_sgu_w[i], even_sgu_b[i],
                             even_conv_w[i], even_a_log[i], even_dt_bias[i], even_gdn_norm[i], even_w_out[i])
        else:
            mix = odd_mixer(x, odd_w_in[i], odd_q_norm[i], odd_w_uq[i], odd_kv_norm[i], odd_w_ukv[i],
                            odd_w_out[i], cos, sin)
        x = layer_norm(DEEPNORM_ALPHA * x + mix, ln_mix_g[layer], ln_mix_b[layer])
        ffn = moe_ffn(x, moe_w_router[layer], moe_b_router[layer], moe_w_gate_up[layer], moe_b_gate_up[layer],
                      moe_w_down[layer], moe_b_down[layer])
        x = layer_norm(DEEPNORM_ALPHA * x + ffn, ln_ffn_g[layer], ln_ffn_b[layer])
    return x
```

```python
import functools

import jax
import jax.numpy as jnp
import numpy as np
from jax import lax
from jax.experimental import pallas as pl
from jax.experimental.pallas import tpu as pltpu

F32 = jnp.float32
BF16 = jnp.bfloat16

D_MODEL = 1024
DEPTH = 4
CHUNK = 64
CHUNK_SHIFT = CHUNK.bit_length() - 1
DEEPNORM_ALPHA = (2 * DEPTH) ** 0.25
LN_EPS = 1e-5
RMS_EPS = 1e-6

A_BLOCK = 128
A_GROUPS = 4
A_WIDTH = D_MODEL // 2
A_GROUP_DIM = A_WIDTH // A_GROUPS

B_HEADS = 4
B_HEAD_DIM = D_MODEL // 8
B_WIDTH = B_HEADS * B_HEAD_DIM
CONV_K = 4

C_HEADS = 8
C_NOPE = 128
C_ROPE = 64
C_V = D_MODEL // C_HEADS
C_QK = C_NOPE + C_ROPE
Q_LORA = 3 * D_MODEL // 8
KV_LORA = D_MODEL // 4
ROPE_THETA = 10000.0

N_EXPERTS = 32
TOP_K = 4
D_FF = D_MODEL
SWIGLU_LIMIT = 7.0
SWIGLU_ALPHA = 1.702

LANES = 128
SUBLANES = 8
VMEM_LIMIT = 48 * 1024 * 1024
NEG_BIG = -1e30


def _params(*sem):
    return pltpu.CompilerParams(dimension_semantics=sem, vmem_limit_bytes=VMEM_LIMIT)


def _dot(a, b):
    return jnp.dot(a, b, preferred_element_type=F32)


def _dot_nt(a, b):
    return lax.dot_general(a, b, (((1,), (1,)), ((), ())), preferred_element_type=F32)


def _dot_tn(a, b):
    return lax.dot_general(a, b, (((0,), (0,)), ((), ())), preferred_element_type=F32)


def _sigmoid(x):
    return 1.0 / (1.0 + jnp.exp(-x))


def _softplus(x):
    return jnp.maximum(x, 0.0) + jnp.log1p(jnp.exp(-jnp.abs(x)))


def _layer_norm_rows(y, g, b):
    mu = jnp.mean(y, axis=-1, keepdims=True)
    d = y - mu
    var = jnp.mean(d * d, axis=-1, keepdims=True)
    return d * lax.rsqrt(var + LN_EPS) * g + b


def _mm_kernel(x_ref, w_ref, o_ref):
    o_ref[...] = _dot(x_ref[...].astype(BF16), w_ref[...]).astype(o_ref.dtype)


def _mm(x, w, out_dtype, tm):
    t, k = x.shape
    n = w.shape[1]
    return pl.pallas_call(
        _mm_kernel,
        grid=(t // tm,),
        in_specs=[pl.BlockSpec((tm, k), lambda i: (i, 0)),
                  pl.BlockSpec((k, n), lambda i: (0, 0))],
        out_specs=pl.BlockSpec((tm, n), lambda i: (i, 0)),
        out_shape=jax.ShapeDtypeStruct((t, n), out_dtype),
        compiler_params=_params("parallel"),
        name="mm",
    )(x, w)


def _gelu(x):
    return 0.5 * x * (1.0 + lax.erf(x * (2.0 ** -0.5)))


def _sgu_kernel(x_ref, w_ref, g_ref, b_ref, sw_ref, sb_ref, o_ref):
    tm = x_ref.shape[0]
    h = _dot(x_ref[...].astype(BF16), w_ref[...])
    for grp in range(A_GROUPS):
        c0 = grp * A_GROUP_DIM
        u = _gelu(h[:, c0:c0 + A_GROUP_DIM])
        v = _gelu(h[:, A_WIDTH + c0:A_WIDTH + c0 + A_GROUP_DIM])
        vn = _layer_norm_rows(v, g_ref[:, c0:c0 + A_GROUP_DIM], b_ref[:, c0:c0 + A_GROUP_DIM]).astype(BF16)
        for blk in range(tm // A_BLOCK):
            r0 = blk * A_BLOCK
            mixed = _dot(sw_ref[grp], vn[r0:r0 + A_BLOCK]) + sb_ref[grp]
            o_ref[r0:r0 + A_BLOCK, c0:c0 + A_GROUP_DIM] = (u[r0:r0 + A_BLOCK] * mixed).astype(o_ref.dtype)


def _sgu(x, w_a, ln_g, ln_b, sw, sb, tm):
    t = x.shape[0]
    return pl.pallas_call(
        _sgu_kernel,
        grid=(t // tm,),
        in_specs=[pl.BlockSpec((tm, D_MODEL), lambda i: (i, 0)),
                  pl.BlockSpec((D_MODEL, 2 * A_WIDTH), lambda i: (0, 0)),
                  pl.BlockSpec((1, A_WIDTH), lambda i: (0, 0)),
                  pl.BlockSpec((1, A_WIDTH), lambda i: (0, 0)),
                  pl.BlockSpec((A_GROUPS, A_BLOCK, A_BLOCK), lambda i: (0, 0, 0)),
                  pl.BlockSpec((A_GROUPS, A_BLOCK, A_GROUP_DIM), lambda i: (0, 0, 0))],
        out_specs=pl.BlockSpec((tm, A_WIDTH), lambda i: (i, 0)),
        out_shape=jax.ShapeDtypeStruct((t, A_WIDTH), BF16),
        compiler_params=_params("parallel"),
        name="sgu",
    )(x, w_a, ln_g, ln_b, sw, sb)


def _gdn_kernel(h_ref, ab_ref, abt_ref, cw_ref, hp_ref, hpt_ref, nw_ref, o_ref, conv_scr, state_scr):
    tr = h_ref.shape[0]
    n_chunks = tr // CHUNK
    s_idx = pl.program_id(1)

    @pl.when(s_idx == 0)
    def _():
        conv_scr[0:SUBLANES, :] = jnp.zeros((SUBLANES, 3 * B_WIDTH), F32)
        state_scr[...] = jnp.zeros(state_scr.shape, F32)

    conv_scr[SUBLANES:SUBLANES + tr, :] = h_ref[:, 0:3 * B_WIDTH].astype(F32)
    acc = conv_scr[SUBLANES:SUBLANES + tr, :] * cw_ref[CONV_K - 1:CONV_K, :]
    for d in range(1, CONV_K):
        acc = acc + conv_scr[SUBLANES - d:SUBLANES - d + tr, :] * cw_ref[CONV_K - 1 - d:CONV_K - d, :]
    conv_scr[0:SUBLANES, :] = conv_scr[tr:tr + SUBLANES, :]
    qkv = acc * _sigmoid(acc)

    ab = ab_ref[...]
    sp_c = _softplus(ab + hp_ref[1:2, :])
    g_col = -jnp.exp(hp_ref[0:1, :]) * sp_c
    beta_col = _sigmoid(ab)
    abt = abt_ref[0]
    sp_r = _softplus(abt + hpt_ref[:, 1:2])
    g_row = -jnp.exp(hpt_ref[:, 0:1]) * sp_r

    ri = lax.broadcasted_iota(jnp.int32, (tr, tr), 0)
    ci = lax.broadcasted_iota(jnp.int32, (tr, tr), 1)
    same = jnp.right_shift(ri, CHUNK_SHIFT) == jnp.right_shift(ci, CHUNK_SHIFT)
    lower = jnp.where(same & (ri >= ci), 1.0, 0.0).astype(F32)
    upper = jnp.where(same & (ri <= ci), 1.0, 0.0).astype(F32)
    gc_col = jnp.dot(lower, g_col, preferred_element_type=F32, precision=lax.Precision.HIGHEST)
    gc_row = jnp.dot(g_row, upper, preferred_element_type=F32, precision=lax.Precision.HIGHEST)

    ii = lax.broadcasted_iota(jnp.int32, (CHUNK, CHUNK), 0)
    jj = lax.broadcasted_iota(jnp.int32, (CHUNK, CHUNK), 1)
    incl = ii >= jj
    strict = ii > jj
    eye = jnp.where(ii == jj, 1.0, 0.0).astype(F32)

    for c in range(n_chunks):
        r0 = c * CHUNK
        for hd in range(B_HEADS):
            l0 = hd * B_HEAD_DIM
            q = qkv[r0:r0 + CHUNK, l0:l0 + B_HEAD_DIM]
            k = qkv[r0:r0 + CHUNK, B_WIDTH + l0:B_WIDTH + l0 + B_HEAD_DIM]
            v = qkv[r0:r0 + CHUNK, 2 * B_WIDTH + l0:2 * B_WIDTH + l0 + B_HEAD_DIM]
            q = q * lax.rsqrt(jnp.sum(q * q, axis=-1, keepdims=True) + RMS_EPS) * (B_HEAD_DIM ** -0.5)
            k = k * lax.rsqrt(jnp.sum(k * k, axis=-1, keepdims=True) + RMS_EPS)
            beta = beta_col[r0:r0 + CHUNK, B_HEADS + hd:B_HEADS + hd + 1]
            gcc = gc_col[r0:r0 + CHUNK, hd:hd + 1]
            gcr = gc_row[hd:hd + 1, r0:r0 + CHUNK]
            g_last = gc_col[r0 + CHUNK - 1:r0 + CHUNK, hd:hd + 1]
            decay = jnp.where(incl, jnp.exp(jnp.where(incl, gcc - gcr, 0.0)), 0.0)
            k_b = k.astype(BF16)
            kb = k * beta
            a_mat = jnp.where(strict, _dot_nt(kb.astype(BF16), k_b) * decay, 0.0)
            p = -a_mat
            t_mat = eye + p
            for _ in range(5):
                pb = p.astype(BF16)
                p = _dot(pb, pb)
                t_mat = t_mat + _dot(t_mat.astype(BF16), p.astype(BF16))
            t_b = t_mat.astype(BF16)
            e_gc = jnp.exp(gcc)
            u = _dot(t_b, (v * beta).astype(BF16))
            w = _dot(t_b, (kb * e_gc).astype(BF16))
            intra = jnp.where(incl, _dot_nt(q.astype(BF16), k_b) * decay, 0.0)
            q_dec = q * e_gc
            k_dec = k * jnp.exp(g_last - gcc)
            state = state_scr[hd]
            v_new = u - _dot(w.astype(BF16), state.astype(BF16))
            out = _dot(q_dec.astype(BF16), state.astype(BF16)) + _dot(intra.astype(BF16), v_new.astype(BF16))
            state_scr[hd] = state * jnp.exp(g_last) + _dot_tn(k_dec.astype(BF16), v_new.astype(BF16))
            z = h_ref[r0:r0 + CHUNK, 3 * B_WIDTH + l0:3 * B_WIDTH + l0 + B_HEAD_DIM].astype(F32)
            o_n = out * lax.rsqrt(jnp.mean(out * out, axis=-1, keepdims=True) + RMS_EPS) * nw_ref[...]
            o_ref[r0:r0 + CHUNK, l0:l0 + B_HEAD_DIM] = (o_n * (z * _sigmoid(z))).astype(o_ref.dtype)


def _gdn(h_b, ab, abt, conv_w, hp, hpt, norm_w, bsz, seq, tr):
    n_s = seq // tr
    return pl.pallas_call(
        _gdn_kernel,
        grid=(bsz, n_s),
        in_specs=[pl.BlockSpec((tr, 4 * B_WIDTH), lambda b, s: (b * n_s + s, 0)),
                  pl.BlockSpec((tr, LANES), lambda b, s: (b * n_s + s, 0)),
                  pl.BlockSpec((1, SUBLANES, tr), lambda b, s: (b, 0, s)),
                  pl.BlockSpec((CONV_K, 3 * B_WIDTH), lambda b, s: (0, 0)),
                  pl.BlockSpec((2, LANES), lambda b, s: (0, 0)),
                  pl.BlockSpec((SUBLANES, 2), lambda b, s: (0, 0)),
                  pl.BlockSpec((1, B_HEAD_DIM), lambda b, s: (0, 0))],
        out_specs=pl.BlockSpec((tr, B_WIDTH), lambda b, s: (b * n_s + s, 0)),
        out_shape=jax.ShapeDtypeStruct((bsz * seq, B_WIDTH), BF16),
        scratch_shapes=[pltpu.VMEM((tr + 2 * SUBLANES, 3 * B_WIDTH), F32),
                        pltpu.VMEM((B_HEADS, B_HEAD_DIM, B_HEAD_DIM), F32)],
        compiler_params=_params("parallel", "arbitrary"),
        name="gdn",
    )(h_b, ab, abt, conv_w, hp, hpt, norm_w)


def _mla_up_kernel(h_ref, qn_ref, kvn_ref, wq_ref, wkv_ref, cs_ref, q_ref, k_ref, v_ref):
    h = h_ref[...].astype(F32)
    cs = cs_ref[...]

    def rms(x, g):
        return x * lax.rsqrt(jnp.mean(x * x, axis=-1, keepdims=True) + RMS_EPS) * g

    def rope(t):
        r = t * cs
        return r + pltpu.roll(r, C_ROPE, 1)

    cq = rms(h[:, 0:Q_LORA], qn_ref[...]).astype(BF16)
    ckv = rms(h[:, Q_LORA:Q_LORA + KV_LORA], kvn_ref[...]).astype(BF16)
    k_pe = rope(h[:, Q_LORA + KV_LORA:Q_LORA + KV_LORA + 2 * C_ROPE])[:, 0:C_ROPE].astype(BF16)
    qf = _dot(cq, wq_ref[...]) * (C_QK ** -0.5)
    kv = _dot(ckv, wkv_ref[...])
    for hd in range(C_HEADS):
        q0 = hd * (C_NOPE + 2 * C_ROPE)
        q_ref[0, hd, :, 0:C_NOPE] = qf[:, q0:q0 + C_NOPE].astype(BF16)
        q_ref[0, hd, :, C_NOPE:C_QK] = rope(qf[:, q0 + C_NOPE:q0 + C_NOPE + 2 * C_ROPE])[:, 0:C_ROPE].astype(BF16)
        k0 = hd * (C_NOPE + C_V)
        k_ref[0, hd, :, 0:C_NOPE] = kv[:, k0:k0 + C_NOPE].astype(BF16)
        k_ref[0, hd, :, C_NOPE:C_QK] = k_pe
        v_ref[0, hd, :, :] = kv[:, k0 + C_NOPE:k0 + C_NOPE + C_V].astype(BF16)


def _mla_up(h, q_norm, kv_norm, w_uq, w_ukv, cs, bsz, seq, tm):
    n_s = seq // tm
    hw = h.shape[1]
    qk_shape = jax.ShapeDtypeStruct((bsz, C_HEADS, seq, C_QK), BF16)
    return pl.pallas_call(
        _mla_up_kernel,
        grid=(bsz, n_s),
        in_specs=[pl.BlockSpec((tm, hw), lambda b, s: (b * n_s + s, 0)),
                  pl.BlockSpec((1, Q_LORA), lambda b, s: (0, 0)),
                  pl.BlockSpec((1, KV_LORA), lambda b, s: (0, 0)),
                  pl.BlockSpec(w_uq.shape, lambda b, s: (0, 0)),
                  pl.BlockSpec(w_ukv.shape, lambda b, s: (0, 0)),
                  pl.BlockSpec((tm, LANES), lambda b, s: (s, 0))],
        out_specs=[pl.BlockSpec((1, C_HEADS, tm, C_QK), lambda b, s: (b, 0, s, 0)),
                   pl.BlockSpec((1, C_HEADS, tm, C_QK), lambda b, s: (b, 0, s, 0)),
                   pl.BlockSpec((1, C_HEADS, tm, C_V), lambda b, s: (b, 0, s, 0))],
        out_shape=[qk_shape, qk_shape, jax.ShapeDtypeStruct((bsz, C_HEADS, seq, C_V), BF16)],
        compiler_params=_params("parallel", "parallel"),
        name="mla_up",
    )(h, q_norm, kv_norm, w_uq, w_ukv, cs)


def _attn_kernel(q_ref, k_ref, v_ref, o_ref):
    tq = q_ref.shape[2]
    qi = pl.program_id(2)
    q = q_ref[0, 0]

    def step(s, m, l, acc, vb):
        m_new = jnp.maximum(m, jnp.max(s, axis=-1, keepdims=True))
        alpha = jnp.exp(m - m_new)
        p = jnp.exp(s - m_new)
        l = alpha * l + jnp.sum(p, axis=-1, keepdims=True)
        acc = alpha * acc + _dot(p.astype(BF16), vb)
        return m_new, l, acc

    def body(j, carry):
        m, l, acc = carry
        k0 = pl.multiple_of(j * tq, tq)
        s = _dot_nt(q, k_ref[0, 0, pl.ds(k0, tq), :])
        return step(s, m, l, acc, v_ref[0, 0, pl.ds(k0, tq), :])

    init = (jnp.full((tq, 1), NEG_BIG, F32), jnp.zeros((tq, 1), F32), jnp.zeros((tq, C_V), F32))
    m, l, acc = lax.fori_loop(0, qi, body, init)
    k0 = pl.multiple_of(qi * tq, tq)
    s = _dot_nt(q, k_ref[0, 0, pl.ds(k0, tq), :])
    ri = jnp.right_shift(lax.broadcasted_iota(jnp.int32, (tq, tq), 0), CHUNK_SHIFT)
    ci = jnp.right_shift(lax.broadcasted_iota(jnp.int32, (tq, tq), 1), CHUNK_SHIFT)
    s = jnp.where(ci <= ri, s, NEG_BIG)
    m, l, acc = step(s, m, l, acc, v_ref[0, 0, pl.ds(k0, tq), :])
    o_ref[0] = (acc / l).astype(o_ref.dtype)


def _attn(q, k, v, tq):
    bsz, _, seq, _ = q.shape
    return pl.pallas_call(
        _attn_kernel,
        grid=(bsz, C_HEADS, seq // tq),
        in_specs=[pl.BlockSpec((1, 1, tq, C_QK), lambda b, h, i: (b, h, i, 0)),
                  pl.BlockSpec((1, 1, seq, C_QK), lambda b, h, i: (b, h, 0, 0)),
                  pl.BlockSpec((1, 1, seq, C_V), lambda b, h, i: (b, h, 0, 0))],
        out_specs=pl.BlockSpec((1, tq, C_V), lambda b, h, i: (b, i, h)),
        out_shape=jax.ShapeDtypeStruct((bsz, seq, C_HEADS * C_V), BF16),
        compiler_params=_params("parallel", "parallel", "arbitrary"),
        name="attn",
    )(q, k, v)


def _proj_ln_router_kernel(n_in, *refs):
    a_refs = refs[0:n_in]
    w_refs = refs[n_in:2 * n_in]
    x_ref, g_ref, b_ref, wr_ref, br_ref = refs[2 * n_in:2 * n_in + 5]
    x1_ref, meta_ref, gate_ref, cnt_ref = refs[2 * n_in + 5:2 * n_in + 9]
    base_scr = refs[2 * n_in + 9]
    tm = x_ref.shape[0]
    i = pl.program_id(0)

    @pl.when(i == 0)
    def _():
        base_scr[...] = jnp.zeros(base_scr.shape, F32)

    mix = _dot(a_refs[0][...], w_refs[0][...])
    for j in range(1, n_in):
        mix = mix + _dot(a_refs[j][...], w_refs[j][...])
    x1 = _layer_norm_rows(DEEPNORM_ALPHA * x_ref[...] + mix, g_ref[...], b_ref[...])
    x1_ref[...] = x1

    logits = _dot(x1.astype(BF16), wr_ref[...]) + br_ref[...]
    lane = lax.broadcasted_iota(jnp.int32, (tm, LANES), 1)
    lane_f = lane.astype(F32)
    work = logits
    cnt = jnp.zeros((tm, LANES), F32)
    tops, idxs, hots = [], [], []
    for _ in range(TOP_K):
        mx = jnp.max(work, axis=-1, keepdims=True)
        idx_f = jnp.min(jnp.where(work == mx, lane_f, float(LANES)), axis=-1, keepdims=True)
        idx = idx_f.astype(jnp.int32)
        hot = lane == idx
        work = jnp.where(hot, NEG_BIG * 2.0, work)
        cnt = cnt + jnp.where(hot, 1.0, 0.0)
        tops.append(mx)
        idxs.append(idx)
        hots.append(hot)
    exps = [jnp.exp(t - tops[0]) for t in tops]
    denom = exps[0] + exps[1] + exps[2] + exps[3]

    ri = lax.broadcasted_iota(jnp.int32, (tm, tm), 0)
    ci = lax.broadcasted_iota(jnp.int32, (tm, tm), 1)
    before = jnp.where(ri > ci, 1.0, 0.0).astype(BF16)
    rank = _dot(before, cnt.astype(BF16)) + base_scr[0:1, :]
    meta = jnp.zeros((tm, LANES), jnp.int32)
    gates = jnp.zeros((tm, LANES), F32)
    for kk in range(TOP_K):
        pos = jnp.sum(jnp.where(hots[kk], rank, 0.0), axis=-1, keepdims=True).astype(jnp.int32)
        meta = jnp.where(lane == kk, idxs[kk], meta)
        meta = jnp.where(lane == TOP_K + kk, pos, meta)
        gates = jnp.where(lane == kk, exps[kk] / denom, gates)
    meta_ref[...] = meta
    gate_ref[...] = gates
    total = base_scr[0:1, :] + jnp.sum(cnt, axis=0, keepdims=True)
    base_scr[...] = jnp.broadcast_to(total, base_scr.shape)
    cnt_ref[...] = jnp.broadcast_to(total, cnt_ref.shape)


def _proj_ln_router(a_list, w_list, x, ln_g, ln_b, w_r, b_r, tm):
    t = x.shape[0]
    n_in = len(a_list)
    in_specs = [pl.BlockSpec((tm, a.shape[1]), lambda i: (i, 0)) for a in a_list]
    in_specs += [pl.BlockSpec(w.shape, lambda i: (0, 0)) for w in w_list]
    in_specs += [pl.BlockSpec((tm, D_MODEL), lambda i: (i, 0)),
                 pl.BlockSpec((1, D_MODEL), lambda i: (0, 0)),
                 pl.BlockSpec((1, D_MODEL), lambda i: (0, 0)),
                 pl.BlockSpec((D_MODEL, LANES), lambda i: (0, 0)),
                 pl.BlockSpec((1, LANES), lambda i: (0, 0))]
    return pl.pallas_call(
        functools.partial(_proj_ln_router_kernel, n_in),
        grid=(t // tm,),
        in_specs=in_specs,
        out_specs=[pl.BlockSpec((tm, D_MODEL), lambda i: (i, 0)),
                   pl.BlockSpec((tm, LANES), lambda i: (i, 0)),
                   pl.BlockSpec((tm, LANES), lambda i: (i, 0)),
                   pl.BlockSpec((SUBLANES, LANES), lambda i: (0, 0))],
        out_shape=[jax.ShapeDtypeStruct((t, D_MODEL), F32),
                   jax.ShapeDtypeStruct((t, LANES), jnp.int32),
                   jax.ShapeDtypeStruct((t, LANES), F32),
                   jax.ShapeDtypeStruct((SUBLANES, LANES), F32)],
        scratch_shapes=[pltpu.VMEM((SUBLANES, LANES), F32)],
        compiler_params=_params("arbitrary"),
        name="proj_ln_router",
    )(*a_list, *w_list, x, ln_g, ln_b, w_r, b_r)


def _moe_kernel(blk_ref, exp_ref, lo_ref, hi_ref, x_ref, wg_ref, wu_ref, wd_ref, bg_ref, bu_ref, bd_ref, o_ref):
    i = pl.program_id(0)
    tm = x_ref.shape[0]
    blk = blk_ref[i]
    lo = lo_ref[i]
    hi = hi_ref[i]
    first = jnp.logical_or(i == 0, blk != blk_ref[jnp.maximum(i - 1, 0)])

    @pl.when(first)
    def _():
        o_ref[...] = jnp.zeros(o_ref.shape, o_ref.dtype)

    @pl.when(hi > lo)
    def _():
        x = x_ref[...]
        glu = jnp.minimum(_dot(x, wg_ref[0]) + bg_ref[0], SWIGLU_LIMIT)
        lin = jnp.clip(_dot(x, wu_ref[0]) + bu_ref[0], -SWIGLU_LIMIT, SWIGLU_LIMIT)
        act = glu * _sigmoid(SWIGLU_ALPHA * glu) * (lin + 1.0)
        y = _dot(act.astype(BF16), wd_ref[0]) + bd_ref[0]
        row = blk * tm + lax.broadcasted_iota(jnp.int32, (tm, 1), 0)
        keep = jnp.logical_and(row >= lo, row < hi)
        o_ref[...] = jnp.where(keep, y.astype(o_ref.dtype), o_ref[...])


def _moe(items, xs, wg, wu, wd, bg, bu, bd, tm):
    n_items = items[0].shape[0]
    n_rows = xs.shape[0]
    grid_spec = pltpu.PrefetchScalarGridSpec(
        num_scalar_prefetch=4,
        grid=(n_items,),
        in_specs=[pl.BlockSpec((tm, D_MODEL), lambda i, blk, ex, lo, hi: (blk[i], 0)),
                  pl.BlockSpec((1, D_MODEL, D_FF), lambda i, blk, ex, lo, hi: (ex[i], 0, 0)),
                  pl.BlockSpec((1, D_MODEL, D_FF), lambda i, blk, ex, lo, hi: (ex[i], 0, 0)),
                  pl.BlockSpec((1, D_FF, D_MODEL), lambda i, blk, ex, lo, hi: (ex[i], 0, 0)),
                  pl.BlockSpec((1, 1, D_FF), lambda i, blk, ex, lo, hi: (ex[i], 0, 0)),
                  pl.BlockSpec((1, 1, D_FF), lambda i, blk, ex, lo, hi: (ex[i], 0, 0)),
                  pl.BlockSpec((1, 1, D_MODEL), lambda i, blk, ex, lo, hi: (ex[i], 0, 0))],
        out_specs=pl.BlockSpec((tm, D_MODEL), lambda i, blk, ex, lo, hi: (blk[i], 0)),
    )
    return pl.pallas_call(
        _moe_kernel,
        grid_spec=grid_spec,
        out_shape=jax.ShapeDtypeStruct((n_rows, D_MODEL), BF16),
        compiler_params=_params("arbitrary"),
        name="moe_ffn",
    )(*items, xs, wg, wu, wd, bg, bu, bd)


def _moe_items(counts, n_rows, tm):
    n_blocks = n_rows // tm
    start = jnp.cumsum(counts) - counts
    cuts = jnp.sort(jnp.concatenate([jnp.arange(n_blocks, dtype=jnp.int32) * tm, start.astype(jnp.int32)]))
    lo = cuts
    hi = jnp.concatenate([cuts[1:], jnp.array([n_rows], jnp.int32)])
    blk = jnp.minimum(lo // tm, n_blocks - 1)
    ex = jnp.clip(jnp.searchsorted(start, lo, side="right") - 1, 0, N_EXPERTS - 1)
    return blk.astype(jnp.int32), ex.astype(jnp.int32), lo.astype(jnp.int32), hi.astype(jnp.int32)


def _combine_ln_kernel(x_ref, y_ref, gate_ref, g_ref, b_ref, o_ref):
    gates = gate_ref[...]
    ffn = y_ref[0].astype(F32) * gates[:, 0:1]
    for kk in range(1, TOP_K):
        ffn = ffn + y_ref[kk].astype(F32) * gates[:, kk:kk + 1]
    o_ref[...] = _layer_norm_rows(DEEPNORM_ALPHA * x_ref[...] + ffn, g_ref[...], b_ref[...])


def _combine_ln(x1, y4, gates, ln_g, ln_b, tm):
    t = x1.shape[0]
    return pl.pallas_call(
        _combine_ln_kernel,
        grid=(t // tm,),
        in_specs=[pl.BlockSpec((tm, D_MODEL), lambda i: (i, 0)),
                  pl.BlockSpec((TOP_K, tm, D_MODEL), lambda i: (0, i, 0)),
                  pl.BlockSpec((tm, LANES), lambda i: (i, 0)),
                  pl.BlockSpec((1, D_MODEL), lambda i: (0, 0)),
                  pl.BlockSpec((1, D_MODEL), lambda i: (0, 0))],
        out_specs=pl.BlockSpec((tm, D_MODEL), lambda i: (i, 0)),
        out_shape=jax.ShapeDtypeStruct((t, D_MODEL), F32),
        compiler_params=_params("parallel"),
        name="combine_ln",
    )(x1, y4, gates, ln_g, ln_b)


def _row_tile(t, pref):
    tm = pref
    while t % tm:
        tm //= 2
    return tm


def _moe_layer(x1, meta, gates, counts, ln_g, ln_b, wg, wu, wd, bg, bu, bd):
    t = x1.shape[0]
    n_rows = t * TOP_K
    tm_e = _row_tile(n_rows, 512)
    e = meta[:, 0:TOP_K]
    pos = meta[:, TOP_K:2 * TOP_K]
    cnt = counts[0, 0:N_EXPERTS].astype(jnp.int32)
    start = jnp.cumsum(cnt) - cnt
    dest = start[e] + pos
    order = jnp.argsort(e.reshape(-1), stable=True)
    xs = x1.astype(BF16)[order // TOP_K]
    items = _moe_items(cnt, n_rows, tm_e)
    ys = _moe(items, xs, wg, wu, wd, bg, bu, bd, tm_e)
    y4 = ys[dest.T]
    return _combine_ln(x1, y4, gates, ln_g, ln_b, _row_tile(t, 256))


def kernel(x, even_w_in, even_sgu_ln_g, even_sgu_ln_b, even_sgu_w, even_sgu_b, even_conv_w, even_a_log, even_dt_bias, even_gdn_norm, even_w_out, odd_w_in, odd_q_norm, odd_w_uq, odd_kv_norm, odd_w_ukv, odd_w_out, ln_mix_g, ln_mix_b, ln_ffn_g, ln_ffn_b, moe_w_router, moe_b_router, moe_w_gate_up, moe_b_gate_up, moe_w_down, moe_b_down):
    bsz, seq, d = x.shape
    t = bsz * seq
    xt = x.reshape(t, d)
    tm = _row_tile(t, 256)

    inv_freq = ROPE_THETA ** (-jnp.arange(0, C_ROPE, 2, dtype=F32) / C_ROPE)
    ang = jnp.arange(seq, dtype=F32)[:, None] * inv_freq[None, :]
    cs = jnp.concatenate([jnp.cos(ang), jnp.cos(ang), jnp.sin(ang), jnp.sin(ang)], axis=1)

    pos_chunk = np.arange(A_BLOCK) // CHUNK
    allowed = jnp.asarray(pos_chunk[None, :] <= pos_chunk[:, None])

    def rot_cols(w):
        half = C_ROPE // 2
        return jnp.concatenate([-w[..., half:], w[..., :half]], axis=-1)

    for layer in range(DEPTH):
        i = layer // 2
        if layer % 2 == 0:
            w_in = even_w_in[i]
            w_a = w_in[:, 0:2 * A_WIDTH].astype(BF16)
            w_b = w_in[:, 2 * A_WIDTH:2 * A_WIDTH + 4 * B_WIDTH].astype(BF16)
            w_ab = jnp.pad(w_in[:, 2 * A_WIDTH + 4 * B_WIDTH:], ((0, 0), (0, LANES - 2 * B_HEADS))).astype(BF16)
            sw = jnp.where(allowed[None], even_sgu_w[i], 0).astype(BF16)
            sb = jnp.broadcast_to(even_sgu_b[i][:, :, None], (A_GROUPS, A_BLOCK, A_GROUP_DIM)).astype(F32)
            y_a = _sgu(xt, w_a, even_sgu_ln_g[i][None], even_sgu_ln_b[i][None], sw, sb, tm)
            h_b = _mm(xt, w_b, BF16, tm)
            ab = _mm(xt, w_ab, F32, tm)
            abt = ab.reshape(bsz, seq, LANES)[:, :, 0:SUBLANES].transpose(0, 2, 1)
            hp = jnp.pad(jnp.stack([even_a_log[i], even_dt_bias[i]]), ((0, 0), (0, LANES - B_HEADS)))
            hpt = jnp.pad(jnp.stack([even_a_log[i], even_dt_bias[i]], axis=1), ((0, SUBLANES - B_HEADS), (0, 0)))
            y_b = _gdn(h_b, ab, abt, even_conv_w[i], hp, hpt, even_gdn_norm[i][None], bsz, seq, _row_tile(seq, 256))
            w_out = even_w_out[i].astype(BF16)
            a_list = [y_a, y_b]
            w_list = [w_out[0:A_WIDTH], w_out[A_WIDTH:]]
        else:
            w_in = odd_w_in[i]
            w_pe = w_in[:, Q_LORA + KV_LORA:]
            w_in2 = jnp.concatenate([w_in, rot_cols(w_pe)], axis=1).astype(BF16)
            w_uq = odd_w_uq[i].reshape(Q_LORA, C_HEADS, C_QK)
            w_uq2 = jnp.concatenate([w_uq, rot_cols(w_uq[..., C_NOPE:])], axis=-1)
            w_uq2 = w_uq2.reshape(Q_LORA, C_HEADS * (C_NOPE + 2 * C_ROPE)).astype(BF16)
            h = _mm(xt, w_in2, BF16, tm)
            q, k, v = _mla_up(h, odd_q_norm[i][None], odd_kv_norm[i][None], w_uq2, odd_w_ukv[i].astype(BF16), cs,
                              bsz, seq, _row_tile(seq, 256))
            o = _attn(q, k, v, _row_tile(seq, 256))
            a_list = [o.reshape(t, C_HEADS * C_V)]
            w_list = [odd_w_out[i].astype(BF16)]

        w_r = jnp.pad(moe_w_router[layer], ((0, 0), (0, LANES - N_EXPERTS))).astype(BF16)
        b_r = jnp.pad(moe_b_router[layer], (0, LANES - N_EXPERTS), constant_values=NEG_BIG)[None]
        x1, meta, gates, counts = _proj_ln_router(a_list, w_list, xt, ln_mix_g[layer][None], ln_mix_b[layer][None],
                                                  w_r, b_r, tm)
        wgu = moe_w_gate_up[layer]
        bgu = moe_b_gate_up[layer]
        xt = _moe_layer(x1, meta, gates, counts, ln_ffn_g[layer][None], ln_ffn_b[layer][None],
                        wgu[:, :, 0::2].astype(BF16), wgu[:, :, 1::2].astype(BF16), moe_w_down[layer].astype(BF16),
                        bgu[:, None, 0::2], bgu[:, None, 1::2], moe_b_down[layer][:, None, :])
    return xt.reshape(bsz, seq, d)
```

```python
import functools

import jax
import jax.numpy as jnp
import numpy as np
from jax import lax
from jax.experimental import pallas as pl
from jax.experimental.pallas import tpu as pltpu

F32 = jnp.float32
BF16 = jnp.bfloat16

D_MODEL = 1024
DEPTH = 4
CHUNK = 64
CHUNK_SHIFT = CHUNK.bit_length() - 1
DEEPNORM_ALPHA = (2 * DEPTH) ** 0.25
LN_EPS = 1e-5
RMS_EPS = 1e-6

A_BLOCK = 128
A_GROUPS = 4
A_WIDTH = D_MODEL // 2
A_GROUP_DIM = A_WIDTH // A_GROUPS

B_HEADS = 4
B_HEAD_DIM = D_MODEL // 8
B_WIDTH = B_HEADS * B_HEAD_DIM
CONV_K = 4

C_HEADS = 8
C_NOPE = 128
C_ROPE = 64
C_V = D_MODEL // C_HEADS
C_QK = C_NOPE + C_ROPE
Q_LORA = 3 * D_MODEL // 8
KV_LORA = D_MODEL // 4
ROPE_THETA = 10000.0

N_EXPERTS = 32
TOP_K = 4
D_FF = D_MODEL
SWIGLU_LIMIT = 7.0
SWIGLU_ALPHA = 1.702

LANES = 128
SUBLANES = 8
VMEM_LIMIT = 48 * 1024 * 1024
MOE_VMEM_LIMIT = 58 * 1024 * 1024
NEG_BIG = -1e30
LOG2_E = 1.4426950408889634

ROWS = 256
GDN_ROWS = 256
ATTN_ROWS = 512
ATTN_HEADS_PER_STEP = 2
MOE_ROWS = 512
MOE_PREP_COLS = 512


def _row_tile(t, pref):
    tm = pref
    while t % tm:
        tm //= 2
    return tm


def _params(*sem, vmem=VMEM_LIMIT):
    return pltpu.CompilerParams(dimension_semantics=sem, vmem_limit_bytes=vmem)


def _dot(a, b):
    return jnp.dot(a, b, preferred_element_type=F32)


def _dot_nt(a, b):
    return lax.dot_general(a, b, (((1,), (1,)), ((), ())), preferred_element_type=F32)


def _dot_tn(a, b):
    return lax.dot_general(a, b, (((0,), (0,)), ((), ())), preferred_element_type=F32)


def _sigmoid(x):
    return 1.0 / (1.0 + jnp.exp(-x))


def _softplus(x):
    return jnp.maximum(x, 0.0) + jnp.log1p(jnp.exp(-jnp.abs(x)))


def _layer_norm_rows(y, g, b):
    mu = jnp.mean(y, axis=-1, keepdims=True)
    d = y - mu
    var = jnp.mean(d * d, axis=-1, keepdims=True)
    return d * lax.rsqrt(var + LN_EPS) * g + b


def _mm_kernel(x_ref, w_ref, o_ref):
    o_ref[...] = _dot(x_ref[...].astype(BF16), w_ref[...]).astype(o_ref.dtype)


def _mm(x, w, out_dtype, tm):
    t, k = x.shape
    n = w.shape[1]
    return pl.pallas_call(
        _mm_kernel,
        grid=(t // tm,),
        in_specs=[pl.BlockSpec((tm, k), lambda i: (i, 0)),
                  pl.BlockSpec((k, n), lambda i: (0, 0))],
        out_specs=pl.BlockSpec((tm, n), lambda i: (i, 0)),
        out_shape=jax.ShapeDtypeStruct((t, n), out_dtype),
        compiler_params=_params("parallel"),
        name="mm",
    )(x, w)


def _gelu(x):
    return 0.5 * x * (1.0 + lax.erf(x * (2.0 ** -0.5)))


def _sgu_kernel(x_ref, w_ref, g_ref, b_ref, sw_ref, sb_ref, o_ref):
    tm = x_ref.shape[0]
    h = _dot(x_ref[...].astype(BF16), w_ref[...])
    for grp in range(A_GROUPS):
        c0 = grp * A_GROUP_DIM
        u = _gelu(h[:, c0:c0 + A_GROUP_DIM])
        v = _gelu(h[:, A_WIDTH + c0:A_WIDTH + c0 + A_GROUP_DIM])
        vn = _layer_norm_rows(v, g_ref[:, c0:c0 + A_GROUP_DIM], b_ref[:, c0:c0 + A_GROUP_DIM]).astype(BF16)
        for blk in range(tm // A_BLOCK):
            r0 = blk * A_BLOCK
            mixed = _dot(sw_ref[grp], vn[r0:r0 + A_BLOCK]) + sb_ref[grp]
            o_ref[r0:r0 + A_BLOCK, c0:c0 + A_GROUP_DIM] = (u[r0:r0 + A_BLOCK] * mixed).astype(o_ref.dtype)


def _sgu(x, w_a, ln_g, ln_b, sw, sb, tm):
    t = x.shape[0]
    return pl.pallas_call(
        _sgu_kernel,
        grid=(t // tm,),
        in_specs=[pl.BlockSpec((tm, D_MODEL), lambda i: (i, 0)),
                  pl.BlockSpec((D_MODEL, 2 * A_WIDTH), lambda i: (0, 0)),
                  pl.BlockSpec((1, A_WIDTH), lambda i: (0, 0)),
                  pl.BlockSpec((1, A_WIDTH), lambda i: (0, 0)),
                  pl.BlockSpec((A_GROUPS, A_BLOCK, A_BLOCK), lambda i: (0, 0, 0)),
                  pl.BlockSpec((A_GROUPS, A_BLOCK, A_GROUP_DIM), lambda i: (0, 0, 0))],
        out_specs=pl.BlockSpec((tm, A_WIDTH), lambda i: (i, 0)),
        out_shape=jax.ShapeDtypeStruct((t, A_WIDTH), BF16),
        compiler_params=_params("parallel"),
        name="sgu",
    )(x, w_a, ln_g, ln_b, sw, sb)


def _gdn_kernel(h_ref, ab_ref, abt_ref, cw_ref, hp_ref, hpt_ref, nw_ref, o_ref, conv_scr, state_scr):
    tr = h_ref.shape[0]
    n_chunks = tr // CHUNK
    s_idx = pl.program_id(1)

    @pl.when(s_idx == 0)
    def _():
        conv_scr[0:SUBLANES, :] = jnp.zeros((SUBLANES, 3 * B_WIDTH), F32)
        state_scr[...] = jnp.zeros(state_scr.shape, F32)

    conv_scr[SUBLANES:SUBLANES + tr, :] = h_ref[:, 0:3 * B_WIDTH].astype(F32)
    acc = conv_scr[SUBLANES:SUBLANES + tr, :] * cw_ref[CONV_K - 1:CONV_K, :]
    for d in range(1, CONV_K):
        acc = acc + conv_scr[SUBLANES - d:SUBLANES - d + tr, :] * cw_ref[CONV_K - 1 - d:CONV_K - d, :]
    conv_scr[0:SUBLANES, :] = conv_scr[tr:tr + SUBLANES, :]
    qkv = acc * _sigmoid(acc)

    ab = ab_ref[...]
    sp_c = _softplus(ab + hp_ref[1:2, :])
    g_col = -jnp.exp(hp_ref[0:1, :]) * sp_c
    beta_col = _sigmoid(ab)
    abt = abt_ref[0]
    sp_r = _softplus(abt + hpt_ref[:, 1:2])
    g_row = -jnp.exp(hpt_ref[:, 0:1]) * sp_r

    ri = lax.broadcasted_iota(jnp.int32, (tr, tr), 0)
    ci = lax.broadcasted_iota(jnp.int32, (tr, tr), 1)
    same = jnp.right_shift(ri, CHUNK_SHIFT) == jnp.right_shift(ci, CHUNK_SHIFT)
    lower = jnp.where(same & (ri >= ci), 1.0, 0.0).astype(F32)
    upper = jnp.where(same & (ri <= ci), 1.0, 0.0).astype(F32)
    gc_col = jnp.dot(lower, g_col, preferred_element_type=F32, precision=lax.Precision.HIGHEST)
    gc_row = jnp.dot(g_row, upper, preferred_element_type=F32, precision=lax.Precision.HIGHEST)

    incl = same & (ri >= ci)
    strict = same & (ri > ci)
    eye = jnp.where(ri == ci, 1.0, 0.0).astype(F32)
    n_sq = CHUNK_SHIFT - 1

    for hd in range(B_HEADS):
        l0 = hd * B_HEAD_DIM
        q = qkv[:, l0:l0 + B_HEAD_DIM]
        k = qkv[:, B_WIDTH + l0:B_WIDTH + l0 + B_HEAD_DIM]
        v = qkv[:, 2 * B_WIDTH + l0:2 * B_WIDTH + l0 + B_HEAD_DIM]
        q = q * lax.rsqrt(jnp.sum(q * q, axis=-1, keepdims=True) + RMS_EPS) * (B_HEAD_DIM ** -0.5)
        k = k * lax.rsqrt(jnp.sum(k * k, axis=-1, keepdims=True) + RMS_EPS)
        beta = beta_col[:, B_HEADS + hd:B_HEADS + hd + 1]
        gcc = gc_col[:, hd:hd + 1]
        gcr = gc_row[hd:hd + 1, :]
        decay = jnp.where(incl, jnp.exp(jnp.where(incl, gcc - gcr, 0.0)), 0.0)
        k_b = k.astype(BF16)
        kb = k * beta
        a_mat = jnp.where(strict, _dot_nt(kb.astype(BF16), k_b) * decay, 0.0)
        p = -a_mat
        t_mat = eye + p
        for _ in range(n_sq):
            pb = p.astype(BF16)
            p = _dot(pb, pb)
            t_mat = t_mat + _dot(t_mat.astype(BF16), p.astype(BF16))
        t_b = t_mat.astype(BF16)
        e_gc = jnp.exp(gcc)
        u = _dot(t_b, (v * beta).astype(BF16))
        w = _dot(t_b, (kb * e_gc).astype(BF16)).astype(BF16)
        intra = jnp.where(incl, _dot_nt(q.astype(BF16), k_b) * decay, 0.0).astype(BF16)
        q_dec = (q * e_gc).astype(BF16)
        for c in range(n_chunks):
            r0 = c * CHUNK
            g_last = gcc[r0 + CHUNK - 1:r0 + CHUNK, :]
            k_dec = (k[r0:r0 + CHUNK] * jnp.exp(g_last - gcc[r0:r0 + CHUNK])).astype(BF16)
            state = state_scr[hd]
            state_b = state.astype(BF16)
            v_new = (u[r0:r0 + CHUNK] - _dot(w[r0:r0 + CHUNK], state_b)).astype(BF16)
            out = _dot(q_dec[r0:r0 + CHUNK], state_b) + _dot(intra[r0:r0 + CHUNK, r0:r0 + CHUNK], v_new)
            state_scr[hd] = state * jnp.exp(g_last) + _dot_tn(k_dec, v_new)
            z = h_ref[r0:r0 + CHUNK, 3 * B_WIDTH + l0:3 * B_WIDTH + l0 + B_HEAD_DIM].astype(F32)
            o_n = out * lax.rsqrt(jnp.mean(out * out, axis=-1, keepdims=True) + RMS_EPS) * nw_ref[...]
            o_ref[r0:r0 + CHUNK, l0:l0 + B_HEAD_DIM] = (o_n * (z * _sigmoid(z))).astype(o_ref.dtype)


def _gdn(h_b, ab, abt, conv_w, hp, hpt, norm_w, bsz, seq):
    tr = _row_tile(seq, GDN_ROWS)
    n_s = seq // tr
    return pl.pallas_call(
        _gdn_kernel,
        grid=(bsz, n_s),
        in_specs=[pl.BlockSpec((tr, 4 * B_WIDTH), lambda b, s: (b * n_s + s, 0)),
                  pl.BlockSpec((tr, LANES), lambda b, s: (b * n_s + s, 0)),
                  pl.BlockSpec((1, SUBLANES, tr), lambda b, s: (b, 0, s)),
                  pl.BlockSpec((CONV_K, 3 * B_WIDTH), lambda b, s: (0, 0)),
                  pl.BlockSpec((2, LANES), lambda b, s: (0, 0)),
                  pl.BlockSpec((SUBLANES, 2), lambda b, s: (0, 0)),
                  pl.BlockSpec((1, B_HEAD_DIM), lambda b, s: (0, 0))],
        out_specs=pl.BlockSpec((tr, B_WIDTH), lambda b, s: (b * n_s + s, 0)),
        out_shape=jax.ShapeDtypeStruct((bsz * seq, B_WIDTH), BF16),
        scratch_shapes=[pltpu.VMEM((tr + 2 * SUBLANES, 3 * B_WIDTH), F32),
                        pltpu.VMEM((B_HEADS, B_HEAD_DIM, B_HEAD_DIM), F32)],
        compiler_params=_params("parallel", "arbitrary"),
        name="gdn",
    )(h_b, ab, abt, conv_w, hp, hpt, norm_w)


def _mla_up_kernel(h_ref, qn_ref, kvn_ref, wq_ref, wkv_ref, cs_ref, q_ref, k_ref, v_ref):
    h = h_ref[...].astype(F32)
    cs = cs_ref[...]

    def rms(x, g):
        return x * lax.rsqrt(jnp.mean(x * x, axis=-1, keepdims=True) + RMS_EPS) * g

    def rope(t):
        r = t * cs
        return r + pltpu.roll(r, C_ROPE, 1)

    cq = rms(h[:, 0:Q_LORA], qn_ref[...]).astype(BF16)
    ckv = rms(h[:, Q_LORA:Q_LORA + KV_LORA], kvn_ref[...]).astype(BF16)
    k_pe = rope(h[:, Q_LORA + KV_LORA:Q_LORA + KV_LORA + 2 * C_ROPE])[:, 0:C_ROPE].astype(BF16)
    qf = _dot(cq, wq_ref[...]) * (C_QK ** -0.5 * LOG2_E)
    kv = _dot(ckv, wkv_ref[...])
    for hd in range(C_HEADS):
        q0 = hd * (C_NOPE + 2 * C_ROPE)
        q_ref[0, hd, :, 0:C_NOPE] = qf[:, q0:q0 + C_NOPE].astype(BF16)
        q_ref[0, hd, :, C_NOPE:C_QK] = rope(qf[:, q0 + C_NOPE:q0 + C_NOPE + 2 * C_ROPE])[:, 0:C_ROPE].astype(BF16)
        k0 = hd * (C_NOPE + C_V)
        k_ref[0, hd, :, 0:C_NOPE] = kv[:, k0:k0 + C_NOPE].astype(BF16)
        k_ref[0, hd, :, C_NOPE:C_QK] = k_pe
        v_ref[0, hd, :, :] = kv[:, k0 + C_NOPE:k0 + C_NOPE + C_V].astype(BF16)


def _mla_up(h, q_norm, kv_norm, w_uq, w_ukv, cs, bsz, seq, tm):
    n_s = seq // tm
    hw = h.shape[1]
    qk_shape = jax.ShapeDtypeStruct((bsz, C_HEADS, seq, C_QK), BF16)
    return pl.pallas_call(
        _mla_up_kernel,
        grid=(bsz, n_s),
        in_specs=[pl.BlockSpec((tm, hw), lambda b, s: (b * n_s + s, 0)),
                  pl.BlockSpec((1, Q_LORA), lambda b, s: (0, 0)),
                  pl.BlockSpec((1, KV_LORA), lambda b, s: (0, 0)),
                  pl.BlockSpec(w_uq.shape, lambda b, s: (0, 0)),
                  pl.BlockSpec(w_ukv.shape, lambda b, s: (0, 0)),
                  pl.BlockSpec((tm, LANES), lambda b, s: (s, 0))],
        out_specs=[pl.BlockSpec((1, C_HEADS, tm, C_QK), lambda b, s: (b, 0, s, 0)),
                   pl.BlockSpec((1, C_HEADS, tm, C_QK), lambda b, s: (b, 0, s, 0)),
                   pl.BlockSpec((1, C_HEADS, tm, C_V), lambda b, s: (b, 0, s, 0))],
        out_shape=[qk_shape, qk_shape, jax.ShapeDtypeStruct((bsz, C_HEADS, seq, C_V), BF16)],
        compiler_params=_params("parallel", "parallel"),
        name="mla_up",
    )(h, q_norm, kv_norm, w_uq, w_ukv, cs)


def _attn_kernel(q_ref, k_ref, v_ref, o_ref):
    nh, tq = q_ref.shape[1], q_ref.shape[2]
    qi = pl.program_id(2)

    def step(hd, k0, carry, mask):
        m, l, acc = carry
        s = _dot_nt(q_ref[0, hd], k_ref[0, hd, pl.ds(k0, tq), :])
        if mask is not None:
            s = jnp.where(mask, s, NEG_BIG)
        m_new = jnp.maximum(m, jnp.max(s, axis=-1, keepdims=True))
        alpha = jnp.exp2(m - m_new)
        p = jnp.exp2(s - m_new)
        l = alpha * l + jnp.sum(p, axis=-1, keepdims=True)
        acc = alpha * acc + _dot(p.astype(BF16), v_ref[0, hd, pl.ds(k0, tq), :])
        return m_new, l, acc

    def body(j, carry):
        k0 = pl.multiple_of(j * tq, tq)
        return tuple(step(hd, k0, carry[hd], None) for hd in range(nh))

    init = tuple((jnp.full((tq, 1), NEG_BIG, F32), jnp.zeros((tq, 1), F32), jnp.zeros((tq, C_V), F32))
                 for _ in range(nh))
    carry = lax.fori_loop(0, qi, body, init)
    k0 = pl.multiple_of(qi * tq, tq)
    ri = jnp.right_shift(lax.broadcasted_iota(jnp.int32, (tq, tq), 0), CHUNK_SHIFT)
    ci = jnp.right_shift(lax.broadcasted_iota(jnp.int32, (tq, tq), 1), CHUNK_SHIFT)
    mask = ci <= ri
    for hd in range(nh):
        _, l, acc = step(hd, k0, carry[hd], mask)
        o_ref[0, :, hd * C_V:(hd + 1) * C_V] = (acc / l).astype(o_ref.dtype)


def _attn(q, k, v):
    bsz, _, seq, _ = q.shape
    tq = _row_tile(seq, ATTN_ROWS)
    nh = ATTN_HEADS_PER_STEP
    return pl.pallas_call(
        _attn_kernel,
        grid=(bsz, C_HEADS // nh, seq // tq),
        in_specs=[pl.BlockSpec((1, nh, tq, C_QK), lambda b, h, i: (b, h, i, 0)),
                  pl.BlockSpec((1, nh, seq, C_QK), lambda b, h, i: (b, h, 0, 0)),
                  pl.BlockSpec((1, nh, seq, C_V), lambda b, h, i: (b, h, 0, 0))],
        out_specs=pl.BlockSpec((1, tq, nh * C_V), lambda b, h, i: (b, i, h)),
        out_shape=jax.ShapeDtypeStruct((bsz, seq, C_HEADS * C_V), BF16),
        compiler_params=_params("parallel", "parallel", "arbitrary"),
        name="attn",
    )(q, k, v)


def _proj_ln_router_kernel(n_in, *refs):
    a_refs = refs[0:n_in]
    w_refs = refs[n_in:2 * n_in]
    x_ref, g_ref, b_ref, wr_ref, br_ref = refs[2 * n_in:2 * n_in + 5]
    x1_ref, meta_ref, gate_ref, cnt_ref = refs[2 * n_in + 5:2 * n_in + 9]
    base_scr = refs[2 * n_in + 9]
    tm = x_ref.shape[0]
    i = pl.program_id(0)

    @pl.when(i == 0)
    def _():
        base_scr[...] = jnp.zeros(base_scr.shape, F32)

    mix = _dot(a_refs[0][...], w_refs[0][...])
    for j in range(1, n_in):
        mix = mix + _dot(a_refs[j][...], w_refs[j][...])
    x1 = _layer_norm_rows(DEEPNORM_ALPHA * x_ref[...] + mix, g_ref[...], b_ref[...])
    x1_ref[...] = x1

    logits = _dot(x1.astype(BF16), wr_ref[...]) + br_ref[...]
    lane = lax.broadcasted_iota(jnp.int32, (tm, LANES), 1)
    lane_f = lane.astype(F32)
    work = logits
    cnt = jnp.zeros((tm, LANES), F32)
    tops, idxs, hots = [], [], []
    for _ in range(TOP_K):
        mx = jnp.max(work, axis=-1, keepdims=True)
        idx_f = jnp.min(jnp.where(work == mx, lane_f, float(LANES)), axis=-1, keepdims=True)
        idx = idx_f.astype(jnp.int32)
        hot = lane == idx
        work = jnp.where(hot, NEG_BIG * 2.0, work)
        cnt = cnt + jnp.where(hot, 1.0, 0.0)
        tops.append(mx)
        idxs.append(idx)
        hots.append(hot)
    exps = [jnp.exp(t - tops[0]) for t in tops]
    denom = exps[0] + exps[1] + exps[2] + exps[3]

    ri = lax.broadcasted_iota(jnp.int32, (tm, tm), 0)
    ci = lax.broadcasted_iota(jnp.int32, (tm, tm), 1)
    before = jnp.where(ri > ci, 1.0, 0.0).astype(BF16)
    rank = _dot(before, cnt.astype(BF16)) + base_scr[0:1, :]
    meta = jnp.zeros((tm, LANES), jnp.int32)
    gates = jnp.zeros((tm, LANES), F32)
    for kk in range(TOP_K):
        pos = jnp.sum(jnp.where(hots[kk], rank, 0.0), axis=-1, keepdims=True).astype(jnp.int32)
        meta = jnp.where(lane == kk, idxs[kk], meta)
        meta = jnp.where(lane == TOP_K + kk, pos, meta)
        gates = jnp.where(lane == kk, exps[kk] / denom, gates)
    meta_ref[...] = meta
    gate_ref[...] = gates
    total = base_scr[0:1, :] + jnp.sum(cnt, axis=0, keepdims=True)
    base_scr[...] = jnp.broadcast_to(total, base_scr.shape)
    cnt_ref[...] = jnp.broadcast_to(total, cnt_ref.shape)


def _proj_ln_router(a_list, w_list, x, ln_g, ln_b, w_r, b_r, tm):
    t = x.shape[0]
    n_in = len(a_list)
    in_specs = [pl.BlockSpec((tm, a.shape[1]), lambda i: (i, 0)) for a in a_list]
    in_specs += [pl.BlockSpec(w.shape, lambda i: (0, 0)) for w in w_list]
    in_specs += [pl.BlockSpec((tm, D_MODEL), lambda i: (i, 0)),
                 pl.BlockSpec((1, D_MODEL), lambda i: (0, 0)),
                 pl.BlockSpec((1, D_MODEL), lambda i: (0, 0)),
                 pl.BlockSpec((D_MODEL, LANES), lambda i: (0, 0)),
                 pl.BlockSpec((1, LANES), lambda i: (0, 0))]
    return pl.pallas_call(
        functools.partial(_proj_ln_router_kernel, n_in),
        grid=(t // tm,),
        in_specs=in_specs,
        out_specs=[pl.BlockSpec((tm, D_MODEL), lambda i: (i, 0)),
                   pl.BlockSpec((tm, LANES), lambda i: (i, 0)),
                   pl.BlockSpec((tm, LANES), lambda i: (i, 0)),
                   pl.BlockSpec((SUBLANES, LANES), lambda i: (0, 0))],
        out_shape=[jax.ShapeDtypeStruct((t, D_MODEL), F32),
                   jax.ShapeDtypeStruct((t, LANES), jnp.int32),
                   jax.ShapeDtypeStruct((t, LANES), F32),
                   jax.ShapeDtypeStruct((SUBLANES, LANES), F32)],
        scratch_shapes=[pltpu.VMEM((SUBLANES, LANES), F32)],
        compiler_params=_params("arbitrary"),
        name="proj_ln_router",
    )(*a_list, *w_list, x, ln_g, ln_b, w_r, b_r)


def _moe_kernel(blk_ref, exp_ref, lo_ref, hi_ref, x_ref, wgu_ref, wd_ref, bg_ref, bu_ref, bd_ref, o_ref,
                wg_scr, wu_scr, wd_scr, t_scr):
    i = pl.program_id(0)
    tm = x_ref.shape[0]
    blk = blk_ref[i]
    lo = lo_ref[i]
    hi = hi_ref[i]
    prev = jnp.maximum(i - 1, 0)
    first = jnp.logical_or(i == 0, blk != blk_ref[prev])
    new_expert = jnp.logical_or(i == 0, exp_ref[i] != exp_ref[prev])

    @pl.when(new_expert)
    def _():
        pc = t_scr.shape[1]
        half = pc // 2
        for c in range(2 * D_FF // pc):
            for kc in range(D_MODEL // LANES):
                k0 = kc * LANES
                t_scr[kc] = wgu_ref[0, 0, k0:k0 + LANES, c * pc:(c + 1) * pc].T
                wg_scr[c * half:(c + 1) * half, k0:k0 + LANES] = t_scr[kc, pl.ds(0, half, stride=2), :].astype(BF16)
                wu_scr[c * half:(c + 1) * half, k0:k0 + LANES] = t_scr[kc, pl.ds(1, half, stride=2), :].astype(BF16)
        wd_scr[...] = wd_ref[0, 0].astype(BF16)

    @pl.when(first)
    def _():
        o_ref[...] = jnp.zeros(o_ref.shape, o_ref.dtype)

    @pl.when(hi > lo)
    def _():
        x = x_ref[...]
        glu = jnp.minimum(_dot_nt(x, wg_scr[...]) + bg_ref[0, 0], SWIGLU_LIMIT)
        lin = jnp.clip(_dot_nt(x, wu_scr[...]) + bu_ref[0, 0], -SWIGLU_LIMIT, SWIGLU_LIMIT)
        act = glu * _sigmoid(SWIGLU_ALPHA * glu) * (lin + 1.0)
        y = _dot(act.astype(BF16), wd_scr[...]) + bd_ref[0, 0]
        row = blk * tm + lax.broadcasted_iota(jnp.int32, (tm, 1), 0)
        keep = jnp.logical_and(row >= lo, row < hi)
        o_ref[...] = jnp.where(keep, y.astype(o_ref.dtype), o_ref[...])


def _moe_sorted(layer, xs, counts, w_gate_up, b_gate, b_up, w_down, b_down):
    n_rows = xs.shape[0]
    tm = _row_tile(n_rows, MOE_ROWS)
    items = _moe_items(counts, n_rows, tm)
    n_items = items[0].shape[0]

    def by_expert(i, blk, ex, lo, hi):
        return (layer, ex[i], 0, 0)

    def by_block(i, blk, ex, lo, hi):
        return (blk[i], 0)

    grid_spec = pltpu.PrefetchScalarGridSpec(
        num_scalar_prefetch=4,
        grid=(n_items,),
        in_specs=[pl.BlockSpec((tm, D_MODEL), by_block),
                  pl.BlockSpec((1, 1, D_MODEL, 2 * D_FF), by_expert),
                  pl.BlockSpec((1, 1, D_FF, D_MODEL), by_expert),
                  pl.BlockSpec((1, 1, 1, D_FF), by_expert),
                  pl.BlockSpec((1, 1, 1, D_FF), by_expert),
                  pl.BlockSpec((1, 1, 1, D_MODEL), by_expert)],
        out_specs=pl.BlockSpec((tm, D_MODEL), by_block),
        scratch_shapes=[pltpu.VMEM((D_FF, D_MODEL), BF16),
                        pltpu.VMEM((D_FF, D_MODEL), BF16),
                        pltpu.VMEM((D_FF, D_MODEL), BF16),
                        pltpu.VMEM((D_MODEL // LANES, MOE_PREP_COLS, LANES), F32)],
    )
    return pl.pallas_call(
        _moe_kernel,
        grid_spec=grid_spec,
        out_shape=jax.ShapeDtypeStruct((n_rows, D_MODEL), BF16),
        compiler_params=_params("arbitrary", vmem=MOE_VMEM_LIMIT),
        name="moe_ffn",
    )(*items, xs, w_gate_up, w_down, b_gate, b_up, b_down)


def _moe_items(counts, n_rows, tm):
    n_blocks = n_rows // tm
    start = jnp.cumsum(counts) - counts
    cuts = jnp.sort(jnp.concatenate([jnp.arange(n_blocks, dtype=jnp.int32) * tm, start.astype(jnp.int32)]))
    lo = cuts
    hi = jnp.concatenate([cuts[1:], jnp.array([n_rows], jnp.int32)])
    blk = jnp.minimum(lo // tm, n_blocks - 1)
    ex = jnp.clip(jnp.searchsorted(start, lo, side="right") - 1, 0, N_EXPERTS - 1)
    return blk.astype(jnp.int32), ex.astype(jnp.int32), lo.astype(jnp.int32), hi.astype(jnp.int32)


def _combine_ln_kernel(x_ref, y_ref, gate_ref, g_ref, b_ref, o_ref):
    gates = gate_ref[...]
    ffn = y_ref[0].astype(F32) * gates[:, 0:1]
    for kk in range(1, TOP_K):
        ffn = ffn + y_ref[kk].astype(F32) * gates[:, kk:kk + 1]
    o_ref[...] = _layer_norm_rows(DEEPNORM_ALPHA * x_ref[...] + ffn, g_ref[...], b_ref[...])


def _combine_ln(x1, y4, gates, ln_g, ln_b, tm):
    t = x1.shape[0]
    return pl.pallas_call(
        _combine_ln_kernel,
        grid=(t // tm,),
        in_specs=[pl.BlockSpec((tm, D_MODEL), lambda i: (i, 0)),
                  pl.BlockSpec((TOP_K, tm, D_MODEL), lambda i: (0, i, 0)),
                  pl.BlockSpec((tm, LANES), lambda i: (i, 0)),
                  pl.BlockSpec((1, D_MODEL), lambda i: (0, 0)),
                  pl.BlockSpec((1, D_MODEL), lambda i: (0, 0))],
        out_specs=pl.BlockSpec((tm, D_MODEL), lambda i: (i, 0)),
        out_shape=jax.ShapeDtypeStruct((t, D_MODEL), F32),
        compiler_params=_params("parallel"),
        name="combine_ln",
    )(x1, y4, gates, ln_g, ln_b)


def _moe_layer(layer, x1, meta, gates, counts, ln_g, ln_b, w_gate_up, b_gate, b_up, w_down, b_down):
    t = x1.shape[0]
    e = meta[:, 0:TOP_K]
    pos = meta[:, TOP_K:2 * TOP_K]
    cnt = counts[0, 0:N_EXPERTS].astype(jnp.int32)
    start = jnp.cumsum(cnt) - cnt
    dest = start[e] + pos
    order = jnp.argsort(e.reshape(-1), stable=True)
    xs = x1.astype(BF16)[order // TOP_K]
    ys = _moe_sorted(layer, xs, cnt, w_gate_up, b_gate, b_up, w_down, b_down)
    y4 = ys[dest.T]
    return _combine_ln(x1, y4, gates, ln_g, ln_b, _row_tile(t, ROWS))


def kernel(x, even_w_in, even_sgu_ln_g, even_sgu_ln_b, even_sgu_w, even_sgu_b, even_conv_w, even_a_log, even_dt_bias, even_gdn_norm, even_w_out, odd_w_in, odd_q_norm, odd_w_uq, odd_kv_norm, odd_w_ukv, odd_w_out, ln_mix_g, ln_mix_b, ln_ffn_g, ln_ffn_b, moe_w_router, moe_b_router, moe_w_gate_up, moe_b_gate_up, moe_w_down, moe_b_down):
    bsz, seq, d = x.shape
    t = bsz * seq
    xt = x.reshape(t, d)
    tm = _row_tile(t, ROWS)
    b_gate = moe_b_gate_up[:, :, None, 0::2]
    b_up = moe_b_gate_up[:, :, None, 1::2]
    b_down = moe_b_down[:, :, None, :]

    inv_freq = ROPE_THETA ** (-jnp.arange(0, C_ROPE, 2, dtype=F32) / C_ROPE)
    ang = jnp.arange(seq, dtype=F32)[:, None] * inv_freq[None, :]
    cs = jnp.concatenate([jnp.cos(ang), jnp.cos(ang), jnp.sin(ang), jnp.sin(ang)], axis=1)

    pos_chunk = np.arange(A_BLOCK) // CHUNK
    allowed = jnp.asarray(pos_chunk[None, :] <= pos_chunk[:, None])

    def rot_cols(w):
        half = C_ROPE // 2
        return jnp.concatenate([-w[..., half:], w[..., :half]], axis=-1)

    for layer in range(DEPTH):
        i = layer // 2
        if layer % 2 == 0:
            w_in = even_w_in[i]
            w_a = w_in[:, 0:2 * A_WIDTH].astype(BF16)
            w_b = w_in[:, 2 * A_WIDTH:2 * A_WIDTH + 4 * B_WIDTH].astype(BF16)
            w_ab = jnp.pad(w_in[:, 2 * A_WIDTH + 4 * B_WIDTH:], ((0, 0), (0, LANES - 2 * B_HEADS))).astype(BF16)
            sw = jnp.where(allowed[None], even_sgu_w[i], 0).astype(BF16)
            sb = jnp.broadcast_to(even_sgu_b[i][:, :, None], (A_GROUPS, A_BLOCK, A_GROUP_DIM)).astype(F32)
            y_a = _sgu(xt, w_a, even_sgu_ln_g[i][None], even_sgu_ln_b[i][None], sw, sb, tm)
            h_b = _mm(xt, w_b, BF16, tm)
            ab = _mm(xt, w_ab, F32, tm)
            abt = ab.reshape(bsz, seq, LANES)[:, :, 0:SUBLANES].transpose(0, 2, 1)
            hp = jnp.pad(jnp.stack([even_a_log[i], even_dt_bias[i]]), ((0, 0), (0, LANES - B_HEADS)))
            hpt = jnp.pad(jnp.stack([even_a_log[i], even_dt_bias[i]], axis=1), ((0, SUBLANES - B_HEADS), (0, 0)))
            y_b = _gdn(h_b, ab, abt, even_conv_w[i], hp, hpt, even_gdn_norm[i][None], bsz, seq)
            w_out = even_w_out[i].astype(BF16)
            a_list = [y_a, y_b]
            w_list = [w_out[0:A_WIDTH], w_out[A_WIDTH:]]
        else:
            w_in = odd_w_in[i]
            w_pe = w_in[:, Q_LORA + KV_LORA:]
            w_in2 = jnp.concatenate([w_in, rot_cols(w_pe)], axis=1).astype(BF16)
            w_uq = odd_w_uq[i].reshape(Q_LORA, C_HEADS, C_QK)
            w_uq2 = jnp.concatenate([w_uq, rot_cols(w_uq[..., C_NOPE:])], axis=-1)
            w_uq2 = w_uq2.reshape(Q_LORA, C_HEADS * (C_NOPE + 2 * C_ROPE)).astype(BF16)
            h = _mm(xt, w_in2, BF16, tm)
            q, k, v = _mla_up(h, odd_q_norm[i][None], odd_kv_norm[i][None], w_uq2, odd_w_ukv[i].astype(BF16), cs,
                              bsz, seq, _row_tile(seq, ROWS))
            o = _attn(q, k, v)
            a_list = [o.reshape(t, C_HEADS * C_V)]
            w_list = [odd_w_out[i].astype(BF16)]

        w_r = jnp.pad(moe_w_router[layer], ((0, 0), (0, LANES - N_EXPERTS))).astype(BF16)
        b_r = jnp.pad(moe_b_router[layer], (0, LANES - N_EXPERTS), constant_values=NEG_BIG)[None]
        x1, meta, gates, counts = _proj_ln_router(a_list, w_list, xt, ln_mix_g[layer][None], ln_mix_b[layer][None],
                                                  w_r, b_r, tm)
        xt = _moe_layer(layer, x1, meta, gates, counts, ln_ffn_g[layer][None], ln_ffn_b[layer][None],
                        moe_w_gate_up, b_gate, b_up, moe_w_down, b_down)
    return xt.reshape(bsz, seq, d)
```

```python
import functools

import jax
import jax.numpy as jnp
import numpy as np
from jax import lax
from jax.experimental import pallas as pl
from jax.experimental.pallas import tpu as pltpu

F32 = jnp.float32
BF16 = jnp.bfloat16

D_MODEL = 1024
DEPTH = 4
CHUNK = 64
CHUNK_SHIFT = CHUNK.bit_length() - 1
DEEPNORM_ALPHA = (2 * DEPTH) ** 0.25
LN_EPS = 1e-5
RMS_EPS = 1e-6

A_BLOCK = 128
A_GROUPS = 4
A_WIDTH = D_MODEL // 2
A_GROUP_DIM = A_WIDTH // A_GROUPS

B_HEADS = 4
B_HEAD_DIM = D_MODEL // 8
B_WIDTH = B_HEADS * B_HEAD_DIM
CONV_K = 4

C_HEADS = 8
C_NOPE = 128
C_ROPE = 64
C_V = D_MODEL // C_HEADS
C_QK = C_NOPE + C_ROPE
Q_LORA = 3 * D_MODEL // 8
KV_LORA = D_MODEL // 4
ROPE_THETA = 10000.0

N_EXPERTS = 32
TOP_K = 4
D_FF = D_MODEL
SWIGLU_LIMIT = 7.0
SWIGLU_ALPHA = 1.702

LANES = 128
SUBLANES = 8
VMEM_LIMIT = 48 * 1024 * 1024
MOE_VMEM_LIMIT = 58 * 1024 * 1024
NEG_BIG = -1e30
LOG2_E = 1.4426950408889634

ROWS = 512
GDN_ROWS = 256
ATTN_ROWS = 512
ATTN_HEADS_PER_STEP = 2
MOE_ROWS = 512
MOE_PREP_COLS = 512


def _row_tile(t, pref):
    tm = pref
    while t % tm:
        tm //= 2
    return tm


def _params(*sem, vmem=VMEM_LIMIT):
    return pltpu.CompilerParams(dimension_semantics=sem, vmem_limit_bytes=vmem)


def _dot(a, b):
    return jnp.dot(a, b, preferred_element_type=F32)


def _dot_nt(a, b):
    return lax.dot_general(a, b, (((1,), (1,)), ((), ())), preferred_element_type=F32)


def _dot_tn(a, b):
    return lax.dot_general(a, b, (((0,), (0,)), ((), ())), preferred_element_type=F32)


def _sigmoid(x):
    return 1.0 / (1.0 + jnp.exp(-x))


def _softplus(x):
    return jnp.maximum(x, 0.0) + jnp.log1p(jnp.exp(-jnp.abs(x)))


def _layer_norm_rows(y, g, b):
    mu = jnp.mean(y, axis=-1, keepdims=True)
    d = y - mu
    var = jnp.mean(d * d, axis=-1, keepdims=True)
    return d * lax.rsqrt(var + LN_EPS) * g + b


def _mm_kernel(x_ref, w_ref, o_ref):
    o_ref[...] = _dot(x_ref[...].astype(BF16), w_ref[...]).astype(o_ref.dtype)


def _mm(x, w, out_dtype, tm):
    t, k = x.shape
    n = w.shape[1]
    return pl.pallas_call(
        _mm_kernel,
        grid=(t // tm,),
        in_specs=[pl.BlockSpec((tm, k), lambda i: (i, 0)),
                  pl.BlockSpec((k, n), lambda i: (0, 0))],
        out_specs=pl.BlockSpec((tm, n), lambda i: (i, 0)),
        out_shape=jax.ShapeDtypeStruct((t, n), out_dtype),
        compiler_params=_params("parallel"),
        name="mm",
    )(x, w)


def _mm_split_kernel(x_ref, w_ref, o0_ref, o1_ref):
    n0 = o0_ref.shape[1]
    h = _dot(x_ref[...].astype(BF16), w_ref[...])
    o0_ref[...] = h[:, 0:n0].astype(o0_ref.dtype)
    o1_ref[...] = h[:, n0:].astype(o1_ref.dtype)


def _mm_split(x, w, n0, dtype0, dtype1, tm):
    t, k = x.shape
    n = w.shape[1]
    return pl.pallas_call(
        _mm_split_kernel,
        grid=(t // tm,),
        in_specs=[pl.BlockSpec((tm, k), lambda i: (i, 0)),
                  pl.BlockSpec((k, n), lambda i: (0, 0))],
        out_specs=[pl.BlockSpec((tm, n0), lambda i: (i, 0)),
                   pl.BlockSpec((tm, n - n0), lambda i: (i, 0))],
        out_shape=[jax.ShapeDtypeStruct((t, n0), dtype0), jax.ShapeDtypeStruct((t, n - n0), dtype1)],
        compiler_params=_params("parallel"),
        name="mm_split",
    )(x, w)


def _gelu(x):
    return 0.5 * x * (1.0 + lax.erf(x * (2.0 ** -0.5)))


def _sgu_kernel(x_ref, w_ref, g_ref, b_ref, sw_ref, sb_ref, o_ref):
    tm = x_ref.shape[0]
    h = _dot(x_ref[...].astype(BF16), w_ref[...])
    for grp in range(A_GROUPS):
        c0 = grp * A_GROUP_DIM
        u = _gelu(h[:, c0:c0 + A_GROUP_DIM])
        v = _gelu(h[:, A_WIDTH + c0:A_WIDTH + c0 + A_GROUP_DIM])
        vn = _layer_norm_rows(v, g_ref[:, c0:c0 + A_GROUP_DIM], b_ref[:, c0:c0 + A_GROUP_DIM]).astype(BF16)
        for blk in range(tm // A_BLOCK):
            r0 = blk * A_BLOCK
            mixed = _dot(sw_ref[grp], vn[r0:r0 + A_BLOCK]) + sb_ref[grp]
            o_ref[r0:r0 + A_BLOCK, c0:c0 + A_GROUP_DIM] = (u[r0:r0 + A_BLOCK] * mixed).astype(o_ref.dtype)


def _sgu(x, w_a, ln_g, ln_b, sw, sb, tm):
    t = x.shape[0]
    return pl.pallas_call(
        _sgu_kernel,
        grid=(t // tm,),
        in_specs=[pl.BlockSpec((tm, D_MODEL), lambda i: (i, 0)),
                  pl.BlockSpec((D_MODEL, 2 * A_WIDTH), lambda i: (0, 0)),
                  pl.BlockSpec((1, A_WIDTH), lambda i: (0, 0)),
                  pl.BlockSpec((1, A_WIDTH), lambda i: (0, 0)),
                  pl.BlockSpec((A_GROUPS, A_BLOCK, A_BLOCK), lambda i: (0, 0, 0)),
                  pl.BlockSpec((A_GROUPS, A_BLOCK, A_GROUP_DIM), lambda i: (0, 0, 0))],
        out_specs=pl.BlockSpec((tm, A_WIDTH), lambda i: (i, 0)),
        out_shape=jax.ShapeDtypeStruct((t, A_WIDTH), BF16),
        compiler_params=_params("parallel"),
        name="sgu",
    )(x, w_a, ln_g, ln_b, sw, sb)


def _gdn_kernel(h_ref, ab_ref, abt_ref, cw_ref, hp_ref, hpt_ref, nw_ref, o_ref, conv_scr, state_scr):
    tr = h_ref.shape[0]
    n_chunks = tr // CHUNK
    s_idx = pl.program_id(1)

    @pl.when(s_idx == 0)
    def _():
        conv_scr[0:SUBLANES, :] = jnp.zeros((SUBLANES, 3 * B_WIDTH), F32)
        state_scr[...] = jnp.zeros(state_scr.shape, F32)

    conv_scr[SUBLANES:SUBLANES + tr, :] = h_ref[:, 0:3 * B_WIDTH].astype(F32)
    acc = conv_scr[SUBLANES:SUBLANES + tr, :] * cw_ref[CONV_K - 1:CONV_K, :]
    for d in range(1, CONV_K):
        acc = acc + conv_scr[SUBLANES - d:SUBLANES - d + tr, :] * cw_ref[CONV_K - 1 - d:CONV_K - d, :]
    conv_scr[0:SUBLANES, :] = conv_scr[tr:tr + SUBLANES, :]
    qkv = acc * _sigmoid(acc)

    ab = ab_ref[...]
    sp_c = _softplus(ab + hp_ref[1:2, :])
    g_col = -jnp.exp(hp_ref[0:1, :]) * sp_c
    beta_col = _sigmoid(ab)
    abt = abt_ref[0]
    sp_r = _softplus(abt + hpt_ref[:, 1:2])
    g_row = -jnp.exp(hpt_ref[:, 0:1]) * sp_r

    ri = lax.broadcasted_iota(jnp.int32, (tr, tr), 0)
    ci = lax.broadcasted_iota(jnp.int32, (tr, tr), 1)
    same = jnp.right_shift(ri, CHUNK_SHIFT) == jnp.right_shift(ci, CHUNK_SHIFT)
    lower = jnp.where(same & (ri >= ci), 1.0, 0.0).astype(F32)
    upper = jnp.where(same & (ri <= ci), 1.0, 0.0).astype(F32)
    gc_col = jnp.dot(lower, g_col, preferred_element_type=F32, precision=lax.Precision.HIGHEST)
    gc_row = jnp.dot(g_row, upper, preferred_element_type=F32, precision=lax.Precision.HIGHEST)

    incl = same & (ri >= ci)
    strict = same & (ri > ci)
    eye = jnp.where(ri == ci, 1.0, 0.0).astype(F32)
    n_sq = CHUNK_SHIFT - 1

    heads = range(B_HEADS)
    qs, ks, vs, betas, gccs, decays, k_bs, kbs, ps, ts = [], [], [], [], [], [], [], [], [], []
    for hd in heads:
        l0 = hd * B_HEAD_DIM
        q = qkv[:, l0:l0 + B_HEAD_DIM]
        k = qkv[:, B_WIDTH + l0:B_WIDTH + l0 + B_HEAD_DIM]
        qs.append(q * lax.rsqrt(jnp.sum(q * q, axis=-1, keepdims=True) + RMS_EPS) * (B_HEAD_DIM ** -0.5))
        ks.append(k * lax.rsqrt(jnp.sum(k * k, axis=-1, keepdims=True) + RMS_EPS))
        vs.append(qkv[:, 2 * B_WIDTH + l0:2 * B_WIDTH + l0 + B_HEAD_DIM])
        betas.append(beta_col[:, B_HEADS + hd:B_HEADS + hd + 1])
        gccs.append(gc_col[:, hd:hd + 1])
        gcr = gc_row[hd:hd + 1, :]
        decays.append(jnp.where(incl, jnp.exp(jnp.where(incl, gccs[hd] - gcr, 0.0)), 0.0))
        k_bs.append(ks[hd].astype(BF16))
        kbs.append(ks[hd] * betas[hd])
    for hd in heads:
        a_mat = jnp.where(strict, _dot_nt(kbs[hd].astype(BF16), k_bs[hd]) * decays[hd], 0.0)
        ps.append(-a_mat)
        ts.append(eye - a_mat)
    for _ in range(n_sq):
        for hd in heads:
            pb = ps[hd].astype(BF16)
            ps[hd] = _dot(pb, pb)
        for hd in heads:
            ts[hd] = ts[hd] + _dot(ts[hd].astype(BF16), ps[hd].astype(BF16))
    us, ws, intras, q_decs = [], [], [], []
    for hd in heads:
        t_b = ts[hd].astype(BF16)
        e_gc = jnp.exp(gccs[hd])
        us.append(_dot(t_b, (vs[hd] * betas[hd]).astype(BF16)))
        ws.append(_dot(t_b, (kbs[hd] * e_gc).astype(BF16)).astype(BF16))
        intras.append(jnp.where(incl, _dot_nt(qs[hd].astype(BF16), k_bs[hd]) * decays[hd], 0.0).astype(BF16))
        q_decs.append((qs[hd] * e_gc).astype(BF16))
    for c in range(n_chunks):
        r0 = c * CHUNK
        for hd in heads:
            l0 = hd * B_HEAD_DIM
            g_last = gccs[hd][r0 + CHUNK - 1:r0 + CHUNK, :]
            k_dec = (ks[hd][r0:r0 + CHUNK] * jnp.exp(g_last - gccs[hd][r0:r0 + CHUNK])).astype(BF16)
            state = state_scr[hd]
            state_b = state.astype(BF16)
            v_new = (us[hd][r0:r0 + CHUNK] - _dot(ws[hd][r0:r0 + CHUNK], state_b)).astype(BF16)
            out = _dot(q_decs[hd][r0:r0 + CHUNK], state_b) + _dot(intras[hd][r0:r0 + CHUNK, r0:r0 + CHUNK], v_new)
            state_scr[hd] = state * jnp.exp(g_last) + _dot_tn(k_dec, v_new)
            z = h_ref[r0:r0 + CHUNK, 3 * B_WIDTH + l0:3 * B_WIDTH + l0 + B_HEAD_DIM].astype(F32)
            o_n = out * lax.rsqrt(jnp.mean(out * out, axis=-1, keepdims=True) + RMS_EPS) * nw_ref[...]
            o_ref[r0:r0 + CHUNK, l0:l0 + B_HEAD_DIM] = (o_n * (z * _sigmoid(z))).astype(o_ref.dtype)


def _gdn(h_b, ab, abt, conv_w, hp, hpt, norm_w, bsz, seq):
    tr = _row_tile(seq, GDN_ROWS)
    n_s = seq // tr
    return pl.pallas_call(
        _gdn_kernel,
        grid=(bsz, n_s),
        in_specs=[pl.BlockSpec((tr, 4 * B_WIDTH), lambda b, s: (b * n_s + s, 0)),
                  pl.BlockSpec((tr, LANES), lambda b, s: (b * n_s + s, 0)),
                  pl.BlockSpec((1, SUBLANES, tr), lambda b, s: (b, 0, s)),
                  pl.BlockSpec((CONV_K, 3 * B_WIDTH), lambda b, s: (0, 0)),
                  pl.BlockSpec((2, LANES), lambda b, s: (0, 0)),
                  pl.BlockSpec((SUBLANES, 2), lambda b, s: (0, 0)),
                  pl.BlockSpec((1, B_HEAD_DIM), lambda b, s: (0, 0))],
        out_specs=pl.BlockSpec((tr, B_WIDTH), lambda b, s: (b * n_s + s, 0)),
        out_shape=jax.ShapeDtypeStruct((bsz * seq, B_WIDTH), BF16),
        scratch_shapes=[pltpu.VMEM((tr + 2 * SUBLANES, 3 * B_WIDTH), F32),
                        pltpu.VMEM((B_HEADS, B_HEAD_DIM, B_HEAD_DIM), F32)],
        compiler_params=_params("parallel", "arbitrary"),
        name="gdn",
    )(h_b, ab, abt, conv_w, hp, hpt, norm_w)


def _mla_up_kernel(h_ref, qn_ref, kvn_ref, wq_ref, wkv_ref, cs_ref, q_ref, k_ref, v_ref):
    h = h_ref[...].astype(F32)
    cs = cs_ref[...]

    def rms(x, g):
        return x * lax.rsqrt(jnp.mean(x * x, axis=-1, keepdims=True) + RMS_EPS) * g

    def rope(t):
        r = t * cs
        return r + pltpu.roll(r, C_ROPE, 1)

    cq = rms(h[:, 0:Q_LORA], qn_ref[...]).astype(BF16)
    ckv = rms(h[:, Q_LORA:Q_LORA + KV_LORA], kvn_ref[...]).astype(BF16)
    k_pe = rope(h[:, Q_LORA + KV_LORA:Q_LORA + KV_LORA + 2 * C_ROPE])[:, 0:C_ROPE].astype(BF16)
    qf = _dot(cq, wq_ref[...]) * (C_QK ** -0.5 * LOG2_E)
    kv = _dot(ckv, wkv_ref[...])
    for hd in range(C_HEADS):
        q0 = hd * (C_NOPE + 2 * C_ROPE)
        q_ref[0, hd, :, 0:C_NOPE] = qf[:, q0:q0 + C_NOPE].astype(BF16)
        q_ref[0, hd, :, C_NOPE:C_QK] = rope(qf[:, q0 + C_NOPE:q0 + C_NOPE + 2 * C_ROPE])[:, 0:C_ROPE].astype(BF16)
        k0 = hd * (C_NOPE + C_V)
        k_ref[0, hd, :, 0:C_NOPE] = kv[:, k0:k0 + C_NOPE].astype(BF16)
        k_ref[0, hd, :, C_NOPE:C_QK] = k_pe
        v_ref[0, hd, :, :] = kv[:, k0 + C_NOPE:k0 + C_NOPE + C_V].astype(BF16)


def _mla_up(h, q_norm, kv_norm, w_uq, w_ukv, cs, bsz, seq, tm):
    n_s = seq // tm
    hw = h.shape[1]
    qk_shape = jax.ShapeDtypeStruct((bsz, C_HEADS, seq, C_QK), BF16)
    return pl.pallas_call(
        _mla_up_kernel,
        grid=(bsz, n_s),
        in_specs=[pl.BlockSpec((tm, hw), lambda b, s: (b * n_s + s, 0)),
                  pl.BlockSpec((1, Q_LORA), lambda b, s: (0, 0)),
                  pl.BlockSpec((1, KV_LORA), lambda b, s: (0, 0)),
                  pl.BlockSpec(w_uq.shape, lambda b, s: (0, 0)),
                  pl.BlockSpec(w_ukv.shape, lambda b, s: (0, 0)),
                  pl.BlockSpec((tm, LANES), lambda b, s: (s, 0))],
        out_specs=[pl.BlockSpec((1, C_HEADS, tm, C_QK), lambda b, s: (b, 0, s, 0)),
                   pl.BlockSpec((1, C_HEADS, tm, C_QK), lambda b, s: (b, 0, s, 0)),
                   pl.BlockSpec((1, C_HEADS, tm, C_V), lambda b, s: (b, 0, s, 0))],
        out_shape=[qk_shape, qk_shape, jax.ShapeDtypeStruct((bsz, C_HEADS, seq, C_V), BF16)],
        compiler_params=_params("parallel", "parallel"),
        name="mla_up",
    )(h, q_norm, kv_norm, w_uq, w_ukv, cs)


def _attn_kernel(q_ref, k_ref, v_ref, o_ref):
    nh, tq = q_ref.shape[1], q_ref.shape[2]
    qi = pl.program_id(2)

    def step(hd, k0, carry, mask):
        m, l, acc = carry
        s = _dot_nt(q_ref[0, hd], k_ref[0, hd, pl.ds(k0, tq), :])
        if mask is not None:
            s = jnp.where(mask, s, NEG_BIG)
        m_new = jnp.maximum(m, jnp.max(s, axis=-1, keepdims=True))
        alpha = jnp.exp2(m - m_new)
        p = jnp.exp2(s - m_new)
        l = alpha * l + jnp.sum(p, axis=-1, keepdims=True)
        acc = alpha * acc + _dot(p.astype(BF16), v_ref[0, hd, pl.ds(k0, tq), :])
        return m_new, l, acc

    def body(j, carry):
        k0 = pl.multiple_of(j * tq, tq)
        return tuple(step(hd, k0, carry[hd], None) for hd in range(nh))

    init = tuple((jnp.full((tq, 1), NEG_BIG, F32), jnp.zeros((tq, 1), F32), jnp.zeros((tq, C_V), F32))
                 for _ in range(nh))
    carry = lax.fori_loop(0, qi, body, init)
    k0 = pl.multiple_of(qi * tq, tq)
    ri = jnp.right_shift(lax.broadcasted_iota(jnp.int32, (tq, tq), 0), CHUNK_SHIFT)
    ci = jnp.right_shift(lax.broadcasted_iota(jnp.int32, (tq, tq), 1), CHUNK_SHIFT)
    mask = ci <= ri
    for hd in range(nh):
        _, l, acc = step(hd, k0, carry[hd], mask)
        o_ref[0, :, hd * C_V:(hd + 1) * C_V] = (acc / l).astype(o_ref.dtype)


def _attn(q, k, v):
    bsz, _, seq, _ = q.shape
    tq = _row_tile(seq, ATTN_ROWS)
    nh = ATTN_HEADS_PER_STEP
    return pl.pallas_call(
        _attn_kernel,
        grid=(bsz, C_HEADS // nh, seq // tq),
        in_specs=[pl.BlockSpec((1, nh, tq, C_QK), lambda b, h, i: (b, h, i, 0)),
                  pl.BlockSpec((1, nh, seq, C_QK), lambda b, h, i: (b, h, 0, 0)),
                  pl.BlockSpec((1, nh, seq, C_V), lambda b, h, i: (b, h, 0, 0))],
        out_specs=pl.BlockSpec((1, tq, nh * C_V), lambda b, h, i: (b, i, h)),
        out_shape=jax.ShapeDtypeStruct((bsz, seq, C_HEADS * C_V), BF16),
        compiler_params=_params("parallel", "parallel", "arbitrary"),
        name="attn",
    )(q, k, v)


def _proj_ln_router_kernel(n_in, *refs):
    a_refs = refs[0:n_in]
    w_refs = refs[n_in:2 * n_in]
    x_ref, g_ref, b_ref, wr_ref, br_ref = refs[2 * n_in:2 * n_in + 5]
    x1_ref, meta_ref, gate_ref, cnt_ref = refs[2 * n_in + 5:2 * n_in + 9]
    base_scr = refs[2 * n_in + 9]
    tm = x_ref.shape[0]
    i = pl.program_id(0)

    @pl.when(i == 0)
    def _():
        base_scr[...] = jnp.zeros(base_scr.shape, F32)

    mix = _dot(a_refs[0][...], w_refs[0][...])
    for j in range(1, n_in):
        mix = mix + _dot(a_refs[j][...], w_refs[j][...])
    x1 = _layer_norm_rows(DEEPNORM_ALPHA * x_ref[...] + mix, g_ref[...], b_ref[...])
    x1_ref[...] = x1

    logits = _dot(x1.astype(BF16), wr_ref[...]) + br_ref[...]
    lane = lax.broadcasted_iota(jnp.int32, (tm, LANES), 1)
    lane_f = lane.astype(F32)
    work = logits
    cnt = jnp.zeros((tm, LANES), F32)
    tops, idxs, hots = [], [], []
    for _ in range(TOP_K):
        mx = jnp.max(work, axis=-1, keepdims=True)
        idx_f = jnp.min(jnp.where(work == mx, lane_f, float(LANES)), axis=-1, keepdims=True)
        idx = idx_f.astype(jnp.int32)
        hot = lane == idx
        work = jnp.where(hot, NEG_BIG * 2.0, work)
        cnt = cnt + jnp.where(hot, 1.0, 0.0)
        tops.append(mx)
        idxs.append(idx)
        hots.append(hot)
    exps = [jnp.exp(t - tops[0]) for t in tops]
    denom = exps[0] + exps[1] + exps[2] + exps[3]

    ri = lax.broadcasted_iota(jnp.int32, (tm, tm), 0)
    ci = lax.broadcasted_iota(jnp.int32, (tm, tm), 1)
    before = jnp.where(ri > ci, 1.0, 0.0).astype(BF16)
    rank = _dot(before, cnt.astype(BF16)) + base_scr[0:1, :]
    meta = jnp.zeros((tm, LANES), jnp.int32)
    gates = jnp.zeros((tm, LANES), F32)
    for kk in range(TOP_K):
        pos = jnp.sum(jnp.where(hots[kk], rank, 0.0), axis=-1, keepdims=True).astype(jnp.int32)
        meta = jnp.where(lane == kk, idxs[kk], meta)
        meta = jnp.where(lane == TOP_K + kk, pos, meta)
        gates = jnp.where(lane == kk, exps[kk] / denom, gates)
    meta_ref[...] = meta
    gate_ref[...] = gates
    total = base_scr[0:1, :] + jnp.sum(cnt, axis=0, keepdims=True)
    base_scr[...] = jnp.broadcast_to(total, base_scr.shape)
    cnt_ref[...] = jnp.broadcast_to(total, cnt_ref.shape)


def _proj_ln_router(a_list, w_list, x, ln_g, ln_b, w_r, b_r, tm):
    t = x.shape[0]
    n_in = len(a_list)
    in_specs = [pl.BlockSpec((tm, a.shape[1]), lambda i: (i, 0)) for a in a_list]
    in_specs += [pl.BlockSpec(w.shape, lambda i: (0, 0)) for w in w_list]
    in_specs += [pl.BlockSpec((tm, D_MODEL), lambda i: (i, 0)),
                 pl.BlockSpec((1, D_MODEL), lambda i: (0, 0)),
                 pl.BlockSpec((1, D_MODEL), lambda i: (0, 0)),
                 pl.BlockSpec((D_MODEL, LANES), lambda i: (0, 0)),
                 pl.BlockSpec((1, LANES), lambda i: (0, 0))]
    return pl.pallas_call(
        functools.partial(_proj_ln_router_kernel, n_in),
        grid=(t // tm,),
        in_specs=in_specs,
        out_specs=[pl.BlockSpec((tm, D_MODEL), lambda i: (i, 0)),
                   pl.BlockSpec((tm, LANES), lambda i: (i, 0)),
                   pl.BlockSpec((tm, LANES), lambda i: (i, 0)),
                   pl.BlockSpec((SUBLANES, LANES), lambda i: (0, 0))],
        out_shape=[jax.ShapeDtypeStruct((t, D_MODEL), F32),
                   jax.ShapeDtypeStruct((t, LANES), jnp.int32),
                   jax.ShapeDtypeStruct((t, LANES), F32),
                   jax.ShapeDtypeStruct((SUBLANES, LANES), F32)],
        scratch_shapes=[pltpu.VMEM((SUBLANES, LANES), F32)],
        compiler_params=_params("arbitrary"),
        name="proj_ln_router",
    )(*a_list, *w_list, x, ln_g, ln_b, w_r, b_r)


def _moe_kernel(blk_ref, exp_ref, lo_ref, hi_ref, x_ref, wgu_ref, wd_ref, bg_ref, bu_ref, bd_ref, o_ref,
                wg_scr, wu_scr, wd_scr, t_scr):
    i = pl.program_id(0)
    tm = x_ref.shape[0]
    blk = blk_ref[i]
    lo = lo_ref[i]
    hi = hi_ref[i]
    prev = jnp.maximum(i - 1, 0)
    first = jnp.logical_or(i == 0, blk != blk_ref[prev])
    new_expert = jnp.logical_or(i == 0, exp_ref[i] != exp_ref[prev])

    @pl.when(new_expert)
    def _():
        pc = t_scr.shape[1]
        half = pc // 2
        for c in range(2 * D_FF // pc):
            for kc in range(D_MODEL // LANES):
                k0 = kc * LANES
                t_scr[kc] = wgu_ref[0, 0, k0:k0 + LANES, c * pc:(c + 1) * pc].T
                wg_scr[c * half:(c + 1) * half, k0:k0 + LANES] = t_scr[kc, pl.ds(0, half, stride=2), :].astype(BF16)
                wu_scr[c * half:(c + 1) * half, k0:k0 + LANES] = t_scr[kc, pl.ds(1, half, stride=2), :].astype(BF16)
        wd_scr[...] = wd_ref[0, 0].astype(BF16)

    @pl.when(first)
    def _():
        o_ref[...] = jnp.zeros(o_ref.shape, o_ref.dtype)

    @pl.when(hi > lo)
    def _():
        x = x_ref[...]
        glu = jnp.minimum(_dot_nt(x, wg_scr[...]) + bg_ref[0, 0], SWIGLU_LIMIT)
        lin = jnp.clip(_dot_nt(x, wu_scr[...]) + bu_ref[0, 0], -SWIGLU_LIMIT, SWIGLU_LIMIT)
        act = glu * _sigmoid(SWIGLU_ALPHA * glu) * (lin + 1.0)
        y = _dot(act.astype(BF16), wd_scr[...]) + bd_ref[0, 0]
        row = blk * tm + lax.broadcasted_iota(jnp.int32, (tm, 1), 0)
        keep = jnp.logical_and(row >= lo, row < hi)
        o_ref[...] = jnp.where(keep, y.astype(o_ref.dtype), o_ref[...])


def _moe_sorted(layer, xs, counts, w_gate_up, b_gate, b_up, w_down, b_down):
    n_rows = xs.shape[0]
    tm = _row_tile(n_rows, MOE_ROWS)
    items = _moe_items(counts, n_rows, tm)
    n_items = items[0].shape[0]

    def by_expert(i, blk, ex, lo, hi):
        return (layer, ex[i], 0, 0)

    def by_block(i, blk, ex, lo, hi):
        return (blk[i], 0)

    grid_spec = pltpu.PrefetchScalarGridSpec(
        num_scalar_prefetch=4,
        grid=(n_items,),
        in_specs=[pl.BlockSpec((tm, D_MODEL), by_block),
                  pl.BlockSpec((1, 1, D_MODEL, 2 * D_FF), by_expert),
                  pl.BlockSpec((1, 1, D_FF, D_MODEL), by_expert),
                  pl.BlockSpec((1, 1, 1, D_FF), by_expert),
                  pl.BlockSpec((1, 1, 1, D_FF), by_expert),
                  pl.BlockSpec((1, 1, 1, D_MODEL), by_expert)],
        out_specs=pl.BlockSpec((tm, D_MODEL), by_block),
        scratch_shapes=[pltpu.VMEM((D_FF, D_MODEL), BF16),
                        pltpu.VMEM((D_FF, D_MODEL), BF16),
                        pltpu.VMEM((D_FF, D_MODEL), BF16),
                        pltpu.VMEM((D_MODEL // LANES, MOE_PREP_COLS, LANES), F32)],
    )
    return pl.pallas_call(
        _moe_kernel,
        grid_spec=grid_spec,
        out_shape=jax.ShapeDtypeStruct((n_rows, D_MODEL), BF16),
        compiler_params=_params("arbitrary", vmem=MOE_VMEM_LIMIT),
        name="moe_ffn",
    )(*items, xs, w_gate_up, w_down, b_gate, b_up, b_down)


def _moe_items(counts, n_rows, tm):
    n_blocks = n_rows // tm
    start = jnp.cumsum(counts) - counts
    cuts = jnp.sort(jnp.concatenate([jnp.arange(n_blocks, dtype=jnp.int32) * tm, start.astype(jnp.int32)]))
    lo = cuts
    hi = jnp.concatenate([cuts[1:], jnp.array([n_rows], jnp.int32)])
    blk = jnp.minimum(lo // tm, n_blocks - 1)
    ex = jnp.clip(jnp.searchsorted(start, lo, side="right") - 1, 0, N_EXPERTS - 1)
    return blk.astype(jnp.int32), ex.astype(jnp.int32), lo.astype(jnp.int32), hi.astype(jnp.int32)


def _combine_ln_kernel(x_ref, y_ref, gate_ref, g_ref, b_ref, o_ref):
    gates = gate_ref[...]
    ffn = y_ref[0].astype(F32) * gates[:, 0:1]
    for kk in range(1, TOP_K):
        ffn = ffn + y_ref[kk].astype(F32) * gates[:, kk:kk + 1]
    o_ref[...] = _layer_norm_rows(DEEPNORM_ALPHA * x_ref[...] + ffn, g_ref[...], b_ref[...])


def _combine_ln(x1, y4, gates, ln_g, ln_b, tm):
    t = x1.shape[0]
    return pl.pallas_call(
        _combine_ln_kernel,
        grid=(t // tm,),
        in_specs=[pl.BlockSpec((tm, D_MODEL), lambda i: (i, 0)),
                  pl.BlockSpec((TOP_K, tm, D_MODEL), lambda i: (0, i, 0)),
                  pl.BlockSpec((tm, LANES), lambda i: (i, 0)),
                  pl.BlockSpec((1, D_MODEL), lambda i: (0, 0)),
                  pl.BlockSpec((1, D_MODEL), lambda i: (0, 0))],
        out_specs=pl.BlockSpec((tm, D_MODEL), lambda i: (i, 0)),
        out_shape=jax.ShapeDtypeStruct((t, D_MODEL), F32),
        compiler_params=_params("parallel"),
        name="combine_ln",
    )(x1, y4, gates, ln_g, ln_b)


def _moe_layer(layer, x1, meta, gates, counts, ln_g, ln_b, w_gate_up, b_gate, b_up, w_down, b_down):
    t = x1.shape[0]
    e = meta[:, 0:TOP_K]
    pos = meta[:, TOP_K:2 * TOP_K]
    cnt = counts[0, 0:N_EXPERTS].astype(jnp.int32)
    start = jnp.cumsum(cnt) - cnt
    dest = start[e] + pos
    n_rows = t * TOP_K
    tok = jnp.arange(n_rows, dtype=jnp.int32) // TOP_K
    tok_sorted = jnp.zeros((n_rows,), jnp.int32).at[dest.reshape(-1)].set(tok, unique_indices=True)
    xs = x1.astype(BF16)[tok_sorted]
    ys = _moe_sorted(layer, xs, cnt, w_gate_up, b_gate, b_up, w_down, b_down)
    y4 = ys[dest.T]
    return _combine_ln(x1, y4, gates, ln_g, ln_b, _row_tile(t, ROWS))


def kernel(x, even_w_in, even_sgu_ln_g, even_sgu_ln_b, even_sgu_w, even_sgu_b, even_conv_w, even_a_log, even_dt_bias, even_gdn_norm, even_w_out, odd_w_in, odd_q_norm, odd_w_uq, odd_kv_norm, odd_w_ukv, odd_w_out, ln_mix_g, ln_mix_b, ln_ffn_g, ln_ffn_b, moe_w_router, moe_b_router, moe_w_gate_up, moe_b_gate_up, moe_w_down, moe_b_down):
    bsz, seq, d = x.shape
    t = bsz * seq
    xt = x.reshape(t, d)
    tm = _row_tile(t, ROWS)
    b_gate = moe_b_gate_up[:, :, None, 0::2]
    b_up = moe_b_gate_up[:, :, None, 1::2]
    b_down = moe_b_down[:, :, None, :]

    inv_freq = ROPE_THETA ** (-jnp.arange(0, C_ROPE, 2, dtype=F32) / C_ROPE)
    ang = jnp.arange(seq, dtype=F32)[:, None] * inv_freq[None, :]
    cs = jnp.concatenate([jnp.cos(ang), jnp.cos(ang), jnp.sin(ang), jnp.sin(ang)], axis=1)

    pos_chunk = np.arange(A_BLOCK) // CHUNK
    allowed = jnp.asarray(pos_chunk[None, :] <= pos_chunk[:, None])

    def rot_cols(w):
        half = C_ROPE // 2
        return jnp.concatenate([-w[..., half:], w[..., :half]], axis=-1)

    for layer in range(DEPTH):
        i = layer // 2
        if layer % 2 == 0:
            w_in = even_w_in[i]
            w_a = w_in[:, 0:2 * A_WIDTH].astype(BF16)
            w_b = jnp.pad(w_in[:, 2 * A_WIDTH:], ((0, 0), (0, LANES - 2 * B_HEADS))).astype(BF16)
            sw = jnp.where(allowed[None], even_sgu_w[i], 0).astype(BF16)
            sb = jnp.broadcast_to(even_sgu_b[i][:, :, None], (A_GROUPS, A_BLOCK, A_GROUP_DIM)).astype(F32)
            y_a = _sgu(xt, w_a, even_sgu_ln_g[i][None], even_sgu_ln_b[i][None], sw, sb, tm)
            h_b, ab = _mm_split(xt, w_b, 4 * B_WIDTH, BF16, F32, tm)
            abt = ab.reshape(bsz, seq, LANES)[:, :, 0:SUBLANES].transpose(0, 2, 1)
            hp = jnp.pad(jnp.stack([even_a_log[i], even_dt_bias[i]]), ((0, 0), (0, LANES - B_HEADS)))
            hpt = jnp.pad(jnp.stack([even_a_log[i], even_dt_bias[i]], axis=1), ((0, SUBLANES - B_HEADS), (0, 0)))
            y_b = _gdn(h_b, ab, abt, even_conv_w[i], hp, hpt, even_gdn_norm[i][None], bsz, seq)
            w_out = even_w_out[i].astype(BF16)
            a_list = [y_a, y_b]
            w_list = [w_out[0:A_WIDTH], w_out[A_WIDTH:]]
        else:
            w_in = odd_w_in[i]
            w_pe = w_in[:, Q_LORA + KV_LORA:]
            w_in2 = jnp.concatenate([w_in, rot_cols(w_pe)], axis=1).astype(BF16)
            w_uq = odd_w_uq[i].reshape(Q_LORA, C_HEADS, C_QK)
            w_uq2 = jnp.concatenate([w_uq, rot_cols(w_uq[..., C_NOPE:])], axis=-1)
            w_uq2 = w_uq2.reshape(Q_LORA, C_HEADS * (C_NOPE + 2 * C_ROPE)).astype(BF16)
            h = _mm(xt, w_in2, BF16, tm)
            q, k, v = _mla_up(h, odd_q_norm[i][None], odd_kv_norm[i][None], w_uq2, odd_w_ukv[i].astype(BF16), cs,
                              bsz, seq, _row_tile(seq, ROWS))
            o = _attn(q, k, v)
            a_list = [o.reshape(t, C_HEADS * C_V)]
            w_list = [odd_w_out[i].astype(BF16)]

        w_r = jnp.pad(moe_w_router[layer], ((0, 0), (0, LANES - N_EXPERTS))).astype(BF16)
        b_r = jnp.pad(moe_b_router[layer], (0, LANES - N_EXPERTS), constant_values=NEG_BIG)[None]
        x1, meta, gates, counts = _proj_ln_router(a_list, w_list, xt, ln_mix_g[layer][None], ln_mix_b[layer][None],
                                                  w_r, b_r, tm)
        xt = _moe_layer(layer, x1, meta, gates, counts, ln_ffn_g[layer][None], ln_ffn_b[layer][None],
                        moe_w_gate_up, b_gate, b_up, moe_w_down, b_down)
    return xt.reshape(bsz, seq, d)
```

```python
import functools

import jax
import jax.numpy as jnp
import numpy as np
from jax import lax
from jax.experimental import pallas as pl
from jax.experimental.pallas import tpu as pltpu

F32 = jnp.float32
BF16 = jnp.bfloat16

D_MODEL = 1024
DEPTH = 4
CHUNK = 64
CHUNK_SHIFT = CHUNK.bit_length() - 1
DEEPNORM_ALPHA = (2 * DEPTH) ** 0.25
LN_EPS = 1e-5
RMS_EPS = 1e-6

A_BLOCK = 128
A_GROUPS = 4
A_WIDTH = D_MODEL // 2
A_GROUP_DIM = A_WIDTH // A_GROUPS

B_HEADS = 4
B_HEAD_DIM = D_MODEL // 8
B_WIDTH = B_HEADS * B_HEAD_DIM
CONV_K = 4

C_HEADS = 8
C_NOPE = 128
C_ROPE = 64
C_V = D_MODEL // C_HEADS
C_QK = C_NOPE + C_ROPE
Q_LORA = 3 * D_MODEL // 8
KV_LORA = D_MODEL // 4
ROPE_THETA = 10000.0

N_EXPERTS = 32
TOP_K = 4
D_FF = D_MODEL
SWIGLU_LIMIT = 7.0
SWIGLU_ALPHA = 1.702

LANES = 128
SUBLANES = 8
VMEM_LIMIT = 48 * 1024 * 1024
MOE_VMEM_LIMIT = 58 * 1024 * 1024
NEG_BIG = -1e30
LOG2_E = 1.4426950408889634

ROWS = 512
GDN_ROWS = 256
ATTN_ROWS = 512
ATTN_HEADS_PER_STEP = 2
MOE_ROWS = 512
MOE_PREP_COLS = 512


def _row_tile(t, pref):
    tm = pref
    while t % tm:
        tm //= 2
    return tm


def _params(*sem, vmem=VMEM_LIMIT):
    return pltpu.CompilerParams(dimension_semantics=sem, vmem_limit_bytes=vmem)


def _dot(a, b):
    return jnp.dot(a, b, preferred_element_type=F32)


def _dot_nt(a, b):
    return lax.dot_general(a, b, (((1,), (1,)), ((), ())), preferred_element_type=F32)


def _dot_tn(a, b):
    return lax.dot_general(a, b, (((0,), (0,)), ((), ())), preferred_element_type=F32)


def _sigmoid(x):
    return 1.0 / (1.0 + jnp.exp(-x))


def _softplus(x):
    return jnp.maximum(x, 0.0) + jnp.log1p(jnp.exp(-jnp.abs(x)))


def _layer_norm_rows(y, g, b):
    mu = jnp.mean(y, axis=-1, keepdims=True)
    d = y - mu
    var = jnp.mean(d * d, axis=-1, keepdims=True)
    return d * lax.rsqrt(var + LN_EPS) * g + b


def _mm_kernel(x_ref, w_ref, o_ref):
    o_ref[...] = _dot(x_ref[...].astype(BF16), w_ref[...]).astype(o_ref.dtype)


def _mm(x, w, out_dtype, tm):
    t, k = x.shape
    n = w.shape[1]
    return pl.pallas_call(
        _mm_kernel,
        grid=(t // tm,),
        in_specs=[pl.BlockSpec((tm, k), lambda i: (i, 0)),
                  pl.BlockSpec((k, n), lambda i: (0, 0))],
        out_specs=pl.BlockSpec((tm, n), lambda i: (i, 0)),
        out_shape=jax.ShapeDtypeStruct((t, n), out_dtype),
        compiler_params=_params("parallel"),
        name="mm",
    )(x, w)


def _mm_split_kernel(x_ref, w_ref, o0_ref, o1_ref):
    n0 = o0_ref.shape[1]
    h = _dot(x_ref[...].astype(BF16), w_ref[...])
    o0_ref[...] = h[:, 0:n0].astype(o0_ref.dtype)
    o1_ref[...] = h[:, n0:].astype(o1_ref.dtype)


def _mm_split(x, w, n0, dtype0, dtype1, tm):
    t, k = x.shape
    n = w.shape[1]
    return pl.pallas_call(
        _mm_split_kernel,
        grid=(t // tm,),
        in_specs=[pl.BlockSpec((tm, k), lambda i: (i, 0)),
                  pl.BlockSpec((k, n), lambda i: (0, 0))],
        out_specs=[pl.BlockSpec((tm, n0), lambda i: (i, 0)),
                   pl.BlockSpec((tm, n - n0), lambda i: (i, 0))],
        out_shape=[jax.ShapeDtypeStruct((t, n0), dtype0), jax.ShapeDtypeStruct((t, n - n0), dtype1)],
        compiler_params=_params("parallel"),
        name="mm_split",
    )(x, w)


def _gelu(x):
    return 0.5 * x * (1.0 + lax.erf(x * (2.0 ** -0.5)))


def _sgu_kernel(x_ref, w_ref, g_ref, b_ref, sw_ref, sb_ref, o_ref):
    tm = x_ref.shape[0]
    h = _dot(x_ref[...].astype(BF16), w_ref[...])
    for grp in range(A_GROUPS):
        c0 = grp * A_GROUP_DIM
        u = _gelu(h[:, c0:c0 + A_GROUP_DIM])
        v = _gelu(h[:, A_WIDTH + c0:A_WIDTH + c0 + A_GROUP_DIM])
        vn = _layer_norm_rows(v, g_ref[:, c0:c0 + A_GROUP_DIM], b_ref[:, c0:c0 + A_GROUP_DIM]).astype(BF16)
        for blk in range(tm // A_BLOCK):
            r0 = blk * A_BLOCK
            mixed = _dot(sw_ref[grp], vn[r0:r0 + A_BLOCK]) + sb_ref[grp]
            o_ref[r0:r0 + A_BLOCK, c0:c0 + A_GROUP_DIM] = (u[r0:r0 + A_BLOCK] * mixed).astype(o_ref.dtype)


def _sgu(x, w_a, ln_g, ln_b, sw, sb, tm):
    t = x.shape[0]
    return pl.pallas_call(
        _sgu_kernel,
        grid=(t // tm,),
        in_specs=[pl.BlockSpec((tm, D_MODEL), lambda i: (i, 0)),
                  pl.BlockSpec((D_MODEL, 2 * A_WIDTH), lambda i: (0, 0)),
                  pl.BlockSpec((1, A_WIDTH), lambda i: (0, 0)),
                  pl.BlockSpec((1, A_WIDTH), lambda i: (0, 0)),
                  pl.BlockSpec((A_GROUPS, A_BLOCK, A_BLOCK), lambda i: (0, 0, 0)),
                  pl.BlockSpec((A_GROUPS, A_BLOCK, A_GROUP_DIM), lambda i: (0, 0, 0))],
        out_specs=pl.BlockSpec((tm, A_WIDTH), lambda i: (i, 0)),
        out_shape=jax.ShapeDtypeStruct((t, A_WIDTH), BF16),
        compiler_params=_params("parallel"),
        name="sgu",
    )(x, w_a, ln_g, ln_b, sw, sb)


def _gdn_kernel(h_ref, ab_ref, abt_ref, cw_ref, hp_ref, hpt_ref, nw_ref, o_ref, conv_scr, state_scr):
    tr = h_ref.shape[0]
    n_chunks = tr // CHUNK
    s_idx = pl.program_id(1)

    @pl.when(s_idx == 0)
    def _():
        conv_scr[0:SUBLANES, :] = jnp.zeros((SUBLANES, 3 * B_WIDTH), F32)
        state_scr[...] = jnp.zeros(state_scr.shape, F32)

    conv_scr[SUBLANES:SUBLANES + tr, :] = h_ref[:, 0:3 * B_WIDTH].astype(F32)
    acc = conv_scr[SUBLANES:SUBLANES + tr, :] * cw_ref[CONV_K - 1:CONV_K, :]
    for d in range(1, CONV_K):
        acc = acc + conv_scr[SUBLANES - d:SUBLANES - d + tr, :] * cw_ref[CONV_K - 1 - d:CONV_K - d, :]
    conv_scr[0:SUBLANES, :] = conv_scr[tr:tr + SUBLANES, :]
    qkv = acc * _sigmoid(acc)

    ab = ab_ref[...]
    sp_c = _softplus(ab + hp_ref[1:2, :])
    g_col = -jnp.exp(hp_ref[0:1, :]) * sp_c
    beta_col = _sigmoid(ab)
    abt = abt_ref[0]
    sp_r = _softplus(abt + hpt_ref[:, 1:2])
    g_row = -jnp.exp(hpt_ref[:, 0:1]) * sp_r

    ri = lax.broadcasted_iota(jnp.int32, (tr, tr), 0)
    ci = lax.broadcasted_iota(jnp.int32, (tr, tr), 1)
    same = jnp.right_shift(ri, CHUNK_SHIFT) == jnp.right_shift(ci, CHUNK_SHIFT)
    lower = jnp.where(same & (ri >= ci), 1.0, 0.0).astype(F32)
    upper = jnp.where(same & (ri <= ci), 1.0, 0.0).astype(F32)
    gc_col = jnp.dot(lower, g_col, preferred_element_type=F32, precision=lax.Precision.HIGHEST)
    gc_row = jnp.dot(g_row, upper, preferred_element_type=F32, precision=lax.Precision.HIGHEST)

    incl = same & (ri >= ci)
    strict = same & (ri > ci)
    eye = jnp.where(ri == ci, 1.0, 0.0).astype(F32)
    n_sq = CHUNK_SHIFT - 1

    heads = range(B_HEADS)
    qs, ks, vs, betas, gccs, decays, k_bs, kbs, ps, ts = [], [], [], [], [], [], [], [], [], []
    for hd in heads:
        l0 = hd * B_HEAD_DIM
        q = qkv[:, l0:l0 + B_HEAD_DIM]
        k = qkv[:, B_WIDTH + l0:B_WIDTH + l0 + B_HEAD_DIM]
        qs.append(q * lax.rsqrt(jnp.sum(q * q, axis=-1, keepdims=True) + RMS_EPS) * (B_HEAD_DIM ** -0.5))
        ks.append(k * lax.rsqrt(jnp.sum(k * k, axis=-1, keepdims=True) + RMS_EPS))
        vs.append(qkv[:, 2 * B_WIDTH + l0:2 * B_WIDTH + l0 + B_HEAD_DIM])
        betas.append(beta_col[:, B_HEADS + hd:B_HEADS + hd + 1])
        gccs.append(gc_col[:, hd:hd + 1])
        gcr = gc_row[hd:hd + 1, :]
        decays.append(jnp.where(incl, jnp.exp(jnp.where(incl, gccs[hd] - gcr, 0.0)), 0.0))
        k_bs.append(ks[hd].astype(BF16))
        kbs.append(ks[hd] * betas[hd])
    for hd in heads:
        a_mat = jnp.where(strict, _dot_nt(kbs[hd].astype(BF16), k_bs[hd]) * decays[hd], 0.0)
        ps.append(-a_mat)
        ts.append(eye - a_mat)
    for _ in range(n_sq):
        for hd in heads:
            pb = ps[hd].astype(BF16)
            ps[hd] = _dot(pb, pb)
        for hd in heads:
            ts[hd] = ts[hd] + _dot(ts[hd].astype(BF16), ps[hd].astype(BF16))
    us, ws, intras, q_decs = [], [], [], []
    for hd in heads:
        t_b = ts[hd].astype(BF16)
        e_gc = jnp.exp(gccs[hd])
        us.append(_dot(t_b, (vs[hd] * betas[hd]).astype(BF16)))
        ws.append(_dot(t_b, (kbs[hd] * e_gc).astype(BF16)).astype(BF16))
        intras.append(jnp.where(incl, _dot_nt(qs[hd].astype(BF16), k_bs[hd]) * decays[hd], 0.0).astype(BF16))
        q_decs.append((qs[hd] * e_gc).astype(BF16))
    for c in range(n_chunks):
        r0 = c * CHUNK
        for hd in heads:
            l0 = hd * B_HEAD_DIM
            g_last = gccs[hd][r0 + CHUNK - 1:r0 + CHUNK, :]
            k_dec = (ks[hd][r0:r0 + CHUNK] * jnp.exp(g_last - gccs[hd][r0:r0 + CHUNK])).astype(BF16)
            state = state_scr[hd]
            state_b = state.astype(BF16)
            v_new = (us[hd][r0:r0 + CHUNK] - _dot(ws[hd][r0:r0 + CHUNK], state_b)).astype(BF16)
            out = _dot(q_decs[hd][r0:r0 + CHUNK], state_b) + _dot(intras[hd][r0:r0 + CHUNK, r0:r0 + CHUNK], v_new)
            state_scr[hd] = state * jnp.exp(g_last) + _dot_tn(k_dec, v_new)
            z = h_ref[r0:r0 + CHUNK, 3 * B_WIDTH + l0:3 * B_WIDTH + l0 + B_HEAD_DIM].astype(F32)
            o_n = out * lax.rsqrt(jnp.mean(out * out, axis=-1, keepdims=True) + RMS_EPS) * nw_ref[...]
            o_ref[r0:r0 + CHUNK, l0:l0 + B_HEAD_DIM] = (o_n * (z * _sigmoid(z))).astype(o_ref.dtype)


def _gdn(h_b, ab, abt, conv_w, hp, hpt, norm_w, bsz, seq):
    tr = _row_tile(seq, GDN_ROWS)
    n_s = seq // tr
    return pl.pallas_call(
        _gdn_kernel,
        grid=(bsz, n_s),
        in_specs=[pl.BlockSpec((tr, 4 * B_WIDTH), lambda b, s: (b * n_s + s, 0)),
                  pl.BlockSpec((tr, LANES), lambda b, s: (b * n_s + s, 0)),
                  pl.BlockSpec((1, SUBLANES, tr), lambda b, s: (b, 0, s)),
                  pl.BlockSpec((CONV_K, 3 * B_WIDTH), lambda b, s: (0, 0)),
                  pl.BlockSpec((2, LANES), lambda b, s: (0, 0)),
                  pl.BlockSpec((SUBLANES, 2), lambda b, s: (0, 0)),
                  pl.BlockSpec((1, B_HEAD_DIM), lambda b, s: (0, 0))],
        out_specs=pl.BlockSpec((tr, B_WIDTH), lambda b, s: (b * n_s + s, 0)),
        out_shape=jax.ShapeDtypeStruct((bsz * seq, B_WIDTH), BF16),
        scratch_shapes=[pltpu.VMEM((tr + 2 * SUBLANES, 3 * B_WIDTH), F32),
                        pltpu.VMEM((B_HEADS, B_HEAD_DIM, B_HEAD_DIM), F32)],
        compiler_params=_params("parallel", "arbitrary"),
        name="gdn",
    )(h_b, ab, abt, conv_w, hp, hpt, norm_w)


def _mla_up_kernel(h_ref, qn_ref, kvn_ref, wq_ref, wkv_ref, cs_ref, q_ref, k_ref, v_ref):
    h = h_ref[...].astype(F32)
    cs = cs_ref[...]

    def rms(x, g):
        return x * lax.rsqrt(jnp.mean(x * x, axis=-1, keepdims=True) + RMS_EPS) * g

    def rope(t):
        r = t * cs
        return r + pltpu.roll(r, C_ROPE, 1)

    cq = rms(h[:, 0:Q_LORA], qn_ref[...]).astype(BF16)
    ckv = rms(h[:, Q_LORA:Q_LORA + KV_LORA], kvn_ref[...]).astype(BF16)
    k_pe = rope(h[:, Q_LORA + KV_LORA:Q_LORA + KV_LORA + 2 * C_ROPE])[:, 0:C_ROPE].astype(BF16)
    qf = _dot(cq, wq_ref[...]) * (C_QK ** -0.5 * LOG2_E)
    kv = _dot(ckv, wkv_ref[...])
    for hd in range(C_HEADS):
        q0 = hd * (C_NOPE + 2 * C_ROPE)
        q_ref[0, hd, :, 0:C_NOPE] = qf[:, q0:q0 + C_NOPE].astype(BF16)
        q_ref[0, hd, :, C_NOPE:C_QK] = rope(qf[:, q0 + C_NOPE:q0 + C_NOPE + 2 * C_ROPE])[:, 0:C_ROPE].astype(BF16)
        k0 = hd * (C_NOPE + C_V)
        k_ref[0, hd, :, 0:C_NOPE] = kv[:, k0:k0 + C_NOPE].astype(BF16)
        k_ref[0, hd, :, C_NOPE:C_QK] = k_pe
        v_ref[0, hd, :, :] = kv[:, k0 + C_NOPE:k0 + C_NOPE + C_V].astype(BF16)


def _mla_up(h, q_norm, kv_norm, w_uq, w_ukv, cs, bsz, seq, tm):
    n_s = seq // tm
    hw = h.shape[1]
    qk_shape = jax.ShapeDtypeStruct((bsz, C_HEADS, seq, C_QK), BF16)
    return pl.pallas_call(
        _mla_up_kernel,
        grid=(bsz, n_s),
        in_specs=[pl.BlockSpec((tm, hw), lambda b, s: (b * n_s + s, 0)),
                  pl.BlockSpec((1, Q_LORA), lambda b, s: (0, 0)),
                  pl.BlockSpec((1, KV_LORA), lambda b, s: (0, 0)),
                  pl.BlockSpec(w_uq.shape, lambda b, s: (0, 0)),
                  pl.BlockSpec(w_ukv.shape, lambda b, s: (0, 0)),
                  pl.BlockSpec((tm, LANES), lambda b, s: (s, 0))],
        out_specs=[pl.BlockSpec((1, C_HEADS, tm, C_QK), lambda b, s: (b, 0, s, 0)),
                   pl.BlockSpec((1, C_HEADS, tm, C_QK), lambda b, s: (b, 0, s, 0)),
                   pl.BlockSpec((1, C_HEADS, tm, C_V), lambda b, s: (b, 0, s, 0))],
        out_shape=[qk_shape, qk_shape, jax.ShapeDtypeStruct((bsz, C_HEADS, seq, C_V), BF16)],
        compiler_params=_params("parallel", "parallel"),
        name="mla_up",
    )(h, q_norm, kv_norm, w_uq, w_ukv, cs)


def _attn_kernel(q_ref, k_ref, v_ref, o_ref):
    nh, tq = q_ref.shape[1], q_ref.shape[2]
    qi = pl.program_id(2)

    def step(hd, k0, carry, mask):
        m, l, acc = carry
        s = _dot_nt(q_ref[0, hd], k_ref[0, hd, pl.ds(k0, tq), :])
        if mask is not None:
            s = jnp.where(mask, s, NEG_BIG)
        m_new = jnp.maximum(m, jnp.max(s, axis=-1, keepdims=True))
        alpha = jnp.exp2(m - m_new)
        p = jnp.exp2(s - m_new)
        l = alpha * l + jnp.sum(p, axis=-1, keepdims=True)
        acc = alpha * acc + _dot(p.astype(BF16), v_ref[0, hd, pl.ds(k0, tq), :])
        return m_new, l, acc

    def body(j, carry):
        k0 = pl.multiple_of(j * tq, tq)
        return tuple(step(hd, k0, carry[hd], None) for hd in range(nh))

    init = tuple((jnp.full((tq, 1), NEG_BIG, F32), jnp.zeros((tq, 1), F32), jnp.zeros((tq, C_V), F32))
                 for _ in range(nh))
    carry = lax.fori_loop(0, qi, body, init)
    k0 = pl.multiple_of(qi * tq, tq)
    ri = jnp.right_shift(lax.broadcasted_iota(jnp.int32, (tq, tq), 0), CHUNK_SHIFT)
    ci = jnp.right_shift(lax.broadcasted_iota(jnp.int32, (tq, tq), 1), CHUNK_SHIFT)
    mask = ci <= ri
    for hd in range(nh):
        _, l, acc = step(hd, k0, carry[hd], mask)
        o_ref[0, :, hd * C_V:(hd + 1) * C_V] = (acc / l).astype(o_ref.dtype)


def _attn(q, k, v):
    bsz, _, seq, _ = q.shape
    tq = _row_tile(seq, ATTN_ROWS)
    nh = ATTN_HEADS_PER_STEP
    return pl.pallas_call(
        _attn_kernel,
        grid=(bsz, C_HEADS // nh, seq // tq),
        in_specs=[pl.BlockSpec((1, nh, tq, C_QK), lambda b, h, i: (b, h, i, 0)),
                  pl.BlockSpec((1, nh, seq, C_QK), lambda b, h, i: (b, h, 0, 0)),
                  pl.BlockSpec((1, nh, seq, C_V), lambda b, h, i: (b, h, 0, 0))],
        out_specs=pl.BlockSpec((1, tq, nh * C_V), lambda b, h, i: (b, i, h)),
        out_shape=jax.ShapeDtypeStruct((bsz, seq, C_HEADS * C_V), BF16),
        compiler_params=_params("parallel", "parallel", "arbitrary"),
        name="attn",
    )(q, k, v)


def _proj_ln_router_kernel(n_in, *refs):
    a_refs = refs[0:n_in]
    w_refs = refs[n_in:2 * n_in]
    x_ref, g_ref, b_ref, wr_ref, br_ref = refs[2 * n_in:2 * n_in + 5]
    x1_ref, x1b_ref, meta_ref, gate_ref, cnt_ref = refs[2 * n_in + 5:2 * n_in + 10]
    base_scr = refs[2 * n_in + 10]
    tm = x_ref.shape[0]
    i = pl.program_id(0)

    @pl.when(i == 0)
    def _():
        base_scr[...] = jnp.zeros(base_scr.shape, F32)

    mix = _dot(a_refs[0][...], w_refs[0][...])
    for j in range(1, n_in):
        mix = mix + _dot(a_refs[j][...], w_refs[j][...])
    x1 = _layer_norm_rows(DEEPNORM_ALPHA * x_ref[...] + mix, g_ref[...], b_ref[...])
    x1_ref[...] = x1
    x1b = x1.astype(BF16)
    x1b_ref[...] = x1b

    logits = _dot(x1b, wr_ref[...]) + br_ref[...]
    lane = lax.broadcasted_iota(jnp.int32, (tm, LANES), 1)
    lane_f = lane.astype(F32)
    work = logits
    cnt = jnp.zeros((tm, LANES), F32)
    tops, idxs, hots = [], [], []
    for _ in range(TOP_K):
        mx = jnp.max(work, axis=-1, keepdims=True)
        idx_f = jnp.min(jnp.where(work == mx, lane_f, float(LANES)), axis=-1, keepdims=True)
        idx = idx_f.astype(jnp.int32)
        hot = lane == idx
        work = jnp.where(hot, NEG_BIG * 2.0, work)
        cnt = cnt + jnp.where(hot, 1.0, 0.0)
        tops.append(mx)
        idxs.append(idx)
        hots.append(hot)
    exps = [jnp.exp(t - tops[0]) for t in tops]
    denom = exps[0] + exps[1] + exps[2] + exps[3]

    ri = lax.broadcasted_iota(jnp.int32, (tm, tm), 0)
    ci = lax.broadcasted_iota(jnp.int32, (tm, tm), 1)
    before = jnp.where(ri > ci, 1.0, 0.0).astype(BF16)
    rank = _dot(before, cnt.astype(BF16)) + base_scr[0:1, :]
    meta = jnp.zeros((tm, LANES), jnp.int32)
    gates = jnp.zeros((tm, LANES), F32)
    for kk in range(TOP_K):
        pos = jnp.sum(jnp.where(hots[kk], rank, 0.0), axis=-1, keepdims=True).astype(jnp.int32)
        meta = jnp.where(lane == kk, idxs[kk], meta)
        meta = jnp.where(lane == TOP_K + kk, pos, meta)
        gates = jnp.where(lane == kk, exps[kk] / denom, gates)
    meta_ref[...] = meta
    gate_ref[...] = gates
    total = base_scr[0:1, :] + jnp.sum(cnt, axis=0, keepdims=True)
    base_scr[...] = jnp.broadcast_to(total, base_scr.shape)
    cnt_ref[...] = jnp.broadcast_to(total, cnt_ref.shape)


def _proj_ln_router(a_list, w_list, x, ln_g, ln_b, w_r, b_r, tm):
    t = x.shape[0]
    n_in = len(a_list)
    in_specs = [pl.BlockSpec((tm, a.shape[1]), lambda i: (i, 0)) for a in a_list]
    in_specs += [pl.BlockSpec(w.shape, lambda i: (0, 0)) for w in w_list]
    in_specs += [pl.BlockSpec((tm, D_MODEL), lambda i: (i, 0)),
                 pl.BlockSpec((1, D_MODEL), lambda i: (0, 0)),
                 pl.BlockSpec((1, D_MODEL), lambda i: (0, 0)),
                 pl.BlockSpec((D_MODEL, LANES), lambda i: (0, 0)),
                 pl.BlockSpec((1, LANES), lambda i: (0, 0))]
    return pl.pallas_call(
        functools.partial(_proj_ln_router_kernel, n_in),
        grid=(t // tm,),
        in_specs=in_specs,
        out_specs=[pl.BlockSpec((tm, D_MODEL), lambda i: (i, 0)),
                   pl.BlockSpec((tm, D_MODEL), lambda i: (i, 0)),
                   pl.BlockSpec((tm, LANES), lambda i: (i, 0)),
                   pl.BlockSpec((tm, LANES), lambda i: (i, 0)),
                   pl.BlockSpec((SUBLANES, LANES), lambda i: (0, 0))],
        out_shape=[jax.ShapeDtypeStruct((t, D_MODEL), F32),
                   jax.ShapeDtypeStruct((t, D_MODEL), BF16),
                   jax.ShapeDtypeStruct((t, LANES), jnp.int32),
                   jax.ShapeDtypeStruct((t, LANES), F32),
                   jax.ShapeDtypeStruct((SUBLANES, LANES), F32)],
        scratch_shapes=[pltpu.VMEM((SUBLANES, LANES), F32)],
        compiler_params=_params("arbitrary"),
        name="proj_ln_router",
    )(*a_list, *w_list, x, ln_g, ln_b, w_r, b_r)


def _moe_kernel(blk_ref, exp_ref, lo_ref, hi_ref, x_ref, wgu_ref, wd_ref, bg_ref, bu_ref, bd_ref, o_ref,
                wg_scr, wu_scr, wd_scr, t_scr):
    i = pl.program_id(0)
    tm = x_ref.shape[0]
    blk = blk_ref[i]
    lo = lo_ref[i]
    hi = hi_ref[i]
    prev = jnp.maximum(i - 1, 0)
    first = jnp.logical_or(i == 0, blk != blk_ref[prev])
    new_expert = jnp.logical_or(i == 0, exp_ref[i] != exp_ref[prev])

    @pl.when(new_expert)
    def _():
        pc = t_scr.shape[1]
        half = pc // 2
        for c in range(2 * D_FF // pc):
            for kc in range(D_MODEL // LANES):
                k0 = kc * LANES
                t_scr[kc] = wgu_ref[0, 0, k0:k0 + LANES, c * pc:(c + 1) * pc].T
                wg_scr[c * half:(c + 1) * half, k0:k0 + LANES] = t_scr[kc, pl.ds(0, half, stride=2), :].astype(BF16)
                wu_scr[c * half:(c + 1) * half, k0:k0 + LANES] = t_scr[kc, pl.ds(1, half, stride=2), :].astype(BF16)
        wd_scr[...] = wd_ref[0, 0].astype(BF16)

    @pl.when(first)
    def _():
        o_ref[...] = jnp.zeros(o_ref.shape, o_ref.dtype)

    @pl.when(hi > lo)
    def _():
        x = x_ref[...]
        glu = jnp.minimum(_dot_nt(x, wg_scr[...]) + bg_ref[0, 0], SWIGLU_LIMIT)
        lin = jnp.clip(_dot_nt(x, wu_scr[...]) + bu_ref[0, 0], -SWIGLU_LIMIT, SWIGLU_LIMIT)
        act = glu * _sigmoid(SWIGLU_ALPHA * glu) * (lin + 1.0)
        y = _dot(act.astype(BF16), wd_scr[...]) + bd_ref[0, 0]
        row = blk * tm + lax.broadcasted_iota(jnp.int32, (tm, 1), 0)
        keep = jnp.logical_and(row >= lo, row < hi)
        o_ref[...] = jnp.where(keep, y.astype(o_ref.dtype), o_ref[...])


def _moe_sorted(layer, xs, counts, w_gate_up, b_gate, b_up, w_down, b_down):
    n_rows = xs.shape[0]
    tm = _row_tile(n_rows, MOE_ROWS)
    items = _moe_items(counts, n_rows, tm)
    n_items = items[0].shape[0]

    def by_expert(i, blk, ex, lo, hi):
        return (layer, ex[i], 0, 0)

    def by_block(i, blk, ex, lo, hi):
        return (blk[i], 0)

    grid_spec = pltpu.PrefetchScalarGridSpec(
        num_scalar_prefetch=4,
        grid=(n_items,),
        in_specs=[pl.BlockSpec((tm, D_MODEL), by_block),
                  pl.BlockSpec((1, 1, D_MODEL, 2 * D_FF), by_expert),
                  pl.BlockSpec((1, 1, D_FF, D_MODEL), by_expert),
                  pl.BlockSpec((1, 1, 1, D_FF), by_expert),
                  pl.BlockSpec((1, 1, 1, D_FF), by_expert),
                  pl.BlockSpec((1, 1, 1, D_MODEL), by_expert)],
        out_specs=pl.BlockSpec((tm, D_MODEL), by_block),
        scratch_shapes=[pltpu.VMEM((D_FF, D_MODEL), BF16),
                        pltpu.VMEM((D_FF, D_MODEL), BF16),
                        pltpu.VMEM((D_FF, D_MODEL), BF16),
                        pltpu.VMEM((D_MODEL // LANES, MOE_PREP_COLS, LANES), F32)],
    )
    return pl.pallas_call(
        _moe_kernel,
        grid_spec=grid_spec,
        out_shape=jax.ShapeDtypeStruct((n_rows, D_MODEL), BF16),
        compiler_params=_params("arbitrary", vmem=MOE_VMEM_LIMIT),
        name="moe_ffn",
    )(*items, xs, w_gate_up, w_down, b_gate, b_up, b_down)


def _moe_items(counts, n_rows, tm):
    n_blocks = n_rows // tm
    start = jnp.cumsum(counts) - counts
    cuts = jnp.sort(jnp.concatenate([jnp.arange(n_blocks, dtype=jnp.int32) * tm, start.astype(jnp.int32)]))
    lo = cuts
    hi = jnp.concatenate([cuts[1:], jnp.array([n_rows], jnp.int32)])
    blk = jnp.minimum(lo // tm, n_blocks - 1)
    ex = jnp.clip(jnp.sum((start[None, :] <= lo[:, None]).astype(jnp.int32), axis=1) - 1, 0, N_EXPERTS - 1)
    return blk.astype(jnp.int32), ex.astype(jnp.int32), lo.astype(jnp.int32), hi.astype(jnp.int32)


def _combine_ln_kernel(x_ref, y_ref, gate_ref, g_ref, b_ref, o_ref):
    gates = gate_ref[...]
    ffn = y_ref[0].astype(F32) * gates[:, 0:1]
    for kk in range(1, TOP_K):
        ffn = ffn + y_ref[kk].astype(F32) * gates[:, kk:kk + 1]
    o_ref[...] = _layer_norm_rows(DEEPNORM_ALPHA * x_ref[...] + ffn, g_ref[...], b_ref[...])


def _combine_ln(x1, y4, gates, ln_g, ln_b, tm):
    t = x1.shape[0]
    return pl.pallas_call(
        _combine_ln_kernel,
        grid=(t // tm,),
        in_specs=[pl.BlockSpec((tm, D_MODEL), lambda i: (i, 0)),
                  pl.BlockSpec((TOP_K, tm, D_MODEL), lambda i: (0, i, 0)),
                  pl.BlockSpec((tm, LANES), lambda i: (i, 0)),
                  pl.BlockSpec((1, D_MODEL), lambda i: (0, 0)),
                  pl.BlockSpec((1, D_MODEL), lambda i: (0, 0))],
        out_specs=pl.BlockSpec((tm, D_MODEL), lambda i: (i, 0)),
        out_shape=jax.ShapeDtypeStruct((t, D_MODEL), F32),
        compiler_params=_params("parallel"),
        name="combine_ln",
    )(x1, y4, gates, ln_g, ln_b)


def _moe_layer(layer, x1, x1b, meta, gates, counts, ln_g, ln_b, w_gate_up, b_gate, b_up, w_down, b_down):
    t = x1.shape[0]
    n_rows = t * TOP_K
    assert N_EXPERTS * n_rows < 2 ** 31
    e = meta[:, 0:TOP_K]
    pos = meta[:, TOP_K:2 * TOP_K]
    cnt = counts[0, 0:N_EXPERTS].astype(jnp.int32)
    start = jnp.cumsum(cnt) - cnt
    experts = jnp.arange(N_EXPERTS, dtype=jnp.int32)
    dest = pos + jnp.sum(jnp.where(e[:, :, None] == experts, start, 0), axis=-1)
    keys = e.reshape(-1) * n_rows + jnp.arange(n_rows, dtype=jnp.int32)
    tok_sorted = jnp.remainder(jnp.sort(keys), n_rows) // TOP_K
    xs = x1b[tok_sorted]
    ys = _moe_sorted(layer, xs, cnt, w_gate_up, b_gate, b_up, w_down, b_down)
    y4 = ys[dest.T]
    return _combine_ln(x1, y4, gates, ln_g, ln_b, _row_tile(t, ROWS))


def kernel(x, even_w_in, even_sgu_ln_g, even_sgu_ln_b, even_sgu_w, even_sgu_b, even_conv_w, even_a_log, even_dt_bias, even_gdn_norm, even_w_out, odd_w_in, odd_q_norm, odd_w_uq, odd_kv_norm, odd_w_ukv, odd_w_out, ln_mix_g, ln_mix_b, ln_ffn_g, ln_ffn_b, moe_w_router, moe_b_router, moe_w_gate_up, moe_b_gate_up, moe_w_down, moe_b_down):
    bsz, seq, d = x.shape
    t = bsz * seq
    xt = x.reshape(t, d)
    tm = _row_tile(t, ROWS)
    b_gate = moe_b_gate_up[:, :, None, 0::2]
    b_up = moe_b_gate_up[:, :, None, 1::2]
    b_down = moe_b_down[:, :, None, :]

    inv_freq = ROPE_THETA ** (-jnp.arange(0, C_ROPE, 2, dtype=F32) / C_ROPE)
    ang = jnp.arange(seq, dtype=F32)[:, None] * inv_freq[None, :]
    cs = jnp.concatenate([jnp.cos(ang), jnp.cos(ang), jnp.sin(ang), jnp.sin(ang)], axis=1)

    pos_chunk = np.arange(A_BLOCK) // CHUNK
    allowed = jnp.asarray(pos_chunk[None, :] <= pos_chunk[:, None])

    def rot_cols(w):
        half = C_ROPE // 2
        return jnp.concatenate([-w[..., half:], w[..., :half]], axis=-1)

    for layer in range(DEPTH):
        i = layer // 2
        if layer % 2 == 0:
            w_in = even_w_in[i]
            w_a = w_in[:, 0:2 * A_WIDTH].astype(BF16)
            w_b = jnp.pad(w_in[:, 2 * A_WIDTH:], ((0, 0), (0, LANES - 2 * B_HEADS))).astype(BF16)
            sw = jnp.where(allowed[None], even_sgu_w[i], 0).astype(BF16)
            sb = jnp.broadcast_to(even_sgu_b[i][:, :, None], (A_GROUPS, A_BLOCK, A_GROUP_DIM)).astype(F32)
            y_a = _sgu(xt, w_a, even_sgu_ln_g[i][None], even_sgu_ln_b[i][None], sw, sb, tm)
            h_b, ab = _mm_split(xt, w_b, 4 * B_WIDTH, BF16, F32, tm)
            abt = ab.reshape(bsz, seq, LANES)[:, :, 0:SUBLANES].transpose(0, 2, 1)
            hp = jnp.pad(jnp.stack([even_a_log[i], even_dt_bias[i]]), ((0, 0), (0, LANES - B_HEADS)))
            hpt = jnp.pad(jnp.stack([even_a_log[i], even_dt_bias[i]], axis=1), ((0, SUBLANES - B_HEADS), (0, 0)))
            y_b = _gdn(h_b, ab, abt, even_conv_w[i], hp, hpt, even_gdn_norm[i][None], bsz, seq)
            w_out = even_w_out[i].astype(BF16)
            a_list = [y_a, y_b]
            w_list = [w_out[0:A_WIDTH], w_out[A_WIDTH:]]
        else:
            w_in = odd_w_in[i]
            w_pe = w_in[:, Q_LORA + KV_LORA:]
            w_in2 = jnp.concatenate([w_in, rot_cols(w_pe)], axis=1).astype(BF16)
            w_uq = odd_w_uq[i].reshape(Q_LORA, C_HEADS, C_QK)
            w_uq2 = jnp.concatenate([w_uq, rot_cols(w_uq[..., C_NOPE:])], axis=-1)
            w_uq2 = w_uq2.reshape(Q_LORA, C_HEADS * (C_NOPE + 2 * C_ROPE)).astype(BF16)
            h = _mm(xt, w_in2, BF16, tm)
            q, k, v = _mla_up(h, odd_q_norm[i][None], odd_kv_norm[i][None], w_uq2, odd_w_ukv[i].astype(BF16), cs,
                              bsz, seq, _row_tile(seq, ROWS))
            o = _attn(q, k, v)
            a_list = [o.reshape(t, C_HEADS * C_V)]
            w_list = [odd_w_out[i].astype(BF16)]

        w_r = jnp.pad(moe_w_router[layer], ((0, 0), (0, LANES - N_EXPERTS))).astype(BF16)
        b_r = jnp.pad(moe_b_router[layer], (0, LANES - N_EXPERTS), constant_values=NEG_BIG)[None]
        x1, x1b, meta, gates, counts = _proj_ln_router(a_list, w_list, xt, ln_mix_g[layer][None],
                                                       ln_mix_b[layer][None], w_r, b_r, tm)
        xt = _moe_layer(layer, x1, x1b, meta, gates, counts, ln_ffn_g[layer][None], ln_ffn_b[layer][None],
                        moe_w_gate_up, b_gate, b_up, moe_w_down, b_down)
    return xt.reshape(bsz, seq, d)
```

```python
import functools

import jax
import jax.numpy as jnp
import numpy as np
from jax import lax
from jax.experimental import pallas as pl
from jax.experimental.pallas import tpu as pltpu

F32 = jnp.float32
BF16 = jnp.bfloat16

D_MODEL = 1024
DEPTH = 4
CHUNK = 64
CHUNK_SHIFT = CHUNK.bit_length() - 1
DEEPNORM_ALPHA = (2 * DEPTH) ** 0.25
LN_EPS = 1e-5
RMS_EPS = 1e-6

A_BLOCK = 128
A_GROUPS = 4
A_WIDTH = D_MODEL // 2
A_GROUP_DIM = A_WIDTH // A_GROUPS

B_HEADS = 4
B_HEAD_DIM = D_MODEL // 8
B_WIDTH = B_HEADS * B_HEAD_DIM
CONV_K = 4

C_HEADS = 8
C_NOPE = 128
C_ROPE = 64
C_V = D_MODEL // C_HEADS
C_QK = C_NOPE + C_ROPE
Q_LORA = 3 * D_MODEL // 8
KV_LORA = D_MODEL // 4
ROPE_THETA = 10000.0

N_EXPERTS = 32
TOP_K = 4
D_FF = D_MODEL
SWIGLU_LIMIT = 7.0
SWIGLU_ALPHA = 1.702

LANES = 128
SUBLANES = 8
VMEM_LIMIT = 48 * 1024 * 1024
MOE_VMEM_LIMIT = 58 * 1024 * 1024
NEG_BIG = -1e30
LOG2_E = 1.4426950408889634

ROWS = 512
GDN_ROWS = 256
ATTN_ROWS = 512
ATTN_HEADS_PER_STEP = 4
ATTN_SUB = 128
MOE_ROWS = 512
MOE_PREP_COLS = 512


def _row_tile(t, pref):
    tm = pref
    while t % tm:
        tm //= 2
    return tm


def _params(*sem, vmem=VMEM_LIMIT):
    return pltpu.CompilerParams(dimension_semantics=sem, vmem_limit_bytes=vmem)


def _dot(a, b):
    return jnp.dot(a, b, preferred_element_type=F32)


def _dot_nt(a, b):
    return lax.dot_general(a, b, (((1,), (1,)), ((), ())), preferred_element_type=F32)


def _dot_tn(a, b):
    return lax.dot_general(a, b, (((0,), (0,)), ((), ())), preferred_element_type=F32)


def _sigmoid(x):
    return 1.0 / (1.0 + jnp.exp(-x))


def _softplus(x):
    return jnp.maximum(x, 0.0) + jnp.log1p(jnp.exp(-jnp.abs(x)))


def _layer_norm_rows(y, g, b):
    mu = jnp.mean(y, axis=-1, keepdims=True)
    d = y - mu
    var = jnp.mean(d * d, axis=-1, keepdims=True)
    return d * lax.rsqrt(var + LN_EPS) * g + b


def _mm_kernel(x_ref, w_ref, o_ref):
    o_ref[...] = _dot(x_ref[...].astype(BF16), w_ref[...]).astype(o_ref.dtype)


def _mm(x, w, out_dtype, tm):
    t, k = x.shape
    n = w.shape[1]
    return pl.pallas_call(
        _mm_kernel,
        grid=(t // tm,),
        in_specs=[pl.BlockSpec((tm, k), lambda i: (i, 0)),
                  pl.BlockSpec((k, n), lambda i: (0, 0))],
        out_specs=pl.BlockSpec((tm, n), lambda i: (i, 0)),
        out_shape=jax.ShapeDtypeStruct((t, n), out_dtype),
        compiler_params=_params("parallel"),
        name="mm",
    )(x, w)


def _mm_split_kernel(x_ref, w_ref, o0_ref, o1_ref):
    n0 = o0_ref.shape[1]
    h = _dot(x_ref[...].astype(BF16), w_ref[...])
    o0_ref[...] = h[:, 0:n0].astype(o0_ref.dtype)
    o1_ref[...] = h[:, n0:].astype(o1_ref.dtype)


def _mm_split(x, w, n0, dtype0, dtype1, tm):
    t, k = x.shape
    n = w.shape[1]
    return pl.pallas_call(
        _mm_split_kernel,
        grid=(t // tm,),
        in_specs=[pl.BlockSpec((tm, k), lambda i: (i, 0)),
                  pl.BlockSpec((k, n), lambda i: (0, 0))],
        out_specs=[pl.BlockSpec((tm, n0), lambda i: (i, 0)),
                   pl.BlockSpec((tm, n - n0), lambda i: (i, 0))],
        out_shape=[jax.ShapeDtypeStruct((t, n0), dtype0), jax.ShapeDtypeStruct((t, n - n0), dtype1)],
        compiler_params=_params("parallel"),
        name="mm_split",
    )(x, w)


def _gelu(x):
    return 0.5 * x * (1.0 + lax.erf(x * (2.0 ** -0.5)))


def _sgu_kernel(x_ref, w_ref, g_ref, b_ref, sw_ref, sb_ref, o_ref):
    tm = x_ref.shape[0]
    h = _dot(x_ref[...].astype(BF16), w_ref[...])
    for grp in range(A_GROUPS):
        c0 = grp * A_GROUP_DIM
        u = _gelu(h[:, c0:c0 + A_GROUP_DIM])
        v = _gelu(h[:, A_WIDTH + c0:A_WIDTH + c0 + A_GROUP_DIM])
        vn = _layer_norm_rows(v, g_ref[:, c0:c0 + A_GROUP_DIM], b_ref[:, c0:c0 + A_GROUP_DIM]).astype(BF16)
        for blk in range(tm // A_BLOCK):
            r0 = blk * A_BLOCK
            mixed = _dot(sw_ref[grp], vn[r0:r0 + A_BLOCK]) + sb_ref[grp]
            o_ref[r0:r0 + A_BLOCK, c0:c0 + A_GROUP_DIM] = (u[r0:r0 + A_BLOCK] * mixed).astype(o_ref.dtype)


def _sgu(x, w_a, ln_g, ln_b, sw, sb, tm):
    t = x.shape[0]
    return pl.pallas_call(
        _sgu_kernel,
        grid=(t // tm,),
        in_specs=[pl.BlockSpec((tm, D_MODEL), lambda i: (i, 0)),
                  pl.BlockSpec((D_MODEL, 2 * A_WIDTH), lambda i: (0, 0)),
                  pl.BlockSpec((1, A_WIDTH), lambda i: (0, 0)),
                  pl.BlockSpec((1, A_WIDTH), lambda i: (0, 0)),
                  pl.BlockSpec((A_GROUPS, A_BLOCK, A_BLOCK), lambda i: (0, 0, 0)),
                  pl.BlockSpec((A_GROUPS, A_BLOCK, A_GROUP_DIM), lambda i: (0, 0, 0))],
        out_specs=pl.BlockSpec((tm, A_WIDTH), lambda i: (i, 0)),
        out_shape=jax.ShapeDtypeStruct((t, A_WIDTH), BF16),
        compiler_params=_params("parallel"),
        name="sgu",
    )(x, w_a, ln_g, ln_b, sw, sb)


def _gdn_kernel(h_ref, ab_ref, abt_ref, cw_ref, hp_ref, hpt_ref, nw_ref, o_ref, conv_scr, state_scr):
    tr = h_ref.shape[0]
    n_chunks = tr // CHUNK
    s_idx = pl.program_id(1)

    @pl.when(s_idx == 0)
    def _():
        conv_scr[0:SUBLANES, :] = jnp.zeros((SUBLANES, 3 * B_WIDTH), F32)
        state_scr[...] = jnp.zeros(state_scr.shape, F32)

    conv_scr[SUBLANES:SUBLANES + tr, :] = h_ref[:, 0:3 * B_WIDTH].astype(F32)
    acc = conv_scr[SUBLANES:SUBLANES + tr, :] * cw_ref[CONV_K - 1:CONV_K, :]
    for d in range(1, CONV_K):
        acc = acc + conv_scr[SUBLANES - d:SUBLANES - d + tr, :] * cw_ref[CONV_K - 1 - d:CONV_K - d, :]
    conv_scr[0:SUBLANES, :] = conv_scr[tr:tr + SUBLANES, :]
    qkv = acc * _sigmoid(acc)

    ab = ab_ref[...]
    sp_c = _softplus(ab + hp_ref[1:2, :])
    g_col = -jnp.exp(hp_ref[0:1, :]) * sp_c
    beta_col = _sigmoid(ab)
    abt = abt_ref[0]
    sp_r = _softplus(abt + hpt_ref[:, 1:2])
    g_row = -jnp.exp(hpt_ref[:, 0:1]) * sp_r

    ri = lax.broadcasted_iota(jnp.int32, (tr, tr), 0)
    ci = lax.broadcasted_iota(jnp.int32, (tr, tr), 1)
    same = jnp.right_shift(ri, CHUNK_SHIFT) == jnp.right_shift(ci, CHUNK_SHIFT)
    lower = jnp.where(same & (ri >= ci), 1.0, 0.0).astype(F32)
    upper = jnp.where(same & (ri <= ci), 1.0, 0.0).astype(F32)
    gc_col = jnp.dot(lower, g_col, preferred_element_type=F32, precision=lax.Precision.HIGHEST)
    gc_row = jnp.dot(g_row, upper, preferred_element_type=F32, precision=lax.Precision.HIGHEST)

    incl = same & (ri >= ci)
    strict = same & (ri > ci)
    eye = jnp.where(ri == ci, 1.0, 0.0).astype(F32)
    n_sq = CHUNK_SHIFT - 1

    heads = range(B_HEADS)
    qs, ks, vs, betas, gccs, decays, k_bs, kbs, ps, ts = [], [], [], [], [], [], [], [], [], []
    for hd in heads:
        l0 = hd * B_HEAD_DIM
        q = qkv[:, l0:l0 + B_HEAD_DIM]
        k = qkv[:, B_WIDTH + l0:B_WIDTH + l0 + B_HEAD_DIM]
        qs.append(q * lax.rsqrt(jnp.sum(q * q, axis=-1, keepdims=True) + RMS_EPS) * (B_HEAD_DIM ** -0.5))
        ks.append(k * lax.rsqrt(jnp.sum(k * k, axis=-1, keepdims=True) + RMS_EPS))
        vs.append(qkv[:, 2 * B_WIDTH + l0:2 * B_WIDTH + l0 + B_HEAD_DIM])
        betas.append(beta_col[:, B_HEADS + hd:B_HEADS + hd + 1])
        gccs.append(gc_col[:, hd:hd + 1])
        gcr = gc_row[hd:hd + 1, :]
        decays.append(jnp.where(incl, jnp.exp(jnp.where(incl, gccs[hd] - gcr, 0.0)), 0.0))
        k_bs.append(ks[hd].astype(BF16))
        kbs.append(ks[hd] * betas[hd])
    for hd in heads:
        a_mat = jnp.where(strict, _dot_nt(kbs[hd].astype(BF16), k_bs[hd]) * decays[hd], 0.0)
        ps.append(-a_mat)
        ts.append(eye - a_mat)
    for _ in range(n_sq):
        for hd in heads:
            pb = ps[hd].astype(BF16)
            ps[hd] = _dot(pb, pb)
        for hd in heads:
            ts[hd] = ts[hd] + _dot(ts[hd].astype(BF16), ps[hd].astype(BF16))
    us, ws, intras, q_decs = [], [], [], []
    for hd in heads:
        t_b = ts[hd].astype(BF16)
        e_gc = jnp.exp(gccs[hd])
        us.append(_dot(t_b, (vs[hd] * betas[hd]).astype(BF16)))
        ws.append(_dot(t_b, (kbs[hd] * e_gc).astype(BF16)).astype(BF16))
        intras.append(jnp.where(incl, _dot_nt(qs[hd].astype(BF16), k_bs[hd]) * decays[hd], 0.0).astype(BF16))
        q_decs.append((qs[hd] * e_gc).astype(BF16))
    for c in range(n_chunks):
        r0 = c * CHUNK
        for hd in heads:
            l0 = hd * B_HEAD_DIM
            g_last = gccs[hd][r0 + CHUNK - 1:r0 + CHUNK, :]
            k_dec = (ks[hd][r0:r0 + CHUNK] * jnp.exp(g_last - gccs[hd][r0:r0 + CHUNK])).astype(BF16)
            state = state_scr[hd]
            state_b = state.astype(BF16)
            v_new = (us[hd][r0:r0 + CHUNK] - _dot(ws[hd][r0:r0 + CHUNK], state_b)).astype(BF16)
            out = _dot(q_decs[hd][r0:r0 + CHUNK], state_b) + _dot(intras[hd][r0:r0 + CHUNK, r0:r0 + CHUNK], v_new)
            state_scr[hd] = state * jnp.exp(g_last) + _dot_tn(k_dec, v_new)
            z = h_ref[r0:r0 + CHUNK, 3 * B_WIDTH + l0:3 * B_WIDTH + l0 + B_HEAD_DIM].astype(F32)
            o_n = out * lax.rsqrt(jnp.mean(out * out, axis=-1, keepdims=True) + RMS_EPS) * nw_ref[...]
            o_ref[r0:r0 + CHUNK, l0:l0 + B_HEAD_DIM] = (o_n * (z * _sigmoid(z))).astype(o_ref.dtype)


def _gdn(h_b, ab, abt, conv_w, hp, hpt, norm_w, bsz, seq):
    tr = _row_tile(seq, GDN_ROWS)
    n_s = seq // tr
    return pl.pallas_call(
        _gdn_kernel,
        grid=(bsz, n_s),
        in_specs=[pl.BlockSpec((tr, 4 * B_WIDTH), lambda b, s: (b * n_s + s, 0)),
                  pl.BlockSpec((tr, LANES), lambda b, s: (b * n_s + s, 0)),
                  pl.BlockSpec((1, SUBLANES, tr), lambda b, s: (b, 0, s)),
                  pl.BlockSpec((CONV_K, 3 * B_WIDTH), lambda b, s: (0, 0)),
                  pl.BlockSpec((2, LANES), lambda b, s: (0, 0)),
                  pl.BlockSpec((SUBLANES, 2), lambda b, s: (0, 0)),
                  pl.BlockSpec((1, B_HEAD_DIM), lambda b, s: (0, 0))],
        out_specs=pl.BlockSpec((tr, B_WIDTH), lambda b, s: (b * n_s + s, 0)),
        out_shape=jax.ShapeDtypeStruct((bsz * seq, B_WIDTH), BF16),
        scratch_shapes=[pltpu.VMEM((tr + 2 * SUBLANES, 3 * B_WIDTH), F32),
                        pltpu.VMEM((B_HEADS, B_HEAD_DIM, B_HEAD_DIM), F32)],
        compiler_params=_params("parallel", "arbitrary"),
        name="gdn",
    )(h_b, ab, abt, conv_w, hp, hpt, norm_w)


def _mla_up_kernel(h_ref, qn_ref, kvn_ref, wq_ref, wkv_ref, cs_ref, q_ref, k_ref, v_ref):
    h = h_ref[...].astype(F32)
    cs = cs_ref[...]

    def rms(x, g):
        return x * lax.rsqrt(jnp.mean(x * x, axis=-1, keepdims=True) + RMS_EPS) * g

    def rope(t):
        r = t * cs
        return r + pltpu.roll(r, C_ROPE, 1)

    cq = rms(h[:, 0:Q_LORA], qn_ref[...]).astype(BF16)
    ckv = rms(h[:, Q_LORA:Q_LORA + KV_LORA], kvn_ref[...]).astype(BF16)
    k_pe = rope(h[:, Q_LORA + KV_LORA:Q_LORA + KV_LORA + 2 * C_ROPE])[:, 0:C_ROPE].astype(BF16)
    qf = _dot(cq, wq_ref[...]) * (C_QK ** -0.5 * LOG2_E)
    kv = _dot(ckv, wkv_ref[...])
    for hd in range(C_HEADS):
        q0 = hd * (C_NOPE + 2 * C_ROPE)
        q_ref[0, hd, :, 0:C_NOPE] = qf[:, q0:q0 + C_NOPE].astype(BF16)
        q_ref[0, hd, :, C_NOPE:C_QK] = rope(qf[:, q0 + C_NOPE:q0 + C_NOPE + 2 * C_ROPE])[:, 0:C_ROPE].astype(BF16)
        k0 = hd * (C_NOPE + C_V)
        k_ref[0, hd, :, 0:C_NOPE] = kv[:, k0:k0 + C_NOPE].astype(BF16)
        k_ref[0, hd, :, C_NOPE:C_QK] = k_pe
        v_ref[0, hd, :, :] = kv[:, k0 + C_NOPE:k0 + C_NOPE + C_V].astype(BF16)


def _mla_up(h, q_norm, kv_norm, w_uq, w_ukv, cs, bsz, seq, tm):
    n_s = seq // tm
    hw = h.shape[1]
    qk_shape = jax.ShapeDtypeStruct((bsz, C_HEADS, seq, C_QK), BF16)
    return pl.pallas_call(
        _mla_up_kernel,
        grid=(bsz, n_s),
        in_specs=[pl.BlockSpec((tm, hw), lambda b, s: (b * n_s + s, 0)),
                  pl.BlockSpec((1, Q_LORA), lambda b, s: (0, 0)),
                  pl.BlockSpec((1, KV_LORA), lambda b, s: (0, 0)),
                  pl.BlockSpec(w_uq.shape, lambda b, s: (0, 0)),
                  pl.BlockSpec(w_ukv.shape, lambda b, s: (0, 0)),
                  pl.BlockSpec((tm, LANES), lambda b, s: (s, 0))],
        out_specs=[pl.BlockSpec((1, C_HEADS, tm, C_QK), lambda b, s: (b, 0, s, 0)),
                   pl.BlockSpec((1, C_HEADS, tm, C_QK), lambda b, s: (b, 0, s, 0)),
                   pl.BlockSpec((1, C_HEADS, tm, C_V), lambda b, s: (b, 0, s, 0))],
        out_shape=[qk_shape, qk_shape, jax.ShapeDtypeStruct((bsz, C_HEADS, seq, C_V), BF16)],
        compiler_params=_params("parallel", "parallel"),
        name="mla_up",
    )(h, q_norm, kv_norm, w_uq, w_ukv, cs)


def _attn_kernel(q_ref, k_ref, v_ref, o_ref, s_scr, p_scr, m_scr, l_scr, a_scr, acc_scr):
    nh, tq = q_ref.shape[1], q_ref.shape[2]
    n_lt = tq // LANES
    qi = pl.program_id(2)

    m_scr[...] = jnp.full(m_scr.shape, NEG_BIG, F32)
    l_scr[...] = jnp.zeros(l_scr.shape, F32)
    acc_scr[...] = jnp.zeros(acc_scr.shape, F32)

    def scores(hd, k0):
        s_scr[hd] = _dot_nt(q_ref[0, hd], k_ref[0, hd, pl.ds(k0, tq), :])

    def softmax(hd, masked):
        for r0 in range(0, tq, ATTN_SUB):
            rows = slice(r0, r0 + ATTN_SUB)
            tiles = []
            for c in range(n_lt):
                s = s_scr[hd, rows, c * LANES:(c + 1) * LANES]
                if masked:
                    ri = jnp.right_shift(r0 + lax.broadcasted_iota(jnp.int32, (ATTN_SUB, LANES), 0), CHUNK_SHIFT)
                    ci = jnp.right_shift(c * LANES + lax.broadcasted_iota(jnp.int32, (ATTN_SUB, LANES), 1),
                                         CHUNK_SHIFT)
                    s = jnp.where(ci <= ri, s, NEG_BIG)
                tiles.append(s)
            mx = tiles[0]
            for c in range(1, n_lt):
                mx = jnp.maximum(mx, tiles[c])
            m_prev = m_scr[hd, rows, :]
            m_new = jnp.maximum(m_prev, jnp.max(mx, axis=-1, keepdims=True))
            alpha = jnp.exp2(m_prev - m_new)
            psum = jnp.zeros((ATTN_SUB, LANES), F32)
            for c in range(n_lt):
                p = jnp.exp2(tiles[c] - m_new)
                psum = psum + p
                p_scr[hd, rows, c * LANES:(c + 1) * LANES] = p.astype(BF16)
            m_scr[hd, rows, :] = m_new
            a_scr[hd, rows, :] = alpha
            l_scr[hd, rows, :] = alpha * l_scr[hd, rows, :] + jnp.sum(psum, axis=-1, keepdims=True)

    def values(hd, k0):
        pv = _dot(p_scr[hd], v_ref[0, hd, pl.ds(k0, tq), :])
        acc_scr[hd] = a_scr[hd] * acc_scr[hd] + pv

    def block(k0, masked):
        for hd in range(nh):
            scores(hd, k0)
        for hd in range(nh):
            softmax(hd, masked)
        for hd in range(nh):
            values(hd, k0)

    def body(j, carry):
        block(pl.multiple_of(j * tq, tq), False)
        return carry

    lax.fori_loop(0, qi, body, 0)
    block(pl.multiple_of(qi * tq, tq), True)
    for hd in range(nh):
        o_ref[0, :, hd * C_V:(hd + 1) * C_V] = (acc_scr[hd] / l_scr[hd]).astype(o_ref.dtype)


def _attn(q, k, v):
    bsz, _, seq, _ = q.shape
    tq = _row_tile(seq, ATTN_ROWS)
    nh = ATTN_HEADS_PER_STEP
    return pl.pallas_call(
        _attn_kernel,
        grid=(bsz, C_HEADS // nh, seq // tq),
        in_specs=[pl.BlockSpec((1, nh, tq, C_QK), lambda b, h, i: (b, h, i, 0)),
                  pl.BlockSpec((1, nh, seq, C_QK), lambda b, h, i: (b, h, 0, 0)),
                  pl.BlockSpec((1, nh, seq, C_V), lambda b, h, i: (b, h, 0, 0))],
        out_specs=pl.BlockSpec((1, tq, nh * C_V), lambda b, h, i: (b, i, h)),
        out_shape=jax.ShapeDtypeStruct((bsz, seq, C_HEADS * C_V), BF16),
        scratch_shapes=[pltpu.VMEM((nh, tq, tq), F32),
                        pltpu.VMEM((nh, tq, tq), BF16),
                        pltpu.VMEM((nh, tq, LANES), F32),
                        pltpu.VMEM((nh, tq, LANES), F32),
                        pltpu.VMEM((nh, tq, LANES), F32),
                        pltpu.VMEM((nh, tq, C_V), F32)],
        compiler_params=_params("parallel", "parallel", "arbitrary"),
        name="attn",
    )(q, k, v)


def _proj_ln_router_kernel(n_in, *refs):
    a_refs = refs[0:n_in]
    w_refs = refs[n_in:2 * n_in]
    x_ref, g_ref, b_ref, wr_ref, br_ref = refs[2 * n_in:2 * n_in + 5]
    x1_ref, x1b_ref, meta_ref, gate_ref, cnt_ref = refs[2 * n_in + 5:2 * n_in + 10]
    base_scr = refs[2 * n_in + 10]
    tm = x_ref.shape[0]
    i = pl.program_id(0)

    @pl.when(i == 0)
    def _():
        base_scr[...] = jnp.zeros(base_scr.shape, F32)

    mix = _dot(a_refs[0][...], w_refs[0][...])
    for j in range(1, n_in):
        mix = mix + _dot(a_refs[j][...], w_refs[j][...])
    x1 = _layer_norm_rows(DEEPNORM_ALPHA * x_ref[...] + mix, g_ref[...], b_ref[...])
    x1_ref[...] = x1
    x1b = x1.astype(BF16)
    x1b_ref[...] = x1b

    work = _dot_nt(wr_ref[...], x1b) + br_ref[:, 0:1]
    eidx = lax.broadcasted_iota(jnp.int32, (N_EXPERTS, tm), 0).astype(F32)
    cnt = jnp.zeros((N_EXPERTS, tm), F32)
    tops, idxs, hots = [], [], []
    for _ in range(TOP_K):
        mx = jnp.max(work, axis=0, keepdims=True)
        idx = jnp.min(jnp.where(work == mx, eidx, float(N_EXPERTS)), axis=0, keepdims=True)
        hot = eidx == idx
        work = jnp.where(hot, NEG_BIG, work)
        cnt = cnt + jnp.where(hot, 1.0, 0.0)
        tops.append(mx)
        idxs.append(idx)
        hots.append(hot)
    exps = [jnp.exp(t - tops[0]) for t in tops]
    denom = exps[0] + exps[1] + exps[2] + exps[3]

    ri = lax.broadcasted_iota(jnp.int32, (tm, tm), 0)
    ci = lax.broadcasted_iota(jnp.int32, (tm, tm), 1)
    before = jnp.where(ri < ci, 1.0, 0.0).astype(BF16)
    base = base_scr[:, 0:1]
    rank = _dot(cnt.astype(BF16), before) + base
    row = lax.broadcasted_iota(jnp.int32, (2 * TOP_K, tm), 0)
    meta = jnp.zeros((2 * TOP_K, tm), jnp.int32)
    gates = jnp.zeros((2 * TOP_K, tm), F32)
    for kk in range(TOP_K):
        pos = jnp.sum(jnp.where(hots[kk], rank, 0.0), axis=0, keepdims=True).astype(jnp.int32)
        meta = jnp.where(row == kk, idxs[kk].astype(jnp.int32), meta)
        meta = jnp.where(row == TOP_K + kk, pos, meta)
        gates = jnp.where(row == kk, exps[kk] / denom, gates)
    meta_ref[...] = meta
    gate_ref[...] = gates
    total = base + jnp.sum(cnt, axis=1, keepdims=True)
    base_scr[...] = jnp.broadcast_to(total, base_scr.shape)
    cnt_ref[...] = jnp.broadcast_to(total, cnt_ref.shape)


def _proj_ln_router(a_list, w_list, x, ln_g, ln_b, w_r, b_r, tm):
    t = x.shape[0]
    n_in = len(a_list)
    in_specs = [pl.BlockSpec((tm, a.shape[1]), lambda i: (i, 0)) for a in a_list]
    in_specs += [pl.BlockSpec(w.shape, lambda i: (0, 0)) for w in w_list]
    in_specs += [pl.BlockSpec((tm, D_MODEL), lambda i: (i, 0)),
                 pl.BlockSpec((1, D_MODEL), lambda i: (0, 0)),
                 pl.BlockSpec((1, D_MODEL), lambda i: (0, 0)),
                 pl.BlockSpec((N_EXPERTS, D_MODEL), lambda i: (0, 0)),
                 pl.BlockSpec((N_EXPERTS, LANES), lambda i: (0, 0))]
    return pl.pallas_call(
        functools.partial(_proj_ln_router_kernel, n_in),
        grid=(t // tm,),
        in_specs=in_specs,
        out_specs=[pl.BlockSpec((tm, D_MODEL), lambda i: (i, 0)),
                   pl.BlockSpec((tm, D_MODEL), lambda i: (i, 0)),
                   pl.BlockSpec((2 * TOP_K, tm), lambda i: (0, i)),
                   pl.BlockSpec((2 * TOP_K, tm), lambda i: (0, i)),
                   pl.BlockSpec((N_EXPERTS, LANES), lambda i: (0, 0))],
        out_shape=[jax.ShapeDtypeStruct((t, D_MODEL), F32),
                   jax.ShapeDtypeStruct((t, D_MODEL), BF16),
                   jax.ShapeDtypeStruct((2 * TOP_K, t), jnp.int32),
                   jax.ShapeDtypeStruct((2 * TOP_K, t), F32),
                   jax.ShapeDtypeStruct((N_EXPERTS, LANES), F32)],
        scratch_shapes=[pltpu.VMEM((N_EXPERTS, LANES), F32)],
        compiler_params=_params("arbitrary"),
        name="proj_ln_router",
    )(*a_list, *w_list, x, ln_g, ln_b, w_r, b_r)


def _moe_kernel(blk_ref, exp_ref, lo_ref, hi_ref, x_ref, wgu_ref, wd_ref, bg_ref, bu_ref, bd_ref, o_ref,
                wg_scr, wu_scr, wd_scr, t_scr):
    i = pl.program_id(0)
    tm = x_ref.shape[0]
    blk = blk_ref[i]
    lo = lo_ref[i]
    hi = hi_ref[i]
    prev = jnp.maximum(i - 1, 0)
    first = jnp.logical_or(i == 0, blk != blk_ref[prev])
    new_expert = jnp.logical_or(i == 0, exp_ref[i] != exp_ref[prev])

    @pl.when(new_expert)
    def _():
        pc = t_scr.shape[1]
        half = pc // 2
        for c in range(2 * D_FF // pc):
            for kc in range(D_MODEL // LANES):
                k0 = kc * LANES
                t_scr[kc] = wgu_ref[0, 0, k0:k0 + LANES, c * pc:(c + 1) * pc].T
                wg_scr[c * half:(c + 1) * half, k0:k0 + LANES] = t_scr[kc, pl.ds(0, half, stride=2), :].astype(BF16)
                wu_scr[c * half:(c + 1) * half, k0:k0 + LANES] = t_scr[kc, pl.ds(1, half, stride=2), :].astype(BF16)
        wd_scr[...] = wd_ref[0, 0].astype(BF16)

    @pl.when(first)
    def _():
        o_ref[...] = jnp.zeros(o_ref.shape, o_ref.dtype)

    @pl.when(hi > lo)
    def _():
        x = x_ref[...]
        glu = jnp.minimum(_dot_nt(x, wg_scr[...]) + bg_ref[0, 0], SWIGLU_LIMIT)
        lin = jnp.clip(_dot_nt(x, wu_scr[...]) + bu_ref[0, 0], -SWIGLU_LIMIT, SWIGLU_LIMIT)
        act = glu * _sigmoid(SWIGLU_ALPHA * glu) * (lin + 1.0)
        y = _dot(act.astype(BF16), wd_scr[...]) + bd_ref[0, 0]
        row = blk * tm + lax.broadcasted_iota(jnp.int32, (tm, 1), 0)
        keep = jnp.logical_and(row >= lo, row < hi)
        o_ref[...] = jnp.where(keep, y.astype(o_ref.dtype), o_ref[...])


def _moe_sorted(layer, xs, counts, w_gate_up, b_gate, b_up, w_down, b_down):
    n_rows = xs.shape[0]
    tm = _row_tile(n_rows, MOE_ROWS)
    items = _moe_items(counts, n_rows, tm)
    n_items = items[0].shape[0]

    def by_expert(i, blk, ex, lo, hi):
        return (layer, ex[i], 0, 0)

    def by_block(i, blk, ex, lo, hi):
        return (blk[i], 0)

    grid_spec = pltpu.PrefetchScalarGridSpec(
        num_scalar_prefetch=4,
        grid=(n_items,),
        in_specs=[pl.BlockSpec((tm, D_MODEL), by_block),
                  pl.BlockSpec((1, 1, D_MODEL, 2 * D_FF), by_expert),
                  pl.BlockSpec((1, 1, D_FF, D_MODEL), by_expert),
                  pl.BlockSpec((1, 1, 1, D_FF), by_expert),
                  pl.BlockSpec((1, 1, 1, D_FF), by_expert),
                  pl.BlockSpec((1, 1, 1, D_MODEL), by_expert)],
        out_specs=pl.BlockSpec((tm, D_MODEL), by_block),
        scratch_shapes=[pltpu.VMEM((D_FF, D_MODEL), BF16),
                        pltpu.VMEM((D_FF, D_MODEL), BF16),
                        pltpu.VMEM((D_FF, D_MODEL), BF16),
                        pltpu.VMEM((D_MODEL // LANES, MOE_PREP_COLS, LANES), F32)],
    )
    return pl.pallas_call(
        _moe_kernel,
        grid_spec=grid_spec,
        out_shape=jax.ShapeDtypeStruct((n_rows, D_MODEL), BF16),
        compiler_params=_params("arbitrary", vmem=MOE_VMEM_LIMIT),
        name="moe_ffn",
    )(*items, xs, w_gate_up, w_down, b_gate, b_up, b_down)


def _moe_items(counts, n_rows, tm):
    n_blocks = n_rows // tm
    start = jnp.cumsum(counts) - counts
    cuts = jnp.sort(jnp.concatenate([jnp.arange(n_blocks, dtype=jnp.int32) * tm, start.astype(jnp.int32)]))
    lo = cuts
    hi = jnp.concatenate([cuts[1:], jnp.array([n_rows], jnp.int32)])
    blk = jnp.minimum(lo // tm, n_blocks - 1)
    ex = jnp.clip(jnp.sum((start[None, :] <= lo[:, None]).astype(jnp.int32), axis=1) - 1, 0, N_EXPERTS - 1)
    return blk.astype(jnp.int32), ex.astype(jnp.int32), lo.astype(jnp.int32), hi.astype(jnp.int32)


def _combine_ln_kernel(x_ref, y_ref, gate_ref, g_ref, b_ref, o_ref):
    gates = gate_ref[...]
    ffn = y_ref[0].astype(F32) * gates[:, 0:1]
    for kk in range(1, TOP_K):
        ffn = ffn + y_ref[kk].astype(F32) * gates[:, kk:kk + 1]
    o_ref[...] = _layer_norm_rows(DEEPNORM_ALPHA * x_ref[...] + ffn, g_ref[...], b_ref[...])


def _combine_ln(x1, y4, gates, ln_g, ln_b, tm):
    t = x1.shape[0]
    return pl.pallas_call(
        _combine_ln_kernel,
        grid=(t // tm,),
        in_specs=[pl.BlockSpec((tm, D_MODEL), lambda i: (i, 0)),
                  pl.BlockSpec((TOP_K, tm, D_MODEL), lambda i: (0, i, 0)),
                  pl.BlockSpec((tm, 2 * TOP_K), lambda i: (i, 0)),
                  pl.BlockSpec((1, D_MODEL), lambda i: (0, 0)),
                  pl.BlockSpec((1, D_MODEL), lambda i: (0, 0))],
        out_specs=pl.BlockSpec((tm, D_MODEL), lambda i: (i, 0)),
        out_shape=jax.ShapeDtypeStruct((t, D_MODEL), F32),
        compiler_params=_params("parallel"),
        name="combine_ln",
    )(x1, y4, gates, ln_g, ln_b)


def _moe_layer(layer, x1, x1b, meta_t, gates_t, counts, ln_g, ln_b, w_gate_up, b_gate, b_up, w_down, b_down):
    t = x1.shape[0]
    assert N_EXPERTS * t < 2 ** 31
    e = meta_t[0:TOP_K]
    pos = meta_t[TOP_K:2 * TOP_K]
    cnt = counts[:, 0].astype(jnp.int32)
    start = jnp.cumsum(cnt) - cnt
    experts = jnp.arange(N_EXPERTS, dtype=jnp.int32)
    dest = pos + jnp.sum(jnp.where(e[:, :, None] == experts, start, 0), axis=-1)
    keys = e * t + jnp.arange(t, dtype=jnp.int32)[None, :]
    tok_sorted = jnp.remainder(jnp.sort(keys.reshape(-1)), t)
    xs = x1b[tok_sorted]
    ys = _moe_sorted(layer, xs, cnt, w_gate_up, b_gate, b_up, w_down, b_down)
    y4 = ys[dest]
    return _combine_ln(x1, y4, gates_t.T, ln_g, ln_b, _row_tile(t, ROWS))


def kernel(x, even_w_in, even_sgu_ln_g, even_sgu_ln_b, even_sgu_w, even_sgu_b, even_conv_w, even_a_log, even_dt_bias, even_gdn_norm, even_w_out, odd_w_in, odd_q_norm, odd_w_uq, odd_kv_norm, odd_w_ukv, odd_w_out, ln_mix_g, ln_mix_b, ln_ffn_g, ln_ffn_b, moe_w_router, moe_b_router, moe_w_gate_up, moe_b_gate_up, moe_w_down, moe_b_down):
    bsz, seq, d = x.shape
    t = bsz * seq
    xt = x.reshape(t, d)
    tm = _row_tile(t, ROWS)
    b_gate = moe_b_gate_up[:, :, None, 0::2]
    b_up = moe_b_gate_up[:, :, None, 1::2]
    b_down = moe_b_down[:, :, None, :]

    inv_freq = ROPE_THETA ** (-jnp.arange(0, C_ROPE, 2, dtype=F32) / C_ROPE)
    ang = jnp.arange(seq, dtype=F32)[:, None] * inv_freq[None, :]
    cs = jnp.concatenate([jnp.cos(ang), jnp.cos(ang), jnp.sin(ang), jnp.sin(ang)], axis=1)

    pos_chunk = np.arange(A_BLOCK) // CHUNK
    allowed = jnp.asarray(pos_chunk[None, :] <= pos_chunk[:, None])

    def rot_cols(w):
        half = C_ROPE // 2
        return jnp.concatenate([-w[..., half:], w[..., :half]], axis=-1)

    for layer in range(DEPTH):
        i = layer // 2
        if layer % 2 == 0:
            w_in = even_w_in[i]
            w_a = w_in[:, 0:2 * A_WIDTH].astype(BF16)
            w_b = jnp.pad(w_in[:, 2 * A_WIDTH:], ((0, 0), (0, LANES - 2 * B_HEADS))).astype(BF16)
            sw = jnp.where(allowed[None], even_sgu_w[i], 0).astype(BF16)
            sb = jnp.broadcast_to(even_sgu_b[i][:, :, None], (A_GROUPS, A_BLOCK, A_GROUP_DIM)).astype(F32)
            y_a = _sgu(xt, w_a, even_sgu_ln_g[i][None], even_sgu_ln_b[i][None], sw, sb, tm)
            h_b, ab = _mm_split(xt, w_b, 4 * B_WIDTH, BF16, F32, tm)
            abt = ab.reshape(bsz, seq, LANES)[:, :, 0:SUBLANES].transpose(0, 2, 1)
            hp = jnp.pad(jnp.stack([even_a_log[i], even_dt_bias[i]]), ((0, 0), (0, LANES - B_HEADS)))
            hpt = jnp.pad(jnp.stack([even_a_log[i], even_dt_bias[i]], axis=1), ((0, SUBLANES - B_HEADS), (0, 0)))
            y_b = _gdn(h_b, ab, abt, even_conv_w[i], hp, hpt, even_gdn_norm[i][None], bsz, seq)
            w_out = even_w_out[i].astype(BF16)
            a_list = [y_a, y_b]
            w_list = [w_out[0:A_WIDTH], w_out[A_WIDTH:]]
        else:
            w_in = odd_w_in[i]
            w_pe = w_in[:, Q_LORA + KV_LORA:]
            w_in2 = jnp.concatenate([w_in, rot_cols(w_pe)], axis=1).astype(BF16)
            w_uq = odd_w_uq[i].reshape(Q_LORA, C_HEADS, C_QK)
            w_uq2 = jnp.concatenate([w_uq, rot_cols(w_uq[..., C_NOPE:])], axis=-1)
            w_uq2 = w_uq2.reshape(Q_LORA, C_HEADS * (C_NOPE + 2 * C_ROPE)).astype(BF16)
            h = _mm(xt, w_in2, BF16, tm)
            q, k, v = _mla_up(h, odd_q_norm[i][None], odd_kv_norm[i][None], w_uq2, odd_w_ukv[i].astype(BF16), cs,
                              bsz, seq, _row_tile(seq, ROWS))
            o = _attn(q, k, v)
            a_list = [o.reshape(t, C_HEADS * C_V)]
            w_list = [odd_w_out[i].astype(BF16)]

        w_r = moe_w_router[layer].T.astype(BF16)
        b_r = jnp.broadcast_to(moe_b_router[layer][:, None], (N_EXPERTS, LANES))
        x1, x1b, meta, gates, counts = _proj_ln_router(a_list, w_list, xt, ln_mix_g[layer][None],
                                                       ln_mix_b[layer][None], w_r, b_r, tm)
        xt = _moe_layer(layer, x1, x1b, meta, gates, counts, ln_ffn_g[layer][None], ln_ffn_b[layer][None],
                        moe_w_gate_up, b_gate, b_up, moe_w_down, b_down)
    return xt.reshape(bsz, seq, d)
```

```python
import functools

import jax
import jax.numpy as jnp
import numpy as np
from jax import lax
from jax.experimental import pallas as pl
from jax.experimental.pallas import tpu as pltpu

F32 = jnp.float32
BF16 = jnp.bfloat16

D_MODEL = 1024
DEPTH = 4
CHUNK = 64
CHUNK_SHIFT = CHUNK.bit_length() - 1
DEEPNORM_ALPHA = (2 * DEPTH) ** 0.25
LN_EPS = 1e-5
RMS_EPS = 1e-6

A_BLOCK = 128
A_GROUPS = 4
A_WIDTH = D_MODEL // 2
A_GROUP_DIM = A_WIDTH // A_GROUPS

B_HEADS = 4
B_HEAD_DIM = D_MODEL // 8
B_WIDTH = B_HEADS * B_HEAD_DIM
CONV_K = 4

C_HEADS = 8
C_NOPE = 128
C_ROPE = 64
C_V = D_MODEL // C_HEADS
C_QK = C_NOPE + C_ROPE
Q_LORA = 3 * D_MODEL // 8
KV_LORA = D_MODEL // 4
ROPE_THETA = 10000.0

N_EXPERTS = 32
TOP_K = 4
D_FF = D_MODEL
SWIGLU_LIMIT = 7.0
SWIGLU_ALPHA = 1.702

LANES = 128
SUBLANES = 8
VMEM_LIMIT = 48 * 1024 * 1024
MOE_VMEM_LIMIT = 58 * 1024 * 1024
NEG_BIG = -1e30
LOG2_E = 1.4426950408889634

ROWS = 512
GDN_ROWS = 256
ATTN_ROWS = 512
ATTN_HEADS_PER_STEP = 4
ATTN_SUB = 128
MOE_ROWS = 512
MOE_PREP_COLS = 512
N_STREAMS = 2


def _row_tile(t, pref):
    tm = pref
    while t % tm:
        tm //= 2
    return tm


def _params(*sem, vmem=VMEM_LIMIT):
    return pltpu.CompilerParams(dimension_semantics=sem, vmem_limit_bytes=vmem)


def _dot(a, b):
    return jnp.dot(a, b, preferred_element_type=F32)


def _dot_nt(a, b):
    return lax.dot_general(a, b, (((1,), (1,)), ((), ())), preferred_element_type=F32)


def _dot_tn(a, b):
    return lax.dot_general(a, b, (((0,), (0,)), ((), ())), preferred_element_type=F32)


def _sigmoid(x):
    return 1.0 / (1.0 + jnp.exp(-x))


def _softplus(x):
    return jnp.maximum(x, 0.0) + jnp.log1p(jnp.exp(-jnp.abs(x)))


def _layer_norm_rows(y, g, b):
    mu = jnp.mean(y, axis=-1, keepdims=True)
    d = y - mu
    var = jnp.mean(d * d, axis=-1, keepdims=True)
    return d * lax.rsqrt(var + LN_EPS) * g + b


def _mm_kernel(x_ref, w_ref, o_ref):
    o_ref[...] = _dot(x_ref[...].astype(BF16), w_ref[...]).astype(o_ref.dtype)


def _mm(x, w, out_dtype, tm):
    t, k = x.shape
    n = w.shape[1]
    return pl.pallas_call(
        _mm_kernel,
        grid=(t // tm,),
        in_specs=[pl.BlockSpec((tm, k), lambda i: (i, 0)),
                  pl.BlockSpec((k, n), lambda i: (0, 0))],
        out_specs=pl.BlockSpec((tm, n), lambda i: (i, 0)),
        out_shape=jax.ShapeDtypeStruct((t, n), out_dtype),
        compiler_params=_params("parallel"),
        name="mm",
    )(x, w)


def _mm_split_kernel(x_ref, w_ref, o0_ref, o1_ref):
    n0 = o0_ref.shape[1]
    h = _dot(x_ref[...].astype(BF16), w_ref[...])
    o0_ref[...] = h[:, 0:n0].astype(o0_ref.dtype)
    o1_ref[...] = h[:, n0:].astype(o1_ref.dtype)


def _mm_split(x, w, n0, dtype0, dtype1, tm):
    t, k = x.shape
    n = w.shape[1]
    return pl.pallas_call(
        _mm_split_kernel,
        grid=(t // tm,),
        in_specs=[pl.BlockSpec((tm, k), lambda i: (i, 0)),
                  pl.BlockSpec((k, n), lambda i: (0, 0))],
        out_specs=[pl.BlockSpec((tm, n0), lambda i: (i, 0)),
                   pl.BlockSpec((tm, n - n0), lambda i: (i, 0))],
        out_shape=[jax.ShapeDtypeStruct((t, n0), dtype0), jax.ShapeDtypeStruct((t, n - n0), dtype1)],
        compiler_params=_params("parallel"),
        name="mm_split",
    )(x, w)


def _gelu(x):
    return 0.5 * x * (1.0 + lax.erf(x * (2.0 ** -0.5)))


def _sgu_kernel(x_ref, w_ref, g_ref, b_ref, sw_ref, sb_ref, o_ref):
    tm = x_ref.shape[0]
    h = _dot(x_ref[...].astype(BF16), w_ref[...])
    for grp in range(A_GROUPS):
        c0 = grp * A_GROUP_DIM
        u = _gelu(h[:, c0:c0 + A_GROUP_DIM])
        v = _gelu(h[:, A_WIDTH + c0:A_WIDTH + c0 + A_GROUP_DIM])
        vn = _layer_norm_rows(v, g_ref[:, c0:c0 + A_GROUP_DIM], b_ref[:, c0:c0 + A_GROUP_DIM]).astype(BF16)
        for blk in range(tm // A_BLOCK):
            r0 = blk * A_BLOCK
            mixed = _dot(sw_ref[grp], vn[r0:r0 + A_BLOCK]) + sb_ref[grp]
            o_ref[r0:r0 + A_BLOCK, c0:c0 + A_GROUP_DIM] = (u[r0:r0 + A_BLOCK] * mixed).astype(o_ref.dtype)


def _sgu(x, w_a, ln_g, ln_b, sw, sb, tm):
    t = x.shape[0]
    return pl.pallas_call(
        _sgu_kernel,
        grid=(t // tm,),
        in_specs=[pl.BlockSpec((tm, D_MODEL), lambda i: (i, 0)),
                  pl.BlockSpec((D_MODEL, 2 * A_WIDTH), lambda i: (0, 0)),
                  pl.BlockSpec((1, A_WIDTH), lambda i: (0, 0)),
                  pl.BlockSpec((1, A_WIDTH), lambda i: (0, 0)),
                  pl.BlockSpec((A_GROUPS, A_BLOCK, A_BLOCK), lambda i: (0, 0, 0)),
                  pl.BlockSpec((A_GROUPS, A_BLOCK, A_GROUP_DIM), lambda i: (0, 0, 0))],
        out_specs=pl.BlockSpec((tm, A_WIDTH), lambda i: (i, 0)),
        out_shape=jax.ShapeDtypeStruct((t, A_WIDTH), BF16),
        compiler_params=_params("parallel"),
        name="sgu",
    )(x, w_a, ln_g, ln_b, sw, sb)


def _gdn_kernel(h_ref, ab_ref, abt_ref, cw_ref, hp_ref, hpt_ref, nw_ref, o_ref, conv_scr, state_scr):
    tr = h_ref.shape[0]
    n_chunks = tr // CHUNK
    s_idx = pl.program_id(1)

    @pl.when(s_idx == 0)
    def _():
        conv_scr[0:SUBLANES, :] = jnp.zeros((SUBLANES, 3 * B_WIDTH), F32)
        state_scr[...] = jnp.zeros(state_scr.shape, F32)

    conv_scr[SUBLANES:SUBLANES + tr, :] = h_ref[:, 0:3 * B_WIDTH].astype(F32)
    acc = conv_scr[SUBLANES:SUBLANES + tr, :] * cw_ref[CONV_K - 1:CONV_K, :]
    for d in range(1, CONV_K):
        acc = acc + conv_scr[SUBLANES - d:SUBLANES - d + tr, :] * cw_ref[CONV_K - 1 - d:CONV_K - d, :]
    conv_scr[0:SUBLANES, :] = conv_scr[tr:tr + SUBLANES, :]
    qkv = acc * _sigmoid(acc)

    ab = ab_ref[...]
    sp_c = _softplus(ab + hp_ref[1:2, :])
    g_col = -jnp.exp(hp_ref[0:1, :]) * sp_c
    beta_col = _sigmoid(ab)
    abt = abt_ref[0]
    sp_r = _softplus(abt + hpt_ref[:, 1:2])
    g_row = -jnp.exp(hpt_ref[:, 0:1]) * sp_r

    ri = lax.broadcasted_iota(jnp.int32, (tr, tr), 0)
    ci = lax.broadcasted_iota(jnp.int32, (tr, tr), 1)
    same = jnp.right_shift(ri, CHUNK_SHIFT) == jnp.right_shift(ci, CHUNK_SHIFT)
    lower = jnp.where(same & (ri >= ci), 1.0, 0.0).astype(F32)
    upper = jnp.where(same & (ri <= ci), 1.0, 0.0).astype(F32)
    gc_col = jnp.dot(lower, g_col, preferred_element_type=F32, precision=lax.Precision.HIGHEST)
    gc_row = jnp.dot(g_row, upper, preferred_element_type=F32, precision=lax.Precision.HIGHEST)

    incl = same & (ri >= ci)
    strict = same & (ri > ci)
    eye = jnp.where(ri == ci, 1.0, 0.0).astype(F32)
    n_sq = CHUNK_SHIFT - 1

    heads = range(B_HEADS)
    qs, ks, vs, betas, gccs, decays, k_bs, kbs, ps, ts = [], [], [], [], [], [], [], [], [], []
    for hd in heads:
        l0 = hd * B_HEAD_DIM
        q = qkv[:, l0:l0 + B_HEAD_DIM]
        k = qkv[:, B_WIDTH + l0:B_WIDTH + l0 + B_HEAD_DIM]
        qs.append(q * lax.rsqrt(jnp.sum(q * q, axis=-1, keepdims=True) + RMS_EPS) * (B_HEAD_DIM ** -0.5))
        ks.append(k * lax.rsqrt(jnp.sum(k * k, axis=-1, keepdims=True) + RMS_EPS))
        vs.append(qkv[:, 2 * B_WIDTH + l0:2 * B_WIDTH + l0 + B_HEAD_DIM])
        betas.append(beta_col[:, B_HEADS + hd:B_HEADS + hd + 1])
        gccs.append(gc_col[:, hd:hd + 1])
        gcr = gc_row[hd:hd + 1, :]
        decays.append(jnp.where(incl, jnp.exp(jnp.where(incl, gccs[hd] - gcr, 0.0)), 0.0))
        k_bs.append(ks[hd].astype(BF16))
        kbs.append(ks[hd] * betas[hd])
    for hd in heads:
        a_mat = jnp.where(strict, _dot_nt(kbs[hd].astype(BF16), k_bs[hd]) * decays[hd], 0.0)
        ps.append(-a_mat)
        ts.append(eye - a_mat)
    for _ in range(n_sq):
        for hd in heads:
            pb = ps[hd].astype(BF16)
            ps[hd] = _dot(pb, pb)
        for hd in heads:
            ts[hd] = ts[hd] + _dot(ts[hd].astype(BF16), ps[hd].astype(BF16))
    us, ws, intras, q_decs = [], [], [], []
    for hd in heads:
        t_b = ts[hd].astype(BF16)
        e_gc = jnp.exp(gccs[hd])
        us.append(_dot(t_b, (vs[hd] * betas[hd]).astype(BF16)))
        ws.append(_dot(t_b, (kbs[hd] * e_gc).astype(BF16)).astype(BF16))
        intras.append(jnp.where(incl, _dot_nt(qs[hd].astype(BF16), k_bs[hd]) * decays[hd], 0.0).astype(BF16))
        q_decs.append((qs[hd] * e_gc).astype(BF16))
    for c in range(n_chunks):
        r0 = c * CHUNK
        for hd in heads:
            l0 = hd * B_HEAD_DIM
            g_last = gccs[hd][r0 + CHUNK - 1:r0 + CHUNK, :]
            k_dec = (ks[hd][r0:r0 + CHUNK] * jnp.exp(g_last - gccs[hd][r0:r0 + CHUNK])).astype(BF16)
            state = state_scr[hd]
            state_b = state.astype(BF16)
            v_new = (us[hd][r0:r0 + CHUNK] - _dot(ws[hd][r0:r0 + CHUNK], state_b)).astype(BF16)
            out = _dot(q_decs[hd][r0:r0 + CHUNK], state_b) + _dot(intras[hd][r0:r0 + CHUNK, r0:r0 + CHUNK], v_new)
            state_scr[hd] = state * jnp.exp(g_last) + _dot_tn(k_dec, v_new)
            z = h_ref[r0:r0 + CHUNK, 3 * B_WIDTH + l0:3 * B_WIDTH + l0 + B_HEAD_DIM].astype(F32)
            o_n = out * lax.rsqrt(jnp.mean(out * out, axis=-1, keepdims=True) + RMS_EPS) * nw_ref[...]
            o_ref[r0:r0 + CHUNK, l0:l0 + B_HEAD_DIM] = (o_n * (z * _sigmoid(z))).astype(o_ref.dtype)


def _gdn(h_b, ab, abt, conv_w, hp, hpt, norm_w, bsz, seq):
    tr = _row_tile(seq, GDN_ROWS)
    n_s = seq // tr
    return pl.pallas_call(
        _gdn_kernel,
        grid=(bsz, n_s),
        in_specs=[pl.BlockSpec((tr, 4 * B_WIDTH), lambda b, s: (b * n_s + s, 0)),
                  pl.BlockSpec((tr, LANES), lambda b, s: (b * n_s + s, 0)),
                  pl.BlockSpec((1, SUBLANES, tr), lambda b, s: (b, 0, s)),
                  pl.BlockSpec((CONV_K, 3 * B_WIDTH), lambda b, s: (0, 0)),
                  pl.BlockSpec((2, LANES), lambda b, s: (0, 0)),
                  pl.BlockSpec((SUBLANES, 2), lambda b, s: (0, 0)),
                  pl.BlockSpec((1, B_HEAD_DIM), lambda b, s: (0, 0))],
        out_specs=pl.BlockSpec((tr, B_WIDTH), lambda b, s: (b * n_s + s, 0)),
        out_shape=jax.ShapeDtypeStruct((bsz * seq, B_WIDTH), BF16),
        scratch_shapes=[pltpu.VMEM((tr + 2 * SUBLANES, 3 * B_WIDTH), F32),
                        pltpu.VMEM((B_HEADS, B_HEAD_DIM, B_HEAD_DIM), F32)],
        compiler_params=_params("parallel", "arbitrary"),
        name="gdn",
    )(h_b, ab, abt, conv_w, hp, hpt, norm_w)


def _mla_up_kernel(h_ref, qn_ref, kvn_ref, wq_ref, wkv_ref, cs_ref, q_ref, k_ref, v_ref):
    h = h_ref[...].astype(F32)
    cs = cs_ref[...]

    def rms(x, g):
        return x * lax.rsqrt(jnp.mean(x * x, axis=-1, keepdims=True) + RMS_EPS) * g

    def rope(t):
        r = t * cs
        return r + pltpu.roll(r, C_ROPE, 1)

    cq = rms(h[:, 0:Q_LORA], qn_ref[...]).astype(BF16)
    ckv = rms(h[:, Q_LORA:Q_LORA + KV_LORA], kvn_ref[...]).astype(BF16)
    k_pe = rope(h[:, Q_LORA + KV_LORA:Q_LORA + KV_LORA + 2 * C_ROPE])[:, 0:C_ROPE].astype(BF16)
    qf = _dot(cq, wq_ref[...]) * (C_QK ** -0.5 * LOG2_E)
    kv = _dot(ckv, wkv_ref[...])
    for hd in range(C_HEADS):
        q0 = hd * (C_NOPE + 2 * C_ROPE)
        q_ref[0, hd, :, 0:C_NOPE] = qf[:, q0:q0 + C_NOPE].astype(BF16)
        q_ref[0, hd, :, C_NOPE:C_QK] = rope(qf[:, q0 + C_NOPE:q0 + C_NOPE + 2 * C_ROPE])[:, 0:C_ROPE].astype(BF16)
        k0 = hd * (C_NOPE + C_V)
        k_ref[0, hd, :, 0:C_NOPE] = kv[:, k0:k0 + C_NOPE].astype(BF16)
        k_ref[0, hd, :, C_NOPE:C_QK] = k_pe
        v_ref[0, hd, :, :] = kv[:, k0 + C_NOPE:k0 + C_NOPE + C_V].astype(BF16)


def _mla_up(h, q_norm, kv_norm, w_uq, w_ukv, cs, bsz, seq, tm):
    n_s = seq // tm
    hw = h.shape[1]
    qk_shape = jax.ShapeDtypeStruct((bsz, C_HEADS, seq, C_QK), BF16)
    return pl.pallas_call(
        _mla_up_kernel,
        grid=(bsz, n_s),
        in_specs=[pl.BlockSpec((tm, hw), lambda b, s: (b * n_s + s, 0)),
                  pl.BlockSpec((1, Q_LORA), lambda b, s: (0, 0)),
                  pl.BlockSpec((1, KV_LORA), lambda b, s: (0, 0)),
                  pl.BlockSpec(w_uq.shape, lambda b, s: (0, 0)),
                  pl.BlockSpec(w_ukv.shape, lambda b, s: (0, 0)),
                  pl.BlockSpec((tm, LANES), lambda b, s: (s, 0))],
        out_specs=[pl.BlockSpec((1, C_HEADS, tm, C_QK), lambda b, s: (b, 0, s, 0)),
                   pl.BlockSpec((1, C_HEADS, tm, C_QK), lambda b, s: (b, 0, s, 0)),
                   pl.BlockSpec((1, C_HEADS, tm, C_V), lambda b, s: (b, 0, s, 0))],
        out_shape=[qk_shape, qk_shape, jax.ShapeDtypeStruct((bsz, C_HEADS, seq, C_V), BF16)],
        compiler_params=_params("parallel", "parallel"),
        name="mla_up",
    )(h, q_norm, kv_norm, w_uq, w_ukv, cs)


def _attn_kernel(q_ref, k_ref, v_ref, o_ref, s_scr, p_scr, m_scr, l_scr, a_scr, acc_scr):
    nh, tq = q_ref.shape[1], q_ref.shape[2]
    n_lt = tq // LANES
    qi = pl.program_id(2)

    m_scr[...] = jnp.full(m_scr.shape, NEG_BIG, F32)
    l_scr[...] = jnp.zeros(l_scr.shape, F32)
    acc_scr[...] = jnp.zeros(acc_scr.shape, F32)

    def scores(hd, k0):
        s_scr[hd] = _dot_nt(q_ref[0, hd], k_ref[0, hd, pl.ds(k0, tq), :])

    def softmax(hd, masked):
        for r0 in range(0, tq, ATTN_SUB):
            rows = slice(r0, r0 + ATTN_SUB)
            tiles = []
            for c in range(n_lt):
                s = s_scr[hd, rows, c * LANES:(c + 1) * LANES]
                if masked:
                    ri = jnp.right_shift(r0 + lax.broadcasted_iota(jnp.int32, (ATTN_SUB, LANES), 0), CHUNK_SHIFT)
                    ci = jnp.right_shift(c * LANES + lax.broadcasted_iota(jnp.int32, (ATTN_SUB, LANES), 1),
                                         CHUNK_SHIFT)
                    s = jnp.where(ci <= ri, s, NEG_BIG)
                tiles.append(s)
            mx = tiles[0]
            for c in range(1, n_lt):
                mx = jnp.maximum(mx, tiles[c])
            m_prev = m_scr[hd, rows, :]
            m_new = jnp.maximum(m_prev, jnp.max(mx, axis=-1, keepdims=True))
            alpha = jnp.exp2(m_prev - m_new)
            psum = jnp.zeros((ATTN_SUB, LANES), F32)
            for c in range(n_lt):
                p = jnp.exp2(tiles[c] - m_new)
                psum = psum + p
                p_scr[hd, rows, c * LANES:(c + 1) * LANES] = p.astype(BF16)
            m_scr[hd, rows, :] = m_new
            a_scr[hd, rows, :] = alpha
            l_scr[hd, rows, :] = alpha * l_scr[hd, rows, :] + jnp.sum(psum, axis=-1, keepdims=True)

    def values(hd, k0):
        pv = _dot(p_scr[hd], v_ref[0, hd, pl.ds(k0, tq), :])
        acc_scr[hd] = a_scr[hd] * acc_scr[hd] + pv

    def block(k0, masked):
        for hd in range(nh):
            scores(hd, k0)
        for hd in range(nh):
            softmax(hd, masked)
        for hd in range(nh):
            values(hd, k0)

    def body(j, carry):
        block(pl.multiple_of(j * tq, tq), False)
        return carry

    lax.fori_loop(0, qi, body, 0)
    block(pl.multiple_of(qi * tq, tq), True)
    for hd in range(nh):
        o_ref[0, :, hd * C_V:(hd + 1) * C_V] = (acc_scr[hd] / l_scr[hd]).astype(o_ref.dtype)


def _attn(q, k, v):
    bsz, _, seq, _ = q.shape
    tq = _row_tile(seq, ATTN_ROWS)
    nh = ATTN_HEADS_PER_STEP
    return pl.pallas_call(
        _attn_kernel,
        grid=(bsz, C_HEADS // nh, seq // tq),
        in_specs=[pl.BlockSpec((1, nh, tq, C_QK), lambda b, h, i: (b, h, i, 0)),
                  pl.BlockSpec((1, nh, seq, C_QK), lambda b, h, i: (b, h, 0, 0)),
                  pl.BlockSpec((1, nh, seq, C_V), lambda b, h, i: (b, h, 0, 0))],
        out_specs=pl.BlockSpec((1, tq, nh * C_V), lambda b, h, i: (b, i, h)),
        out_shape=jax.ShapeDtypeStruct((bsz, seq, C_HEADS * C_V), BF16),
        scratch_shapes=[pltpu.VMEM((nh, tq, tq), F32),
                        pltpu.VMEM((nh, tq, tq), BF16),
                        pltpu.VMEM((nh, tq, LANES), F32),
                        pltpu.VMEM((nh, tq, LANES), F32),
                        pltpu.VMEM((nh, tq, LANES), F32),
                        pltpu.VMEM((nh, tq, C_V), F32)],
        compiler_params=_params("parallel", "parallel", "arbitrary"),
        name="attn",
    )(q, k, v)


def _proj_ln_router_kernel(n_in, *refs):
    a_refs = refs[0:n_in]
    w_refs = refs[n_in:2 * n_in]
    x_ref, g_ref, b_ref, wr_ref, br_ref = refs[2 * n_in:2 * n_in + 5]
    x1_ref, x1b_ref, meta_ref, gate_ref, cnt_ref = refs[2 * n_in + 5:2 * n_in + 10]
    base_scr = refs[2 * n_in + 10]
    tm = x_ref.shape[0]
    i = pl.program_id(0)

    @pl.when(i == 0)
    def _():
        base_scr[...] = jnp.zeros(base_scr.shape, F32)

    mix = _dot(a_refs[0][...], w_refs[0][...])
    for j in range(1, n_in):
        mix = mix + _dot(a_refs[j][...], w_refs[j][...])
    x1 = _layer_norm_rows(DEEPNORM_ALPHA * x_ref[...] + mix, g_ref[...], b_ref[...])
    x1_ref[...] = x1
    x1b = x1.astype(BF16)
    x1b_ref[...] = x1b

    work = _dot_nt(wr_ref[...], x1b) + br_ref[:, 0:1]
    eidx = lax.broadcasted_iota(jnp.int32, (N_EXPERTS, tm), 0).astype(F32)
    cnt = jnp.zeros((N_EXPERTS, tm), F32)
    tops, idxs, hots = [], [], []
    for _ in range(TOP_K):
        mx = jnp.max(work, axis=0, keepdims=True)
        idx = jnp.min(jnp.where(work == mx, eidx, float(N_EXPERTS)), axis=0, keepdims=True)
        hot = eidx == idx
        work = jnp.where(hot, NEG_BIG, work)
        cnt = cnt + jnp.where(hot, 1.0, 0.0)
        tops.append(mx)
        idxs.append(idx)
        hots.append(hot)
    exps = [jnp.exp(t - tops[0]) for t in tops]
    denom = exps[0] + exps[1] + exps[2] + exps[3]

    ri = lax.broadcasted_iota(jnp.int32, (tm, tm), 0)
    ci = lax.broadcasted_iota(jnp.int32, (tm, tm), 1)
    before = jnp.where(ri < ci, 1.0, 0.0).astype(BF16)
    base = base_scr[:, 0:1]
    rank = _dot(cnt.astype(BF16), before) + base
    row = lax.broadcasted_iota(jnp.int32, (2 * TOP_K, tm), 0)
    meta = jnp.zeros((2 * TOP_K, tm), jnp.int32)
    gates = jnp.zeros((2 * TOP_K, tm), F32)
    for kk in range(TOP_K):
        pos = jnp.sum(jnp.where(hots[kk], rank, 0.0), axis=0, keepdims=True).astype(jnp.int32)
        meta = jnp.where(row == kk, idxs[kk].astype(jnp.int32), meta)
        meta = jnp.where(row == TOP_K + kk, pos, meta)
        gates = jnp.where(row == kk, exps[kk] / denom, gates)
    meta_ref[...] = meta
    gate_ref[...] = gates
    total = base + jnp.sum(cnt, axis=1, keepdims=True)
    base_scr[...] = jnp.broadcast_to(total, base_scr.shape)
    cnt_ref[...] = jnp.broadcast_to(total, cnt_ref.shape)


def _proj_ln_router(a_list, w_list, x, ln_g, ln_b, w_r, b_r, tm):
    t = x.shape[0]
    n_in = len(a_list)
    in_specs = [pl.BlockSpec((tm, a.shape[1]), lambda i: (i, 0)) for a in a_list]
    in_specs += [pl.BlockSpec(w.shape, lambda i: (0, 0)) for w in w_list]
    in_specs += [pl.BlockSpec((tm, D_MODEL), lambda i: (i, 0)),
                 pl.BlockSpec((1, D_MODEL), lambda i: (0, 0)),
                 pl.BlockSpec((1, D_MODEL), lambda i: (0, 0)),
                 pl.BlockSpec((N_EXPERTS, D_MODEL), lambda i: (0, 0)),
                 pl.BlockSpec((N_EXPERTS, LANES), lambda i: (0, 0))]
    return pl.pallas_call(
        functools.partial(_proj_ln_router_kernel, n_in),
        grid=(t // tm,),
        in_specs=in_specs,
        out_specs=[pl.BlockSpec((tm, D_MODEL), lambda i: (i, 0)),
                   pl.BlockSpec((tm, D_MODEL), lambda i: (i, 0)),
                   pl.BlockSpec((2 * TOP_K, tm), lambda i: (0, i)),
                   pl.BlockSpec((2 * TOP_K, tm), lambda i: (0, i)),
                   pl.BlockSpec((N_EXPERTS, LANES), lambda i: (0, 0))],
        out_shape=[jax.ShapeDtypeStruct((t, D_MODEL), F32),
                   jax.ShapeDtypeStruct((t, D_MODEL), BF16),
                   jax.ShapeDtypeStruct((2 * TOP_K, t), jnp.int32),
                   jax.ShapeDtypeStruct((2 * TOP_K, t), F32),
                   jax.ShapeDtypeStruct((N_EXPERTS, LANES), F32)],
        scratch_shapes=[pltpu.VMEM((N_EXPERTS, LANES), F32)],
        compiler_params=_params("arbitrary"),
        name="proj_ln_router",
    )(*a_list, *w_list, x, ln_g, ln_b, w_r, b_r)


def _moe_kernel(blk_ref, exp_ref, lo_ref, hi_ref, x_ref, wgu_ref, wd_ref, bg_ref, bu_ref, bd_ref, o_ref,
                wg_scr, wu_scr, wd_scr, t_scr):
    i = pl.program_id(0)
    tm = x_ref.shape[0]
    blk = blk_ref[i]
    lo = lo_ref[i]
    hi = hi_ref[i]
    prev = jnp.maximum(i - 1, 0)
    first = jnp.logical_or(i == 0, blk != blk_ref[prev])
    new_expert = jnp.logical_or(i == 0, exp_ref[i] != exp_ref[prev])

    @pl.when(new_expert)
    def _():
        pc = t_scr.shape[1]
        half = pc // 2
        for c in range(2 * D_FF // pc):
            for kc in range(D_MODEL // LANES):
                k0 = kc * LANES
                t_scr[kc] = wgu_ref[0, 0, k0:k0 + LANES, c * pc:(c + 1) * pc].T
                wg_scr[c * half:(c + 1) * half, k0:k0 + LANES] = t_scr[kc, pl.ds(0, half, stride=2), :].astype(BF16)
                wu_scr[c * half:(c + 1) * half, k0:k0 + LANES] = t_scr[kc, pl.ds(1, half, stride=2), :].astype(BF16)
        wd_scr[...] = wd_ref[0, 0].astype(BF16)

    @pl.when(first)
    def _():
        o_ref[...] = jnp.zeros(o_ref.shape, o_ref.dtype)

    @pl.when(hi > lo)
    def _():
        x = x_ref[...]
        glu = jnp.minimum(_dot_nt(x, wg_scr[...]) + bg_ref[0, 0], SWIGLU_LIMIT)
        lin = jnp.clip(_dot_nt(x, wu_scr[...]) + bu_ref[0, 0], -SWIGLU_LIMIT, SWIGLU_LIMIT)
        act = glu * _sigmoid(SWIGLU_ALPHA * glu) * (lin + 1.0)
        y = _dot(act.astype(BF16), wd_scr[...]) + bd_ref[0, 0]
        row = blk * tm + lax.broadcasted_iota(jnp.int32, (tm, 1), 0)
        keep = jnp.logical_and(row >= lo, row < hi)
        o_ref[...] = jnp.where(keep, y.astype(o_ref.dtype), o_ref[...])


def _moe_sorted(layer, xs, counts, w_gate_up, b_gate, b_up, w_down, b_down):
    n_rows = xs.shape[0]
    tm = _row_tile(n_rows, MOE_ROWS)
    items = _moe_items(counts, n_rows, tm)
    n_items = items[0].shape[0]

    def by_expert(i, blk, ex, lo, hi):
        return (layer, ex[i], 0, 0)

    def by_block(i, blk, ex, lo, hi):
        return (blk[i], 0)

    grid_spec = pltpu.PrefetchScalarGridSpec(
        num_scalar_prefetch=4,
        grid=(n_items,),
        in_specs=[pl.BlockSpec((tm, D_MODEL), by_block),
                  pl.BlockSpec((1, 1, D_MODEL, 2 * D_FF), by_expert),
                  pl.BlockSpec((1, 1, D_FF, D_MODEL), by_expert),
                  pl.BlockSpec((1, 1, 1, D_FF), by_expert),
                  pl.BlockSpec((1, 1, 1, D_FF), by_expert),
                  pl.BlockSpec((1, 1, 1, D_MODEL), by_expert)],
        out_specs=pl.BlockSpec((tm, D_MODEL), by_block),
        scratch_shapes=[pltpu.VMEM((D_FF, D_MODEL), BF16),
                        pltpu.VMEM((D_FF, D_MODEL), BF16),
                        pltpu.VMEM((D_FF, D_MODEL), BF16),
                        pltpu.VMEM((D_MODEL // LANES, MOE_PREP_COLS, LANES), F32)],
    )
    return pl.pallas_call(
        _moe_kernel,
        grid_spec=grid_spec,
        out_shape=jax.ShapeDtypeStruct((n_rows, D_MODEL), BF16),
        compiler_params=_params("arbitrary", vmem=MOE_VMEM_LIMIT),
        name="moe_ffn",
    )(*items, xs, w_gate_up, w_down, b_gate, b_up, b_down)


def _moe_items(counts, n_rows, tm):
    n_blocks = n_rows // tm
    start = jnp.cumsum(counts) - counts
    cuts = jnp.sort(jnp.concatenate([jnp.arange(n_blocks, dtype=jnp.int32) * tm, start.astype(jnp.int32)]))
    lo = cuts
    hi = jnp.concatenate([cuts[1:], jnp.array([n_rows], jnp.int32)])
    blk = jnp.minimum(lo // tm, n_blocks - 1)
    ex = jnp.clip(jnp.sum((start[None, :] <= lo[:, None]).astype(jnp.int32), axis=1) - 1, 0, N_EXPERTS - 1)
    return blk.astype(jnp.int32), ex.astype(jnp.int32), lo.astype(jnp.int32), hi.astype(jnp.int32)


def _combine_ln_kernel(x_ref, y_ref, gate_ref, g_ref, b_ref, o_ref):
    gates = gate_ref[...]
    ffn = y_ref[0].astype(F32) * gates[:, 0:1]
    for kk in range(1, TOP_K):
        ffn = ffn + y_ref[kk].astype(F32) * gates[:, kk:kk + 1]
    o_ref[...] = _layer_norm_rows(DEEPNORM_ALPHA * x_ref[...] + ffn, g_ref[...], b_ref[...])


def _combine_ln(x1, y4, gates, ln_g, ln_b, tm):
    t = x1.shape[0]
    return pl.pallas_call(
        _combine_ln_kernel,
        grid=(t // tm,),
        in_specs=[pl.BlockSpec((tm, D_MODEL), lambda i: (i, 0)),
                  pl.BlockSpec((TOP_K, tm, D_MODEL), lambda i: (0, i, 0)),
                  pl.BlockSpec((tm, 2 * TOP_K), lambda i: (i, 0)),
                  pl.BlockSpec((1, D_MODEL), lambda i: (0, 0)),
                  pl.BlockSpec((1, D_MODEL), lambda i: (0, 0))],
        out_specs=pl.BlockSpec((tm, D_MODEL), lambda i: (i, 0)),
        out_shape=jax.ShapeDtypeStruct((t, D_MODEL), F32),
        compiler_params=_params("parallel"),
        name="combine_ln",
    )(x1, y4, gates, ln_g, ln_b)


def _moe_layer(layer, x1, x1b, meta_t, gates_t, counts, ln_g, ln_b, w_gate_up, b_gate, b_up, w_down, b_down):
    t = x1.shape[0]
    assert N_EXPERTS * t < 2 ** 31
    e = meta_t[0:TOP_K]
    pos = meta_t[TOP_K:2 * TOP_K]
    cnt = counts[:, 0].astype(jnp.int32)
    start = jnp.cumsum(cnt) - cnt
    experts = jnp.arange(N_EXPERTS, dtype=jnp.int32)
    dest = pos + jnp.sum(jnp.where(e[:, :, None] == experts, start, 0), axis=-1)
    keys = e * t + jnp.arange(t, dtype=jnp.int32)[None, :]
    tok_sorted = jnp.remainder(jnp.sort(keys.reshape(-1)), t)
    xs = x1b[tok_sorted]
    ys = _moe_sorted(layer, xs, cnt, w_gate_up, b_gate, b_up, w_down, b_down)
    y4 = ys[dest]
    return _combine_ln(x1, y4, gates_t.T, ln_g, ln_b, _row_tile(t, ROWS))


def kernel(x, even_w_in, even_sgu_ln_g, even_sgu_ln_b, even_sgu_w, even_sgu_b, even_conv_w, even_a_log, even_dt_bias, even_gdn_norm, even_w_out, odd_w_in, odd_q_norm, odd_w_uq, odd_kv_norm, odd_w_ukv, odd_w_out, ln_mix_g, ln_mix_b, ln_ffn_g, ln_ffn_b, moe_w_router, moe_b_router, moe_w_gate_up, moe_b_gate_up, moe_w_down, moe_b_down):
    n_streams = N_STREAMS if x.shape[0] % N_STREAMS == 0 else 1
    bsz, seq, d = x.shape[0] // n_streams, x.shape[1], x.shape[2]
    t = bsz * seq
    streams = [x[g * bsz:(g + 1) * bsz].reshape(t, d) for g in range(n_streams)]
    tm = _row_tile(t, ROWS)
    b_gate = moe_b_gate_up[:, :, None, 0::2]
    b_up = moe_b_gate_up[:, :, None, 1::2]
    b_down = moe_b_down[:, :, None, :]

    inv_freq = ROPE_THETA ** (-jnp.arange(0, C_ROPE, 2, dtype=F32) / C_ROPE)
    ang = jnp.arange(seq, dtype=F32)[:, None] * inv_freq[None, :]
    cs = jnp.concatenate([jnp.cos(ang), jnp.cos(ang), jnp.sin(ang), jnp.sin(ang)], axis=1)

    pos_chunk = np.arange(A_BLOCK) // CHUNK
    allowed = jnp.asarray(pos_chunk[None, :] <= pos_chunk[:, None])

    def rot_cols(w):
        half = C_ROPE // 2
        return jnp.concatenate([-w[..., half:], w[..., :half]], axis=-1)

    for layer in range(DEPTH):
        i = layer // 2
        if layer % 2 == 0:
            w_in = even_w_in[i]
            w_a = w_in[:, 0:2 * A_WIDTH].astype(BF16)
            w_b = jnp.pad(w_in[:, 2 * A_WIDTH:], ((0, 0), (0, LANES - 2 * B_HEADS))).astype(BF16)
            sw = jnp.where(allowed[None], even_sgu_w[i], 0).astype(BF16)
            sb = jnp.broadcast_to(even_sgu_b[i][:, :, None], (A_GROUPS, A_BLOCK, A_GROUP_DIM)).astype(F32)
            hp = jnp.pad(jnp.stack([even_a_log[i], even_dt_bias[i]]), ((0, 0), (0, LANES - B_HEADS)))
            hpt = jnp.pad(jnp.stack([even_a_log[i], even_dt_bias[i]], axis=1), ((0, SUBLANES - B_HEADS), (0, 0)))
            w_out = even_w_out[i].astype(BF16)
            w_list = [w_out[0:A_WIDTH], w_out[A_WIDTH:]]
        else:
            w_in = odd_w_in[i]
            w_pe = w_in[:, Q_LORA + KV_LORA:]
            w_in2 = jnp.concatenate([w_in, rot_cols(w_pe)], axis=1).astype(BF16)
            w_uq = odd_w_uq[i].reshape(Q_LORA, C_HEADS, C_QK)
            w_uq2 = jnp.concatenate([w_uq, rot_cols(w_uq[..., C_NOPE:])], axis=-1)
            w_uq2 = w_uq2.reshape(Q_LORA, C_HEADS * (C_NOPE + 2 * C_ROPE)).astype(BF16)
            w_ukv = odd_w_ukv[i].astype(BF16)
            w_list = [odd_w_out[i].astype(BF16)]
        w_r = moe_w_router[layer].T.astype(BF16)
        b_r = jnp.broadcast_to(moe_b_router[layer][:, None], (N_EXPERTS, LANES))

        for g in range(n_streams):
            xt = streams[g]
            if layer % 2 == 0:
                y_a = _sgu(xt, w_a, even_sgu_ln_g[i][None], even_sgu_ln_b[i][None], sw, sb, tm)
                h_b, ab = _mm_split(xt, w_b, 4 * B_WIDTH, BF16, F32, tm)
                abt = ab.reshape(bsz, seq, LANES)[:, :, 0:SUBLANES].transpose(0, 2, 1)
                y_b = _gdn(h_b, ab, abt, even_conv_w[i], hp, hpt, even_gdn_norm[i][None], bsz, seq)
                a_list = [y_a, y_b]
            else:
                h = _mm(xt, w_in2, BF16, tm)
                q, k, v = _mla_up(h, odd_q_norm[i][None], odd_kv_norm[i][None], w_uq2, w_ukv, cs,
                                  bsz, seq, _row_tile(seq, ROWS))
                a_list = [_attn(q, k, v).reshape(t, C_HEADS * C_V)]
            x1, x1b, meta, gates, counts = _proj_ln_router(a_list, w_list, xt, ln_mix_g[layer][None],
                                                           ln_mix_b[layer][None], w_r, b_r, tm)
            streams[g] = _moe_layer(layer, x1, x1b, meta, gates, counts, ln_ffn_g[layer][None],
                                    ln_ffn_b[layer][None], moe_w_gate_up, b_gate, b_up, moe_w_down, b_down)
    return jnp.concatenate([xs.reshape(bsz, seq, d) for xs in streams], axis=0)
```

```python
import functools

import jax
import jax.numpy as jnp
import numpy as np
from jax import lax
from jax.experimental import pallas as pl
from jax.experimental.pallas import tpu as pltpu

F32 = jnp.float32
BF16 = jnp.bfloat16

D_MODEL = 1024
DEPTH = 4
CHUNK = 64
CHUNK_SHIFT = CHUNK.bit_length() - 1
DEEPNORM_ALPHA = (2 * DEPTH) ** 0.25
LN_EPS = 1e-5
RMS_EPS = 1e-6

A_BLOCK = 128
A_GROUPS = 4
A_WIDTH = D_MODEL // 2
A_GROUP_DIM = A_WIDTH // A_GROUPS

B_HEADS = 4
B_HEAD_DIM = D_MODEL // 8
B_WIDTH = B_HEADS * B_HEAD_DIM
CONV_K = 4

C_HEADS = 8
C_NOPE = 128
C_ROPE = 64
C_V = D_MODEL // C_HEADS
C_QK = C_NOPE + C_ROPE
Q_LORA = 3 * D_MODEL // 8
KV_LORA = D_MODEL // 4
ROPE_THETA = 10000.0

N_EXPERTS = 32
TOP_K = 4
D_FF = D_MODEL
SWIGLU_LIMIT = 7.0
SWIGLU_ALPHA = 1.702

LANES = 128
SUBLANES = 8
VMEM_LIMIT = 48 * 1024 * 1024
MOE_VMEM_LIMIT = 58 * 1024 * 1024
NEG_BIG = -1e30
LOG2_E = 1.4426950408889634

ROWS = 512
GDN_ROWS = 256
INV_BASE_SHIFT = 1
ATTN_ROWS = 512
ATTN_HEADS_PER_STEP = 4
ATTN_SUB = 128
MOE_ROWS = 512
MOE_PREP_COLS = 512


def _row_tile(t, pref):
    tm = pref
    while t % tm:
        tm //= 2
    return tm


def _params(*sem, vmem=VMEM_LIMIT):
    return pltpu.CompilerParams(dimension_semantics=sem, vmem_limit_bytes=vmem)


def _dot(a, b):
    return jnp.dot(a, b, preferred_element_type=F32)


def _dot_nt(a, b):
    return lax.dot_general(a, b, (((1,), (1,)), ((), ())), preferred_element_type=F32)


def _dot_tn(a, b):
    return lax.dot_general(a, b, (((0,), (0,)), ((), ())), preferred_element_type=F32)


def _sigmoid(x):
    return 1.0 / (1.0 + jnp.exp(-x))


def _softplus(x):
    return jnp.maximum(x, 0.0) + jnp.log1p(jnp.exp(-jnp.abs(x)))


def _layer_norm_rows(y, g, b):
    mu = jnp.mean(y, axis=-1, keepdims=True)
    d = y - mu
    var = jnp.mean(d * d, axis=-1, keepdims=True)
    return d * lax.rsqrt(var + LN_EPS) * g + b


def _mm_kernel(x_ref, w_ref, o_ref):
    o_ref[...] = _dot(x_ref[...].astype(BF16), w_ref[...]).astype(o_ref.dtype)


def _mm(x, w, out_dtype, tm):
    t, k = x.shape
    n = w.shape[1]
    return pl.pallas_call(
        _mm_kernel,
        grid=(t // tm,),
        in_specs=[pl.BlockSpec((tm, k), lambda i: (i, 0)),
                  pl.BlockSpec((k, n), lambda i: (0, 0))],
        out_specs=pl.BlockSpec((tm, n), lambda i: (i, 0)),
        out_shape=jax.ShapeDtypeStruct((t, n), out_dtype),
        compiler_params=_params("parallel"),
        name="mm",
    )(x, w)


def _mm_split_kernel(x_ref, w_ref, o0_ref, o1_ref):
    n0 = o0_ref.shape[1]
    h = _dot(x_ref[...].astype(BF16), w_ref[...])
    o0_ref[...] = h[:, 0:n0].astype(o0_ref.dtype)
    o1_ref[...] = h[:, n0:].astype(o1_ref.dtype)


def _mm_split(x, w, n0, dtype0, dtype1, tm):
    t, k = x.shape
    n = w.shape[1]
    return pl.pallas_call(
        _mm_split_kernel,
        grid=(t // tm,),
        in_specs=[pl.BlockSpec((tm, k), lambda i: (i, 0)),
                  pl.BlockSpec((k, n), lambda i: (0, 0))],
        out_specs=[pl.BlockSpec((tm, n0), lambda i: (i, 0)),
                   pl.BlockSpec((tm, n - n0), lambda i: (i, 0))],
        out_shape=[jax.ShapeDtypeStruct((t, n0), dtype0), jax.ShapeDtypeStruct((t, n - n0), dtype1)],
        compiler_params=_params("parallel"),
        name="mm_split",
    )(x, w)


def _gelu(x):
    return 0.5 * x * (1.0 + lax.erf(x * (2.0 ** -0.5)))


def _sgu_kernel(x_ref, w_ref, g_ref, b_ref, sw_ref, sb_ref, o_ref):
    tm = x_ref.shape[0]
    h = _dot(x_ref[...].astype(BF16), w_ref[...])
    for grp in range(A_GROUPS):
        c0 = grp * A_GROUP_DIM
        u = _gelu(h[:, c0:c0 + A_GROUP_DIM])
        v = _gelu(h[:, A_WIDTH + c0:A_WIDTH + c0 + A_GROUP_DIM])
        vn = _layer_norm_rows(v, g_ref[:, c0:c0 + A_GROUP_DIM], b_ref[:, c0:c0 + A_GROUP_DIM]).astype(BF16)
        for blk in range(tm // A_BLOCK):
            r0 = blk * A_BLOCK
            mixed = _dot(sw_ref[grp], vn[r0:r0 + A_BLOCK]) + sb_ref[grp]
            o_ref[r0:r0 + A_BLOCK, c0:c0 + A_GROUP_DIM] = (u[r0:r0 + A_BLOCK] * mixed).astype(o_ref.dtype)


def _sgu(x, w_a, ln_g, ln_b, sw, sb, tm):
    t = x.shape[0]
    return pl.pallas_call(
        _sgu_kernel,
        grid=(t // tm,),
        in_specs=[pl.BlockSpec((tm, D_MODEL), lambda i: (i, 0)),
                  pl.BlockSpec((D_MODEL, 2 * A_WIDTH), lambda i: (0, 0)),
                  pl.BlockSpec((1, A_WIDTH), lambda i: (0, 0)),
                  pl.BlockSpec((1, A_WIDTH), lambda i: (0, 0)),
                  pl.BlockSpec((A_GROUPS, A_BLOCK, A_BLOCK), lambda i: (0, 0, 0)),
                  pl.BlockSpec((A_GROUPS, A_BLOCK, A_GROUP_DIM), lambda i: (0, 0, 0))],
        out_specs=pl.BlockSpec((tm, A_WIDTH), lambda i: (i, 0)),
        out_shape=jax.ShapeDtypeStruct((t, A_WIDTH), BF16),
        compiler_params=_params("parallel"),
        name="sgu",
    )(x, w_a, ln_g, ln_b, sw, sb)


def _gdn_kernel(h_ref, ab_ref, abt_ref, cw_ref, hp_ref, hpt_ref, nw_ref, o_ref, conv_scr, state_scr):
    tr = h_ref.shape[0]
    n_chunks = tr // CHUNK
    s_idx = pl.program_id(1)

    @pl.when(s_idx == 0)
    def _():
        conv_scr[0:SUBLANES, :] = jnp.zeros((SUBLANES, 3 * B_WIDTH), F32)
        state_scr[...] = jnp.zeros(state_scr.shape, F32)

    conv_scr[SUBLANES:SUBLANES + tr, :] = h_ref[:, 0:3 * B_WIDTH].astype(F32)
    acc = conv_scr[SUBLANES:SUBLANES + tr, :] * cw_ref[CONV_K - 1:CONV_K, :]
    for d in range(1, CONV_K):
        acc = acc + conv_scr[SUBLANES - d:SUBLANES - d + tr, :] * cw_ref[CONV_K - 1 - d:CONV_K - d, :]
    conv_scr[0:SUBLANES, :] = conv_scr[tr:tr + SUBLANES, :]
    qkv = acc * _sigmoid(acc)

    ab = ab_ref[...]
    sp_c = _softplus(ab + hp_ref[1:2, :])
    g_col = -jnp.exp(hp_ref[0:1, :]) * sp_c
    beta_col = _sigmoid(ab)
    abt = abt_ref[0]
    sp_r = _softplus(abt + hpt_ref[:, 1:2])
    g_row = -jnp.exp(hpt_ref[:, 0:1]) * sp_r

    ri = lax.broadcasted_iota(jnp.int32, (tr, tr), 0)
    ci = lax.broadcasted_iota(jnp.int32, (tr, tr), 1)
    same = jnp.right_shift(ri, CHUNK_SHIFT) == jnp.right_shift(ci, CHUNK_SHIFT)
    lower = jnp.where(same & (ri >= ci), 1.0, 0.0).astype(F32)
    upper = jnp.where(same & (ri <= ci), 1.0, 0.0).astype(F32)
    gc_col = jnp.dot(lower, g_col, preferred_element_type=F32, precision=lax.Precision.HIGHEST)
    gc_row = jnp.dot(g_row, upper, preferred_element_type=F32, precision=lax.Precision.HIGHEST)

    incl = same & (ri >= ci)
    strict = same & (ri > ci)
    eye = jnp.where(ri == ci, 1.0, 0.0).astype(F32)

    heads = range(B_HEADS)
    qs, ks, vs, betas, gccs, decays, k_bs, kbs, ps, ts = [], [], [], [], [], [], [], [], [], []
    for hd in heads:
        l0 = hd * B_HEAD_DIM
        q = qkv[:, l0:l0 + B_HEAD_DIM]
        k = qkv[:, B_WIDTH + l0:B_WIDTH + l0 + B_HEAD_DIM]
        qs.append(q * lax.rsqrt(jnp.sum(q * q, axis=-1, keepdims=True) + RMS_EPS) * (B_HEAD_DIM ** -0.5))
        ks.append(k * lax.rsqrt(jnp.sum(k * k, axis=-1, keepdims=True) + RMS_EPS))
        vs.append(qkv[:, 2 * B_WIDTH + l0:2 * B_WIDTH + l0 + B_HEAD_DIM])
        betas.append(beta_col[:, B_HEADS + hd:B_HEADS + hd + 1])
        gccs.append(gc_col[:, hd:hd + 1])
        gcr = gc_row[hd:hd + 1, :]
        decays.append(jnp.where(incl, jnp.exp(jnp.where(incl, gccs[hd] - gcr, 0.0)), 0.0))
        k_bs.append(ks[hd].astype(BF16))
        kbs.append(ks[hd] * betas[hd])
    lvl = [jnp.right_shift(ri, s) == jnp.right_shift(ci, s) for s in range(INV_BASE_SHIFT, CHUNK_SHIFT + 1)]
    a_mats = []
    for hd in heads:
        a_mat = jnp.where(strict, _dot_nt(kbs[hd].astype(BF16), k_bs[hd]) * decays[hd], 0.0)
        a_mats.append(a_mat)
        d_mat = jnp.where(lvl[0], a_mat, 0.0)
        ps.append(-d_mat)
        ts.append(eye - d_mat)
    for _ in range(INV_BASE_SHIFT - 1):
        for hd in heads:
            pb = ps[hd].astype(BF16)
            ps[hd] = _dot(pb, pb)
        for hd in heads:
            ts[hd] = ts[hd] + _dot(ts[hd].astype(BF16), ps[hd].astype(BF16))
    for j in range(1, len(lvl)):
        for hd in heads:
            join = jnp.where(lvl[j] & jnp.logical_not(lvl[j - 1]), a_mats[hd], 0.0).astype(BF16)
            t_b = ts[hd].astype(BF16)
            ps[hd] = _dot(_dot(t_b, join).astype(BF16), t_b)
        for hd in heads:
            ts[hd] = ts[hd] - ps[hd]
    us, ws, intras, q_decs = [], [], [], []
    for hd in heads:
        t_b = ts[hd].astype(BF16)
        e_gc = jnp.exp(gccs[hd])
        us.append(_dot(t_b, (vs[hd] * betas[hd]).astype(BF16)))
        ws.append(_dot(t_b, (kbs[hd] * e_gc).astype(BF16)).astype(BF16))
        intras.append(jnp.where(incl, _dot_nt(qs[hd].astype(BF16), k_bs[hd]) * decays[hd], 0.0).astype(BF16))
        q_decs.append((qs[hd] * e_gc).astype(BF16))
    for c in range(n_chunks):
        r0 = c * CHUNK
        for hd in heads:
            l0 = hd * B_HEAD_DIM
            g_last = gccs[hd][r0 + CHUNK - 1:r0 + CHUNK, :]
            k_dec = (ks[hd][r0:r0 + CHUNK] * jnp.exp(g_last - gccs[hd][r0:r0 + CHUNK])).astype(BF16)
            state = state_scr[hd]
            state_b = state.astype(BF16)
            v_new = (us[hd][r0:r0 + CHUNK] - _dot(ws[hd][r0:r0 + CHUNK], state_b)).astype(BF16)
            out = _dot(q_decs[hd][r0:r0 + CHUNK], state_b) + _dot(intras[hd][r0:r0 + CHUNK, r0:r0 + CHUNK], v_new)
            state_scr[hd] = state * jnp.exp(g_last) + _dot_tn(k_dec, v_new)
            z = h_ref[r0:r0 + CHUNK, 3 * B_WIDTH + l0:3 * B_WIDTH + l0 + B_HEAD_DIM].astype(F32)
            o_n = out * lax.rsqrt(jnp.mean(out * out, axis=-1, keepdims=True) + RMS_EPS) * nw_ref[...]
            o_ref[r0:r0 + CHUNK, l0:l0 + B_HEAD_DIM] = (o_n * (z * _sigmoid(z))).astype(o_ref.dtype)


def _gdn(h_b, ab, abt, conv_w, hp, hpt, norm_w, bsz, seq):
    tr = _row_tile(seq, GDN_ROWS)
    n_s = seq // tr
    return pl.pallas_call(
        _gdn_kernel,
        grid=(bsz, n_s),
        in_specs=[pl.BlockSpec((tr, 4 * B_WIDTH), lambda b, s: (b * n_s + s, 0)),
                  pl.BlockSpec((tr, LANES), lambda b, s: (b * n_s + s, 0)),
                  pl.BlockSpec((1, SUBLANES, tr), lambda b, s: (b, 0, s)),
                  pl.BlockSpec((CONV_K, 3 * B_WIDTH), lambda b, s: (0, 0)),
                  pl.BlockSpec((2, LANES), lambda b, s: (0, 0)),
                  pl.BlockSpec((SUBLANES, 2), lambda b, s: (0, 0)),
                  pl.BlockSpec((1, B_HEAD_DIM), lambda b, s: (0, 0))],
        out_specs=pl.BlockSpec((tr, B_WIDTH), lambda b, s: (b * n_s + s, 0)),
        out_shape=jax.ShapeDtypeStruct((bsz * seq, B_WIDTH), BF16),
        scratch_shapes=[pltpu.VMEM((tr + 2 * SUBLANES, 3 * B_WIDTH), F32),
                        pltpu.VMEM((B_HEADS, B_HEAD_DIM, B_HEAD_DIM), F32)],
        compiler_params=_params("parallel", "arbitrary"),
        name="gdn",
    )(h_b, ab, abt, conv_w, hp, hpt, norm_w)


def _mla_up_kernel(h_ref, qn_ref, kvn_ref, wq_ref, wkv_ref, cs_ref, q_ref, k_ref, v_ref):
    h = h_ref[...].astype(F32)
    cs = cs_ref[...]

    def rms(x, g):
        return x * lax.rsqrt(jnp.mean(x * x, axis=-1, keepdims=True) + RMS_EPS) * g

    def rope(t):
        r = t * cs
        return r + pltpu.roll(r, C_ROPE, 1)

    cq = rms(h[:, 0:Q_LORA], qn_ref[...]).astype(BF16)
    ckv = rms(h[:, Q_LORA:Q_LORA + KV_LORA], kvn_ref[...]).astype(BF16)
    k_pe = rope(h[:, Q_LORA + KV_LORA:Q_LORA + KV_LORA + 2 * C_ROPE])[:, 0:C_ROPE].astype(BF16)
    qf = _dot(cq, wq_ref[...]) * (C_QK ** -0.5 * LOG2_E)
    kv = _dot(ckv, wkv_ref[...])
    for hd in range(C_HEADS):
        q0 = hd * (C_NOPE + 2 * C_ROPE)
        q_ref[0, hd, :, 0:C_NOPE] = qf[:, q0:q0 + C_NOPE].astype(BF16)
        q_ref[0, hd, :, C_NOPE:C_QK] = rope(qf[:, q0 + C_NOPE:q0 + C_NOPE + 2 * C_ROPE])[:, 0:C_ROPE].astype(BF16)
        k0 = hd * (C_NOPE + C_V)
        k_ref[0, hd, :, 0:C_NOPE] = kv[:, k0:k0 + C_NOPE].astype(BF16)
        k_ref[0, hd, :, C_NOPE:C_QK] = k_pe
        v_ref[0, hd, :, :] = kv[:, k0 + C_NOPE:k0 + C_NOPE + C_V].astype(BF16)


def _mla_up(h, q_norm, kv_norm, w_uq, w_ukv, cs, bsz, seq, tm):
    n_s = seq // tm
    hw = h.shape[1]
    qk_shape = jax.ShapeDtypeStruct((bsz, C_HEADS, seq, C_QK), BF16)
    return pl.pallas_call(
        _mla_up_kernel,
        grid=(bsz, n_s),
        in_specs=[pl.BlockSpec((tm, hw), lambda b, s: (b * n_s + s, 0)),
                  pl.BlockSpec((1, Q_LORA), lambda b, s: (0, 0)),
                  pl.BlockSpec((1, KV_LORA), lambda b, s: (0, 0)),
                  pl.BlockSpec(w_uq.shape, lambda b, s: (0, 0)),
                  pl.BlockSpec(w_ukv.shape, lambda b, s: (0, 0)),
                  pl.BlockSpec((tm, LANES), lambda b, s: (s, 0))],
        out_specs=[pl.BlockSpec((1, C_HEADS, tm, C_QK), lambda b, s: (b, 0, s, 0)),
                   pl.BlockSpec((1, C_HEADS, tm, C_QK), lambda b, s: (b, 0, s, 0)),
                   pl.BlockSpec((1, C_HEADS, tm, C_V), lambda b, s: (b, 0, s, 0))],
        out_shape=[qk_shape, qk_shape, jax.ShapeDtypeStruct((bsz, C_HEADS, seq, C_V), BF16)],
        compiler_params=_params("parallel", "parallel"),
        name="mla_up",
    )(h, q_norm, kv_norm, w_uq, w_ukv, cs)


def _attn_kernel(q_ref, k_ref, v_ref, o_ref, s_scr, p_scr, m_scr, l_scr, a_scr, acc_scr):
    nh, tq = q_ref.shape[1], q_ref.shape[2]
    n_lt = tq // LANES
    qi = pl.program_id(2)

    m_scr[...] = jnp.full(m_scr.shape, NEG_BIG, F32)
    l_scr[...] = jnp.zeros(l_scr.shape, F32)
    acc_scr[...] = jnp.zeros(acc_scr.shape, F32)

    def scores(hd, k0):
        s_scr[hd] = _dot_nt(q_ref[0, hd], k_ref[0, hd, pl.ds(k0, tq), :])

    def softmax(hd, masked):
        for r0 in range(0, tq, ATTN_SUB):
            rows = slice(r0, r0 + ATTN_SUB)
            tiles = []
            for c in range(n_lt):
                s = s_scr[hd, rows, c * LANES:(c + 1) * LANES]
                if masked:
                    ri = jnp.right_shift(r0 + lax.broadcasted_iota(jnp.int32, (ATTN_SUB, LANES), 0), CHUNK_SHIFT)
                    ci = jnp.right_shift(c * LANES + lax.broadcasted_iota(jnp.int32, (ATTN_SUB, LANES), 1),
                                         CHUNK_SHIFT)
                    s = jnp.where(ci <= ri, s, NEG_BIG)
                tiles.append(s)
            mx = tiles[0]
            for c in range(1, n_lt):
                mx = jnp.maximum(mx, tiles[c])
            m_prev = m_scr[hd, rows, :]
            m_new = jnp.maximum(m_prev, jnp.max(mx, axis=-1, keepdims=True))
            alpha = jnp.exp2(m_prev - m_new)
            psum = jnp.zeros((ATTN_SUB, LANES), F32)
            for c in range(n_lt):
                p = jnp.exp2(tiles[c] - m_new)
                psum = psum + p
                p_scr[hd, rows, c * LANES:(c + 1) * LANES] = p.astype(BF16)
            m_scr[hd, rows, :] = m_new
            a_scr[hd, rows, :] = alpha
            l_scr[hd, rows, :] = alpha * l_scr[hd, rows, :] + jnp.sum(psum, axis=-1, keepdims=True)

    def values(hd, k0):
        pv = _dot(p_scr[hd], v_ref[0, hd, pl.ds(k0, tq), :])
        acc_scr[hd] = a_scr[hd] * acc_scr[hd] + pv

    def block(k0, masked):
        for hd in range(nh):
            scores(hd, k0)
        for hd in range(nh):
            softmax(hd, masked)
        for hd in range(nh):
            values(hd, k0)

    def body(j, carry):
        block(pl.multiple_of(j * tq, tq), False)
        return carry

    lax.fori_loop(0, qi, body, 0)
    block(pl.multiple_of(qi * tq, tq), True)
    for hd in range(nh):
        o_ref[0, :, hd * C_V:(hd + 1) * C_V] = (acc_scr[hd] / l_scr[hd]).astype(o_ref.dtype)


def _attn(q, k, v):
    bsz, _, seq, _ = q.shape
    tq = _row_tile(seq, ATTN_ROWS)
    nh = ATTN_HEADS_PER_STEP
    return pl.pallas_call(
        _attn_kernel,
        grid=(bsz, C_HEADS // nh, seq // tq),
        in_specs=[pl.BlockSpec((1, nh, tq, C_QK), lambda b, h, i: (b, h, i, 0)),
                  pl.BlockSpec((1, nh, seq, C_QK), lambda b, h, i: (b, h, 0, 0)),
                  pl.BlockSpec((1, nh, seq, C_V), lambda b, h, i: (b, h, 0, 0))],
        out_specs=pl.BlockSpec((1, tq, nh * C_V), lambda b, h, i: (b, i, h)),
        out_shape=jax.ShapeDtypeStruct((bsz, seq, C_HEADS * C_V), BF16),
        scratch_shapes=[pltpu.VMEM((nh, tq, tq), F32),
                        pltpu.VMEM((nh, tq, tq), BF16),
                        pltpu.VMEM((nh, tq, LANES), F32),
                        pltpu.VMEM((nh, tq, LANES), F32),
                        pltpu.VMEM((nh, tq, LANES), F32),
                        pltpu.VMEM((nh, tq, C_V), F32)],
        compiler_params=_params("parallel", "parallel", "arbitrary"),
        name="attn",
    )(q, k, v)


def _proj_ln_router_kernel(n_in, *refs):
    a_refs = refs[0:n_in]
    w_refs = refs[n_in:2 * n_in]
    x_ref, g_ref, b_ref, wr_ref, br_ref = refs[2 * n_in:2 * n_in + 5]
    x1_ref, x1b_ref, meta_ref, gate_ref, cnt_ref = refs[2 * n_in + 5:2 * n_in + 10]
    base_scr = refs[2 * n_in + 10]
    tm = x_ref.shape[0]
    i = pl.program_id(0)

    @pl.when(i == 0)
    def _():
        base_scr[...] = jnp.zeros(base_scr.shape, F32)

    mix = _dot(a_refs[0][...], w_refs[0][...])
    for j in range(1, n_in):
        mix = mix + _dot(a_refs[j][...], w_refs[j][...])
    x1 = _layer_norm_rows(DEEPNORM_ALPHA * x_ref[...] + mix, g_ref[...], b_ref[...])
    x1_ref[...] = x1
    x1b = x1.astype(BF16)
    x1b_ref[...] = x1b

    work = _dot_nt(wr_ref[...], x1b) + br_ref[:, 0:1]
    eidx = lax.broadcasted_iota(jnp.int32, (N_EXPERTS, tm), 0).astype(F32)
    cnt = jnp.zeros((N_EXPERTS, tm), F32)
    tops, idxs, hots = [], [], []
    for _ in range(TOP_K):
        mx = jnp.max(work, axis=0, keepdims=True)
        idx = jnp.min(jnp.where(work == mx, eidx, float(N_EXPERTS)), axis=0, keepdims=True)
        hot = eidx == idx
        work = jnp.where(hot, NEG_BIG, work)
        cnt = cnt + jnp.where(hot, 1.0, 0.0)
        tops.append(mx)
        idxs.append(idx)
        hots.append(hot)
    exps = [jnp.exp(t - tops[0]) for t in tops]
    denom = exps[0] + exps[1] + exps[2] + exps[3]

    ri = lax.broadcasted_iota(jnp.int32, (tm, tm), 0)
    ci = lax.broadcasted_iota(jnp.int32, (tm, tm), 1)
    before = jnp.where(ri < ci, 1.0, 0.0).astype(BF16)
    base = base_scr[:, 0:1]
    rank = _dot(cnt.astype(BF16), before) + base
    row = lax.broadcasted_iota(jnp.int32, (2 * TOP_K, tm), 0)
    meta = jnp.zeros((2 * TOP_K, tm), jnp.int32)
    gates = jnp.zeros((2 * TOP_K, tm), F32)
    for kk in range(TOP_K):
        pos = jnp.sum(jnp.where(hots[kk], rank, 0.0), axis=0, keepdims=True).astype(jnp.int32)
        meta = jnp.where(row == kk, idxs[kk].astype(jnp.int32), meta)
        meta = jnp.where(row == TOP_K + kk, pos, meta)
        gates = jnp.where(row == kk, exps[kk] / denom, gates)
    meta_ref[...] = meta
    gate_ref[...] = gates
    total = base + jnp.sum(cnt, axis=1, keepdims=True)
    base_scr[...] = jnp.broadcast_to(total, base_scr.shape)
    cnt_ref[...] = jnp.broadcast_to(total, cnt_ref.shape)


def _proj_ln_router(a_list, w_list, x, ln_g, ln_b, w_r, b_r, tm):
    t = x.shape[0]
    n_in = len(a_list)
    in_specs = [pl.BlockSpec((tm, a.shape[1]), lambda i: (i, 0)) for a in a_list]
    in_specs += [pl.BlockSpec(w.shape, lambda i: (0, 0)) for w in w_list]
    in_specs += [pl.BlockSpec((tm, D_MODEL), lambda i: (i, 0)),
                 pl.BlockSpec((1, D_MODEL), lambda i: (0, 0)),
                 pl.BlockSpec((1, D_MODEL), lambda i: (0, 0)),
                 pl.BlockSpec((N_EXPERTS, D_MODEL), lambda i: (0, 0)),
                 pl.BlockSpec((N_EXPERTS, LANES), lambda i: (0, 0))]
    return pl.pallas_call(
        functools.partial(_proj_ln_router_kernel, n_in),
        grid=(t // tm,),
        in_specs=in_specs,
        out_specs=[pl.BlockSpec((tm, D_MODEL), lambda i: (i, 0)),
                   pl.BlockSpec((tm, D_MODEL), lambda i: (i, 0)),
                   pl.BlockSpec((2 * TOP_K, tm), lambda i: (0, i)),
                   pl.BlockSpec((2 * TOP_K, tm), lambda i: (0, i)),
                   pl.BlockSpec((N_EXPERTS, LANES), lambda i: (0, 0))],
        out_shape=[jax.ShapeDtypeStruct((t, D_MODEL), F32),
                   jax.ShapeDtypeStruct((t, D_MODEL), BF16),
                   jax.ShapeDtypeStruct((2 * TOP_K, t), jnp.int32),
                   jax.ShapeDtypeStruct((2 * TOP_K, t), F32),
                   jax.ShapeDtypeStruct((N_EXPERTS, LANES), F32)],
        scratch_shapes=[pltpu.VMEM((N_EXPERTS, LANES), F32)],
        compiler_params=_params("arbitrary"),
        name="proj_ln_router",
    )(*a_list, *w_list, x, ln_g, ln_b, w_r, b_r)


def _moe_kernel(blk_ref, exp_ref, lo_ref, hi_ref, x_ref, wgu_ref, wd_ref, bg_ref, bu_ref, bd_ref, o_ref,
                wg_scr, wu_scr, wd_scr, t_scr):
    i = pl.program_id(0)
    tm = x_ref.shape[0]
    blk = blk_ref[i]
    lo = lo_ref[i]
    hi = hi_ref[i]
    prev = jnp.maximum(i - 1, 0)
    first = jnp.logical_or(i == 0, blk != blk_ref[prev])
    new_expert = jnp.logical_or(i == 0, exp_ref[i] != exp_ref[prev])

    @pl.when(new_expert)
    def _():
        pc = t_scr.shape[1]
        half = pc // 2
        for c in range(2 * D_FF // pc):
            for kc in range(D_MODEL // LANES):
                k0 = kc * LANES
                t_scr[kc] = wgu_ref[0, 0, k0:k0 + LANES, c * pc:(c + 1) * pc].T
                wg_scr[c * half:(c + 1) * half, k0:k0 + LANES] = t_scr[kc, pl.ds(0, half, stride=2), :].astype(BF16)
                wu_scr[c * half:(c + 1) * half, k0:k0 + LANES] = t_scr[kc, pl.ds(1, half, stride=2), :].astype(BF16)
        wd_scr[...] = wd_ref[0, 0].astype(BF16)

    @pl.when(first)
    def _():
        o_ref[...] = jnp.zeros(o_ref.shape, o_ref.dtype)

    @pl.when(hi > lo)
    def _():
        x = x_ref[...]
        glu = jnp.minimum(_dot_nt(x, wg_scr[...]) + bg_ref[0, 0], SWIGLU_LIMIT)
        lin = jnp.clip(_dot_nt(x, wu_scr[...]) + bu_ref[0, 0], -SWIGLU_LIMIT, SWIGLU_LIMIT)
        act = glu * _sigmoid(SWIGLU_ALPHA * glu) * (lin + 1.0)
        y = _dot(act.astype(BF16), wd_scr[...]) + bd_ref[0, 0]
        row = blk * tm + lax.broadcasted_iota(jnp.int32, (tm, 1), 0)
        keep = jnp.logical_and(row >= lo, row < hi)
        o_ref[...] = jnp.where(keep, y.astype(o_ref.dtype), o_ref[...])


def _moe_sorted(layer, xs, counts, w_gate_up, b_gate, b_up, w_down, b_down):
    n_rows = xs.shape[0]
    tm = _row_tile(n_rows, MOE_ROWS)
    items = _moe_items(counts, n_rows, tm)
    n_items = items[0].shape[0]

    def by_expert(i, blk, ex, lo, hi):
        return (layer, ex[i], 0, 0)

    def by_block(i, blk, ex, lo, hi):
        return (blk[i], 0)

    grid_spec = pltpu.PrefetchScalarGridSpec(
        num_scalar_prefetch=4,
        grid=(n_items,),
        in_specs=[pl.BlockSpec((tm, D_MODEL), by_block),
                  pl.BlockSpec((1, 1, D_MODEL, 2 * D_FF), by_expert),
                  pl.BlockSpec((1, 1, D_FF, D_MODEL), by_expert),
                  pl.BlockSpec((1, 1, 1, D_FF), by_expert),
                  pl.BlockSpec((1, 1, 1, D_FF), by_expert),
                  pl.BlockSpec((1, 1, 1, D_MODEL), by_expert)],
        out_specs=pl.BlockSpec((tm, D_MODEL), by_block),
        scratch_shapes=[pltpu.VMEM((D_FF, D_MODEL), BF16),
                        pltpu.VMEM((D_FF, D_MODEL), BF16),
                        pltpu.VMEM((D_FF, D_MODEL), BF16),
                        pltpu.VMEM((D_MODEL // LANES, MOE_PREP_COLS, LANES), F32)],
    )
    return pl.pallas_call(
        _moe_kernel,
        grid_spec=grid_spec,
        out_shape=jax.ShapeDtypeStruct((n_rows, D_MODEL), BF16),
        compiler_params=_params("arbitrary", vmem=MOE_VMEM_LIMIT),
        name="moe_ffn",
    )(*items, xs, w_gate_up, w_down, b_gate, b_up, b_down)


def _moe_items(counts, n_rows, tm):
    n_blocks = n_rows // tm
    start = jnp.cumsum(counts) - counts
    cuts = jnp.sort(jnp.concatenate([jnp.arange(n_blocks, dtype=jnp.int32) * tm, start.astype(jnp.int32)]))
    lo = cuts
    hi = jnp.concatenate([cuts[1:], jnp.array([n_rows], jnp.int32)])
    blk = jnp.minimum(lo // tm, n_blocks - 1)
    ex = jnp.clip(jnp.sum((start[None, :] <= lo[:, None]).astype(jnp.int32), axis=1) - 1, 0, N_EXPERTS - 1)
    return blk.astype(jnp.int32), ex.astype(jnp.int32), lo.astype(jnp.int32), hi.astype(jnp.int32)


def _combine_ln_kernel(x_ref, y_ref, gate_ref, g_ref, b_ref, o_ref):
    gates = gate_ref[...]
    ffn = y_ref[0].astype(F32) * gates[:, 0:1]
    for kk in range(1, TOP_K):
        ffn = ffn + y_ref[kk].astype(F32) * gates[:, kk:kk + 1]
    o_ref[...] = _layer_norm_rows(DEEPNORM_ALPHA * x_ref[...] + ffn, g_ref[...], b_ref[...])


def _combine_ln(x1, y4, gates, ln_g, ln_b, tm):
    t = x1.shape[0]
    return pl.pallas_call(
        _combine_ln_kernel,
        grid=(t // tm,),
        in_specs=[pl.BlockSpec((tm, D_MODEL), lambda i: (i, 0)),
                  pl.BlockSpec((TOP_K, tm, D_MODEL), lambda i: (0, i, 0)),
                  pl.BlockSpec((tm, 2 * TOP_K), lambda i: (i, 0)),
                  pl.BlockSpec((1, D_MODEL), lambda i: (0, 0)),
                  pl.BlockSpec((1, D_MODEL), lambda i: (0, 0))],
        out_specs=pl.BlockSpec((tm, D_MODEL), lambda i: (i, 0)),
        out_shape=jax.ShapeDtypeStruct((t, D_MODEL), F32),
        compiler_params=_params("parallel"),
        name="combine_ln",
    )(x1, y4, gates, ln_g, ln_b)


def _moe_layer(layer, x1, x1b, meta_t, gates_t, counts, ln_g, ln_b, w_gate_up, b_gate, b_up, w_down, b_down):
    t = x1.shape[0]
    assert N_EXPERTS * t < 2 ** 31
    e = meta_t[0:TOP_K]
    pos = meta_t[TOP_K:2 * TOP_K]
    cnt = counts[:, 0].astype(jnp.int32)
    start = jnp.cumsum(cnt) - cnt
    experts = jnp.arange(N_EXPERTS, dtype=jnp.int32)
    dest = pos + jnp.sum(jnp.where(e[:, :, None] == experts, start, 0), axis=-1)
    keys = e * t + jnp.arange(t, dtype=jnp.int32)[None, :]
    tok_sorted = jnp.remainder(jnp.sort(keys.reshape(-1)), t)
    xs = x1b[tok_sorted]
    ys = _moe_sorted(layer, xs, cnt, w_gate_up, b_gate, b_up, w_down, b_down)
    y4 = ys[dest]
    return _combine_ln(x1, y4, gates_t.T, ln_g, ln_b, _row_tile(t, ROWS))


def kernel(x, even_w_in, even_sgu_ln_g, even_sgu_ln_b, even_sgu_w, even_sgu_b, even_conv_w, even_a_log, even_dt_bias, even_gdn_norm, even_w_out, odd_w_in, odd_q_norm, odd_w_uq, odd_kv_norm, odd_w_ukv, odd_w_out, ln_mix_g, ln_mix_b, ln_ffn_g, ln_ffn_b, moe_w_router, moe_b_router, moe_w_gate_up, moe_b_gate_up, moe_w_down, moe_b_down):
    bsz, seq, d = x.shape
    t = bsz * seq
    xt = x.reshape(t, d)
    tm = _row_tile(t, ROWS)
    b_gate = moe_b_gate_up[:, :, None, 0::2]
    b_up = moe_b_gate_up[:, :, None, 1::2]
    b_down = moe_b_down[:, :, None, :]

    inv_freq = ROPE_THETA ** (-jnp.arange(0, C_ROPE, 2, dtype=F32) / C_ROPE)
    ang = jnp.arange(seq, dtype=F32)[:, None] * inv_freq[None, :]
    cs = jnp.concatenate([jnp.cos(ang), jnp.cos(ang), jnp.sin(ang), jnp.sin(ang)], axis=1)

    pos_chunk = np.arange(A_BLOCK) // CHUNK
    allowed = jnp.asarray(pos_chunk[None, :] <= pos_chunk[:, None])

    def rot_cols(w):
        half = C_ROPE // 2
        return jnp.concatenate([-w[..., half:], w[..., :half]], axis=-1)

    for layer in range(DEPTH):
        i = layer // 2
        if layer % 2 == 0:
            w_in = even_w_in[i]
            w_a = w_in[:, 0:2 * A_WIDTH].astype(BF16)
            w_b = jnp.pad(w_in[:, 2 * A_WIDTH:], ((0, 0), (0, LANES - 2 * B_HEADS))).astype(BF16)
            sw = jnp.where(allowed[None], even_sgu_w[i], 0).astype(BF16)
            sb = jnp.broadcast_to(even_sgu_b[i][:, :, None], (A_GROUPS, A_BLOCK, A_GROUP_DIM)).astype(F32)
            y_a = _sgu(xt, w_a, even_sgu_ln_g[i][None], even_sgu_ln_b[i][None], sw, sb, tm)
            h_b, ab = _mm_split(xt, w_b, 4 * B_WIDTH, BF16, F32, tm)
            abt = ab.reshape(bsz, seq, LANES)[:, :, 0:SUBLANES].transpose(0, 2, 1)
            hp = jnp.pad(jnp.stack([even_a_log[i], even_dt_bias[i]]), ((0, 0), (0, LANES - B_HEADS)))
            hpt = jnp.pad(jnp.stack([even_a_log[i], even_dt_bias[i]], axis=1), ((0, SUBLANES - B_HEADS), (0, 0)))
            y_b = _gdn(h_b, ab, abt, even_conv_w[i], hp, hpt, even_gdn_norm[i][None], bsz, seq)
            w_out = even_w_out[i].astype(BF16)
            a_list = [y_a, y_b]
            w_list = [w_out[0:A_WIDTH], w_out[A_WIDTH:]]
        else:
            w_in = odd_w_in[i]
            w_pe = w_in[:, Q_LORA + KV_LORA:]
            w_in2 = jnp.concatenate([w_in, rot_cols(w_pe)], axis=1).astype(BF16)
            w_uq = odd_w_uq[i].reshape(Q_LORA, C_HEADS, C_QK)
            w_uq2 = jnp.concatenate([w_uq, rot_cols(w_uq[..., C_NOPE:])], axis=-1)
            w_uq2 = w_uq2.reshape(Q_LORA, C_HEADS * (C_NOPE + 2 * C_ROPE)).astype(BF16)
            h = _mm(xt, w_in2, BF16, tm)
            q, k, v = _mla_up(h, odd_q_norm[i][None], odd_kv_norm[i][None], w_uq2, odd_w_ukv[i].astype(BF16), cs,
                              bsz, seq, _row_tile(seq, ROWS))
            o = _attn(q, k, v)
            a_list = [o.reshape(t, C_HEADS * C_V)]
            w_list = [odd_w_out[i].astype(BF16)]

        w_r = moe_w_router[layer].T.astype(BF16)
        b_r = jnp.broadcast_to(moe_b_router[layer][:, None], (N_EXPERTS, LANES))
        x1, x1b, meta, gates, counts = _proj_ln_router(a_list, w_list, xt, ln_mix_g[layer][None],
                                                       ln_mix_b[layer][None], w_r, b_r, tm)
        xt = _moe_layer(layer, x1, x1b, meta, gates, counts, ln_ffn_g[layer][None], ln_ffn_b[layer][None],
                        moe_w_gate_up, b_gate, b_up, moe_w_down, b_down)
    return xt.reshape(bsz, seq, d)
```

```python
import functools

import jax
import jax.numpy as jnp
import numpy as np
from jax import lax
from jax.experimental import pallas as pl
from jax.experimental.pallas import tpu as pltpu

F32 = jnp.float32
BF16 = jnp.bfloat16

D_MODEL = 1024
DEPTH = 4
CHUNK = 64
CHUNK_SHIFT = CHUNK.bit_length() - 1
DEEPNORM_ALPHA = (2 * DEPTH) ** 0.25
LN_EPS = 1e-5
RMS_EPS = 1e-6

A_BLOCK = 128
A_GROUPS = 4
A_WIDTH = D_MODEL // 2
A_GROUP_DIM = A_WIDTH // A_GROUPS

B_HEADS = 4
B_HEAD_DIM = D_MODEL // 8
B_WIDTH = B_HEADS * B_HEAD_DIM
CONV_K = 4

C_HEADS = 8
C_NOPE = 128
C_ROPE = 64
C_V = D_MODEL // C_HEADS
C_QK = C_NOPE + C_ROPE
Q_LORA = 3 * D_MODEL // 8
KV_LORA = D_MODEL // 4
ROPE_THETA = 10000.0

N_EXPERTS = 32
TOP_K = 4
D_FF = D_MODEL
SWIGLU_LIMIT = 7.0
SWIGLU_ALPHA = 1.702

LANES = 128
SUBLANES = 8
VMEM_LIMIT = 48 * 1024 * 1024
MOE_VMEM_LIMIT = 58 * 1024 * 1024
NEG_BIG = -1e30
LOG2_E = 1.4426950408889634

ROWS = 512
GDN_ROWS = 256
INV_BASE_SHIFT = 1
ATTN_ROWS = 512
ATTN_HEADS_PER_STEP = 4
ATTN_SUB = 128
MOE_ROWS = 512
MOE_PREP_COLS = 512


def _row_tile(t, pref):
    tm = pref
    while t % tm:
        tm //= 2
    return tm


def _params(*sem, vmem=VMEM_LIMIT):
    return pltpu.CompilerParams(dimension_semantics=sem, vmem_limit_bytes=vmem)


def _dot(a, b):
    return jnp.dot(a, b, preferred_element_type=F32)


def _dot_nt(a, b):
    return lax.dot_general(a, b, (((1,), (1,)), ((), ())), preferred_element_type=F32)


def _dot_tn(a, b):
    return lax.dot_general(a, b, (((0,), (0,)), ((), ())), preferred_element_type=F32)


def _sigmoid(x):
    return 1.0 / (1.0 + jnp.exp(-x))


def _softplus(x):
    return jnp.maximum(x, 0.0) + jnp.log1p(jnp.exp(-jnp.abs(x)))


def _layer_norm_rows(y, g, b):
    mu = jnp.mean(y, axis=-1, keepdims=True)
    d = y - mu
    var = jnp.mean(d * d, axis=-1, keepdims=True)
    return d * lax.rsqrt(var + LN_EPS) * g + b


def _gelu(x):
    return 0.5 * x * (1.0 + lax.erf(x * (2.0 ** -0.5)))


def _even_in_kernel(x_ref, w_ref, g_ref, b_ref, sw_ref, sb_ref, o_ref, hb_ref, ab_ref):
    tm = x_ref.shape[0]
    h = _dot(x_ref[...].astype(BF16), w_ref[...])
    n_b = hb_ref.shape[1]
    hb_ref[...] = h[:, 2 * A_WIDTH:2 * A_WIDTH + n_b].astype(hb_ref.dtype)
    ab_ref[...] = h[:, 2 * A_WIDTH + n_b:]
    for grp in range(A_GROUPS):
        c0 = grp * A_GROUP_DIM
        u = _gelu(h[:, c0:c0 + A_GROUP_DIM])
        v = _gelu(h[:, A_WIDTH + c0:A_WIDTH + c0 + A_GROUP_DIM])
        vn = _layer_norm_rows(v, g_ref[:, c0:c0 + A_GROUP_DIM], b_ref[:, c0:c0 + A_GROUP_DIM]).astype(BF16)
        for blk in range(tm // A_BLOCK):
            r0 = blk * A_BLOCK
            mixed = _dot(sw_ref[grp], vn[r0:r0 + A_BLOCK]) + sb_ref[grp]
            o_ref[r0:r0 + A_BLOCK, c0:c0 + A_GROUP_DIM] = (u[r0:r0 + A_BLOCK] * mixed).astype(o_ref.dtype)


def _even_in(x, w_all, ln_g, ln_b, sw, sb, tm):
    t = x.shape[0]
    n_all = w_all.shape[1]
    return pl.pallas_call(
        _even_in_kernel,
        grid=(t // tm,),
        in_specs=[pl.BlockSpec((tm, D_MODEL), lambda i: (i, 0)),
                  pl.BlockSpec((D_MODEL, n_all), lambda i: (0, 0)),
                  pl.BlockSpec((1, A_WIDTH), lambda i: (0, 0)),
                  pl.BlockSpec((1, A_WIDTH), lambda i: (0, 0)),
                  pl.BlockSpec((A_GROUPS, A_BLOCK, A_BLOCK), lambda i: (0, 0, 0)),
                  pl.BlockSpec((A_GROUPS, A_BLOCK, A_GROUP_DIM), lambda i: (0, 0, 0))],
        out_specs=[pl.BlockSpec((tm, A_WIDTH), lambda i: (i, 0)),
                   pl.BlockSpec((tm, 4 * B_WIDTH), lambda i: (i, 0)),
                   pl.BlockSpec((tm, LANES), lambda i: (i, 0))],
        out_shape=[jax.ShapeDtypeStruct((t, A_WIDTH), BF16),
                   jax.ShapeDtypeStruct((t, 4 * B_WIDTH), BF16),
                   jax.ShapeDtypeStruct((t, LANES), F32)],
        compiler_params=_params("parallel"),
        name="even_in",
    )(x, w_all, ln_g, ln_b, sw, sb)


def _gdn_kernel(h_ref, ab_ref, abt_ref, cw_ref, hp_ref, hpt_ref, nw_ref, o_ref, conv_scr, state_scr):
    tr = h_ref.shape[0]
    n_chunks = tr // CHUNK
    s_idx = pl.program_id(1)

    @pl.when(s_idx == 0)
    def _():
        conv_scr[0:SUBLANES, :] = jnp.zeros((SUBLANES, 3 * B_WIDTH), F32)
        state_scr[...] = jnp.zeros(state_scr.shape, F32)

    conv_scr[SUBLANES:SUBLANES + tr, :] = h_ref[:, 0:3 * B_WIDTH].astype(F32)
    acc = conv_scr[SUBLANES:SUBLANES + tr, :] * cw_ref[CONV_K - 1:CONV_K, :]
    for d in range(1, CONV_K):
        acc = acc + conv_scr[SUBLANES - d:SUBLANES - d + tr, :] * cw_ref[CONV_K - 1 - d:CONV_K - d, :]
    conv_scr[0:SUBLANES, :] = conv_scr[tr:tr + SUBLANES, :]
    qkv = acc * _sigmoid(acc)

    ab = ab_ref[...]
    sp_c = _softplus(ab + hp_ref[1:2, :])
    g_col = -jnp.exp(hp_ref[0:1, :]) * sp_c
    beta_col = _sigmoid(ab)
    abt = abt_ref[0]
    sp_r = _softplus(abt + hpt_ref[:, 1:2])
    g_row = -jnp.exp(hpt_ref[:, 0:1]) * sp_r

    ri = lax.broadcasted_iota(jnp.int32, (tr, tr), 0)
    ci = lax.broadcasted_iota(jnp.int32, (tr, tr), 1)
    same = jnp.right_shift(ri, CHUNK_SHIFT) == jnp.right_shift(ci, CHUNK_SHIFT)
    lower = jnp.where(same & (ri >= ci), 1.0, 0.0).astype(F32)
    upper = jnp.where(same & (ri <= ci), 1.0, 0.0).astype(F32)
    gc_col = jnp.dot(lower, g_col, preferred_element_type=F32, precision=lax.Precision.HIGHEST)
    gc_row = jnp.dot(g_row, upper, preferred_element_type=F32, precision=lax.Precision.HIGHEST)

    incl = same & (ri >= ci)
    strict = same & (ri > ci)
    eye = jnp.where(ri == ci, 1.0, 0.0).astype(F32)

    heads = range(B_HEADS)
    qs, ks, vs, betas, gccs, decays, k_bs, kbs, ps, ts = [], [], [], [], [], [], [], [], [], []
    for hd in heads:
        l0 = hd * B_HEAD_DIM
        q = qkv[:, l0:l0 + B_HEAD_DIM]
        k = qkv[:, B_WIDTH + l0:B_WIDTH + l0 + B_HEAD_DIM]
        qs.append(q * lax.rsqrt(jnp.sum(q * q, axis=-1, keepdims=True) + RMS_EPS) * (B_HEAD_DIM ** -0.5))
        ks.append(k * lax.rsqrt(jnp.sum(k * k, axis=-1, keepdims=True) + RMS_EPS))
        vs.append(qkv[:, 2 * B_WIDTH + l0:2 * B_WIDTH + l0 + B_HEAD_DIM])
        betas.append(beta_col[:, B_HEADS + hd:B_HEADS + hd + 1])
        gccs.append(gc_col[:, hd:hd + 1])
        gcr = gc_row[hd:hd + 1, :]
        decays.append(jnp.where(incl, jnp.exp(jnp.where(incl, gccs[hd] - gcr, 0.0)), 0.0))
        k_bs.append(ks[hd].astype(BF16))
        kbs.append(ks[hd] * betas[hd])
    lvl = [jnp.right_shift(ri, s) == jnp.right_shift(ci, s) for s in range(INV_BASE_SHIFT, CHUNK_SHIFT + 1)]
    a_mats = []
    for hd in heads:
        a_mat = jnp.where(strict, _dot_nt(kbs[hd].astype(BF16), k_bs[hd]) * decays[hd], 0.0)
        a_mats.append(a_mat)
        d_mat = jnp.where(lvl[0], a_mat, 0.0)
        ps.append(-d_mat)
        ts.append(eye - d_mat)
    for _ in range(INV_BASE_SHIFT - 1):
        for hd in heads:
            pb = ps[hd].astype(BF16)
            ps[hd] = _dot(pb, pb)
        for hd in heads:
            ts[hd] = ts[hd] + _dot(ts[hd].astype(BF16), ps[hd].astype(BF16))
    for j in range(1, len(lvl)):
        for hd in heads:
            join = jnp.where(lvl[j] & jnp.logical_not(lvl[j - 1]), a_mats[hd], 0.0).astype(BF16)
            t_b = ts[hd].astype(BF16)
            ps[hd] = _dot(_dot(t_b, join).astype(BF16), t_b)
        for hd in heads:
            ts[hd] = ts[hd] - ps[hd]
    us, ws, intras, q_decs = [], [], [], []
    for hd in heads:
        t_b = ts[hd].astype(BF16)
        e_gc = jnp.exp(gccs[hd])
        us.append(_dot(t_b, (vs[hd] * betas[hd]).astype(BF16)))
        ws.append(_dot(t_b, (kbs[hd] * e_gc).astype(BF16)).astype(BF16))
        intras.append(jnp.where(incl, _dot_nt(qs[hd].astype(BF16), k_bs[hd]) * decays[hd], 0.0).astype(BF16))
        q_decs.append((qs[hd] * e_gc).astype(BF16))
    for c in range(n_chunks):
        r0 = c * CHUNK
        for hd in heads:
            l0 = hd * B_HEAD_DIM
            g_last = gccs[hd][r0 + CHUNK - 1:r0 + CHUNK, :]
            k_dec = (ks[hd][r0:r0 + CHUNK] * jnp.exp(g_last - gccs[hd][r0:r0 + CHUNK])).astype(BF16)
            state = state_scr[hd]
            state_b = state.astype(BF16)
            v_new = (us[hd][r0:r0 + CHUNK] - _dot(ws[hd][r0:r0 + CHUNK], state_b)).astype(BF16)
            out = _dot(q_decs[hd][r0:r0 + CHUNK], state_b) + _dot(intras[hd][r0:r0 + CHUNK, r0:r0 + CHUNK], v_new)
            state_scr[hd] = state * jnp.exp(g_last) + _dot_tn(k_dec, v_new)
            z = h_ref[r0:r0 + CHUNK, 3 * B_WIDTH + l0:3 * B_WIDTH + l0 + B_HEAD_DIM].astype(F32)
            o_n = out * lax.rsqrt(jnp.mean(out * out, axis=-1, keepdims=True) + RMS_EPS) * nw_ref[...]
            o_ref[r0:r0 + CHUNK, l0:l0 + B_HEAD_DIM] = (o_n * (z * _sigmoid(z))).astype(o_ref.dtype)


def _gdn(h_b, ab, abt, conv_w, hp, hpt, norm_w, bsz, seq):
    tr = _row_tile(seq, GDN_ROWS)
    n_s = seq // tr
    return pl.pallas_call(
        _gdn_kernel,
        grid=(bsz, n_s),
        in_specs=[pl.BlockSpec((tr, 4 * B_WIDTH), lambda b, s: (b * n_s + s, 0)),
                  pl.BlockSpec((tr, LANES), lambda b, s: (b * n_s + s, 0)),
                  pl.BlockSpec((1, SUBLANES, tr), lambda b, s: (b, 0, s)),
                  pl.BlockSpec((CONV_K, 3 * B_WIDTH), lambda b, s: (0, 0)),
                  pl.BlockSpec((2, LANES), lambda b, s: (0, 0)),
                  pl.BlockSpec((SUBLANES, 2), lambda b, s: (0, 0)),
                  pl.BlockSpec((1, B_HEAD_DIM), lambda b, s: (0, 0))],
        out_specs=pl.BlockSpec((tr, B_WIDTH), lambda b, s: (b * n_s + s, 0)),
        out_shape=jax.ShapeDtypeStruct((bsz * seq, B_WIDTH), BF16),
        scratch_shapes=[pltpu.VMEM((tr + 2 * SUBLANES, 3 * B_WIDTH), F32),
                        pltpu.VMEM((B_HEADS, B_HEAD_DIM, B_HEAD_DIM), F32)],
        compiler_params=_params("parallel", "arbitrary"),
        name="gdn",
    )(h_b, ab, abt, conv_w, hp, hpt, norm_w)


def _mla_up_kernel(x_ref, win_ref, qn_ref, kvn_ref, wq_ref, wkv_ref, cs_ref, q_ref, k_ref, v_ref):
    h = _dot(x_ref[...].astype(BF16), win_ref[...])
    cs = cs_ref[...]

    def rms(x, g):
        return x * lax.rsqrt(jnp.mean(x * x, axis=-1, keepdims=True) + RMS_EPS) * g

    def rope(t):
        r = t * cs
        return r + pltpu.roll(r, C_ROPE, 1)

    cq = rms(h[:, 0:Q_LORA], qn_ref[...]).astype(BF16)
    ckv = rms(h[:, Q_LORA:Q_LORA + KV_LORA], kvn_ref[...]).astype(BF16)
    k_pe = rope(h[:, Q_LORA + KV_LORA:Q_LORA + KV_LORA + 2 * C_ROPE])[:, 0:C_ROPE].astype(BF16)
    qf = _dot(cq, wq_ref[...]) * (C_QK ** -0.5 * LOG2_E)
    kv = _dot(ckv, wkv_ref[...])
    for hd in range(C_HEADS):
        q0 = hd * (C_NOPE + 2 * C_ROPE)
        q_ref[0, hd, :, 0:C_NOPE] = qf[:, q0:q0 + C_NOPE].astype(BF16)
        q_ref[0, hd, :, C_NOPE:C_QK] = rope(qf[:, q0 + C_NOPE:q0 + C_NOPE + 2 * C_ROPE])[:, 0:C_ROPE].astype(BF16)
        k0 = hd * (C_NOPE + C_V)
        k_ref[0, hd, :, 0:C_NOPE] = kv[:, k0:k0 + C_NOPE].astype(BF16)
        k_ref[0, hd, :, C_NOPE:C_QK] = k_pe
        v_ref[0, hd, :, :] = kv[:, k0 + C_NOPE:k0 + C_NOPE + C_V].astype(BF16)


def _mla_up(x, w_in, q_norm, kv_norm, w_uq, w_ukv, cs, bsz, seq, tm):
    n_s = seq // tm
    qk_shape = jax.ShapeDtypeStruct((bsz, C_HEADS, seq, C_QK), BF16)
    return pl.pallas_call(
        _mla_up_kernel,
        grid=(bsz, n_s),
        in_specs=[pl.BlockSpec((tm, D_MODEL), lambda b, s: (b * n_s + s, 0)),
                  pl.BlockSpec(w_in.shape, lambda b, s: (0, 0)),
                  pl.BlockSpec((1, Q_LORA), lambda b, s: (0, 0)),
                  pl.BlockSpec((1, KV_LORA), lambda b, s: (0, 0)),
                  pl.BlockSpec(w_uq.shape, lambda b, s: (0, 0)),
                  pl.BlockSpec(w_ukv.shape, lambda b, s: (0, 0)),
                  pl.BlockSpec((tm, LANES), lambda b, s: (s, 0))],
        out_specs=[pl.BlockSpec((1, C_HEADS, tm, C_QK), lambda b, s: (b, 0, s, 0)),
                   pl.BlockSpec((1, C_HEADS, tm, C_QK), lambda b, s: (b, 0, s, 0)),
                   pl.BlockSpec((1, C_HEADS, tm, C_V), lambda b, s: (b, 0, s, 0))],
        out_shape=[qk_shape, qk_shape, jax.ShapeDtypeStruct((bsz, C_HEADS, seq, C_V), BF16)],
        compiler_params=_params("parallel", "parallel"),
        name="mla_up",
    )(x, w_in, q_norm, kv_norm, w_uq, w_ukv, cs)


def _attn_kernel(q_ref, k_ref, v_ref, o_ref, s_scr, p_scr, m_scr, l_scr, a_scr, acc_scr):
    nh, tq = q_ref.shape[1], q_ref.shape[2]
    n_lt = tq // LANES
    qi = pl.program_id(2)

    def scores(hd, k0):
        s_scr[hd] = _dot_nt(q_ref[0, hd], k_ref[0, hd, pl.ds(k0, tq), :])

    def softmax(hd, first):
        for r0 in range(0, tq, ATTN_SUB):
            rows = slice(r0, r0 + ATTN_SUB)
            n_vis = -(-(r0 + ATTN_SUB) // LANES) if first else n_lt
            tiles = []
            for c in range(n_vis):
                s = s_scr[hd, rows, c * LANES:(c + 1) * LANES]
                if first and ((c + 1) * LANES - 1) // CHUNK > r0 // CHUNK:
                    ri = jnp.right_shift(r0 + lax.broadcasted_iota(jnp.int32, (ATTN_SUB, LANES), 0), CHUNK_SHIFT)
                    ci = jnp.right_shift(c * LANES + lax.broadcasted_iota(jnp.int32, (ATTN_SUB, LANES), 1),
                                         CHUNK_SHIFT)
                    s = jnp.where(ci <= ri, s, NEG_BIG)
                tiles.append(s)
            mx = tiles[0]
            for c in range(1, n_vis):
                mx = jnp.maximum(mx, tiles[c])
            m_new = jnp.broadcast_to(jnp.max(mx, axis=-1, keepdims=True), (ATTN_SUB, LANES))
            if not first:
                m_prev = m_scr[hd, rows, :]
                m_new = jnp.maximum(m_prev, m_new)
                alpha = jnp.exp2(m_prev - m_new)
                a_scr[hd, rows, :] = alpha
            psum = jnp.zeros((ATTN_SUB, LANES), F32)
            for c in range(n_vis):
                p = jnp.exp2(tiles[c] - m_new)
                psum = psum + p
                p_scr[hd, rows, c * LANES:(c + 1) * LANES] = p.astype(BF16)
            for c in range(n_vis, n_lt):
                p_scr[hd, rows, c * LANES:(c + 1) * LANES] = jnp.zeros((ATTN_SUB, LANES), BF16)
            m_scr[hd, rows, :] = m_new
            l_new = jnp.broadcast_to(jnp.sum(psum, axis=-1, keepdims=True), (ATTN_SUB, LANES))
            l_scr[hd, rows, :] = l_new if first else alpha * l_scr[hd, rows, :] + l_new

    def values(hd, k0, first):
        pv = _dot(p_scr[hd], v_ref[0, hd, pl.ds(k0, tq), :])
        acc_scr[hd] = pv if first else a_scr[hd] * acc_scr[hd] + pv

    def block(k0, first):
        for hd in range(nh):
            scores(hd, k0)
        for hd in range(nh):
            softmax(hd, first)
        for hd in range(nh):
            values(hd, k0, first)

    def body(j, carry):
        block(pl.multiple_of(j * tq, tq), False)
        return carry

    block(pl.multiple_of(qi * tq, tq), True)
    lax.fori_loop(0, qi, body, 0)
    for hd in range(nh):
        o_ref[0, :, hd * C_V:(hd + 1) * C_V] = (acc_scr[hd] / l_scr[hd]).astype(o_ref.dtype)


def _attn(q, k, v):
    bsz, _, seq, _ = q.shape
    tq = _row_tile(seq, ATTN_ROWS)
    nh = ATTN_HEADS_PER_STEP
    return pl.pallas_call(
        _attn_kernel,
        grid=(bsz, C_HEADS // nh, seq // tq),
        in_specs=[pl.BlockSpec((1, nh, tq, C_QK), lambda b, h, i: (b, h, i, 0)),
                  pl.BlockSpec((1, nh, seq, C_QK), lambda b, h, i: (b, h, 0, 0)),
                  pl.BlockSpec((1, nh, seq, C_V), lambda b, h, i: (b, h, 0, 0))],
        out_specs=pl.BlockSpec((1, tq, nh * C_V), lambda b, h, i: (b, i, h)),
        out_shape=jax.ShapeDtypeStruct((bsz, seq, C_HEADS * C_V), BF16),
        scratch_shapes=[pltpu.VMEM((nh, tq, tq), F32),
                        pltpu.VMEM((nh, tq, tq), BF16),
                        pltpu.VMEM((nh, tq, LANES), F32),
                        pltpu.VMEM((nh, tq, LANES), F32),
                        pltpu.VMEM((nh, tq, LANES), F32),
                        pltpu.VMEM((nh, tq, C_V), F32)],
        compiler_params=_params("parallel", "parallel", "arbitrary"),
        name="attn",
    )(q, k, v)


def _proj_ln_router_kernel(n_in, *refs):
    a_refs = refs[0:n_in]
    w_refs = refs[n_in:2 * n_in]
    x_ref, g_ref, b_ref, wr_ref, br_ref = refs[2 * n_in:2 * n_in + 5]
    x1_ref, x1b_ref, meta_ref, gate_ref, cnt_ref = refs[2 * n_in + 5:2 * n_in + 10]
    base_scr = refs[2 * n_in + 10]
    tm = x_ref.shape[0]
    i = pl.program_id(0)

    @pl.when(i == 0)
    def _():
        base_scr[...] = jnp.zeros(base_scr.shape, F32)

    mix = _dot(a_refs[0][...], w_refs[0][...])
    for j in range(1, n_in):
        mix = mix + _dot(a_refs[j][...], w_refs[j][...])
    x1 = _layer_norm_rows(DEEPNORM_ALPHA * x_ref[...] + mix, g_ref[...], b_ref[...])
    x1_ref[...] = x1
    x1b = x1.astype(BF16)
    x1b_ref[...] = x1b

    work = _dot_nt(wr_ref[...], x1b) + br_ref[:, 0:1]
    eidx = lax.broadcasted_iota(jnp.int32, (N_EXPERTS, tm), 0).astype(F32)
    cnt = jnp.zeros((N_EXPERTS, tm), F32)
    tops, idxs, hots = [], [], []
    for _ in range(TOP_K):
        mx = jnp.max(work, axis=0, keepdims=True)
        idx = jnp.min(jnp.where(work == mx, eidx, float(N_EXPERTS)), axis=0, keepdims=True)
        hot = eidx == idx
        work = jnp.where(hot, NEG_BIG, work)
        cnt = cnt + jnp.where(hot, 1.0, 0.0)
        tops.append(mx)
        idxs.append(idx)
        hots.append(hot)
    exps = [jnp.exp(t - tops[0]) for t in tops]
    denom = exps[0] + exps[1] + exps[2] + exps[3]

    ri = lax.broadcasted_iota(jnp.int32, (tm, tm), 0)
    ci = lax.broadcasted_iota(jnp.int32, (tm, tm), 1)
    before = jnp.where(ri < ci, 1.0, 0.0).astype(BF16)
    base = base_scr[:, 0:1]
    rank = _dot(cnt.astype(BF16), before) + base
    row = lax.broadcasted_iota(jnp.int32, (2 * TOP_K, tm), 0)
    meta = jnp.zeros((2 * TOP_K, tm), jnp.int32)
    gates = jnp.zeros((2 * TOP_K, tm), F32)
    for kk in range(TOP_K):
        pos = jnp.sum(jnp.where(hots[kk], rank, 0.0), axis=0, keepdims=True).astype(jnp.int32)
        meta = jnp.where(row == kk, idxs[kk].astype(jnp.int32), meta)
        meta = jnp.where(row == TOP_K + kk, pos, meta)
        gates = jnp.where(row == kk, exps[kk] / denom, gates)
    meta_ref[...] = meta
    gate_ref[...] = gates
    total = base + jnp.sum(cnt, axis=1, keepdims=True)
    base_scr[...] = jnp.broadcast_to(total, base_scr.shape)
    cnt_ref[...] = jnp.broadcast_to(total, cnt_ref.shape)


def _proj_ln_router(a_list, w_list, x, ln_g, ln_b, w_r, b_r, tm):
    t = x.shape[0]
    n_in = len(a_list)
    in_specs = [pl.BlockSpec((tm, a.shape[1]), lambda i: (i, 0)) for a in a_list]
    in_specs += [pl.BlockSpec(w.shape, lambda i: (0, 0)) for w in w_list]
    in_specs += [pl.BlockSpec((tm, D_MODEL), lambda i: (i, 0)),
                 pl.BlockSpec((1, D_MODEL), lambda i: (0, 0)),
                 pl.BlockSpec((1, D_MODEL), lambda i: (0, 0)),
                 pl.BlockSpec((N_EXPERTS, D_MODEL), lambda i: (0, 0)),
                 pl.BlockSpec((N_EXPERTS, LANES), lambda i: (0, 0))]
    return pl.pallas_call(
        functools.partial(_proj_ln_router_kernel, n_in),
        grid=(t // tm,),
        in_specs=in_specs,
        out_specs=[pl.BlockSpec((tm, D_MODEL), lambda i: (i, 0)),
                   pl.BlockSpec((tm, D_MODEL), lambda i: (i, 0)),
                   pl.BlockSpec((2 * TOP_K, tm), lambda i: (0, i)),
                   pl.BlockSpec((2 * TOP_K, tm), lambda i: (0, i)),
                   pl.BlockSpec((N_EXPERTS, LANES), lambda i: (0, 0))],
        out_shape=[jax.ShapeDtypeStruct((t, D_MODEL), F32),
                   jax.ShapeDtypeStruct((t, D_MODEL), BF16),
                   jax.ShapeDtypeStruct((2 * TOP_K, t), jnp.int32),
                   jax.ShapeDtypeStruct((2 * TOP_K, t), F32),
                   jax.ShapeDtypeStruct((N_EXPERTS, LANES), F32)],
        scratch_shapes=[pltpu.VMEM((N_EXPERTS, LANES), F32)],
        compiler_params=_params("arbitrary"),
        name="proj_ln_router",
    )(*a_list, *w_list, x, ln_g, ln_b, w_r, b_r)


def _moe_kernel(blk_ref, exp_ref, lo_ref, hi_ref, x_ref, wgu_ref, wd_ref, bg_ref, bu_ref, bd_ref, o_ref,
                wg_scr, wu_scr, wd_scr):
    i = pl.program_id(0)
    tm = x_ref.shape[0]
    blk = blk_ref[i]
    lo = lo_ref[i]
    hi = hi_ref[i]
    prev = jnp.maximum(i - 1, 0)
    first = jnp.logical_or(i == 0, blk != blk_ref[prev])
    new_expert = jnp.logical_or(i == 0, exp_ref[i] != exp_ref[prev])

    @pl.when(new_expert)
    def _():
        pc = MOE_PREP_COLS
        half = pc // 2
        for c in range(2 * D_FF // pc):
            for kc in range(D_MODEL // LANES):
                k0 = kc * LANES
                tb = wgu_ref[0, 0, k0:k0 + LANES, c * pc:(c + 1) * pc].astype(BF16).T
                pair = pltpu.bitcast(tb, jnp.uint32)
                gate = pltpu.bitcast(jnp.left_shift(pair, 16), F32)
                up = pltpu.bitcast(jnp.bitwise_and(pair, jnp.uint32(0xFFFF0000)), F32)
                wg_scr[c * half:(c + 1) * half, k0:k0 + LANES] = gate.astype(BF16)
                wu_scr[c * half:(c + 1) * half, k0:k0 + LANES] = up.astype(BF16)
        wd_scr[...] = wd_ref[0, 0].astype(BF16)

    @pl.when(first)
    def _():
        o_ref[...] = jnp.zeros(o_ref.shape, o_ref.dtype)

    @pl.when(hi > lo)
    def _():
        x = x_ref[...]
        glu = jnp.minimum(_dot_nt(x, wg_scr[...]) + bg_ref[0, 0], SWIGLU_LIMIT)
        lin = jnp.clip(_dot_nt(x, wu_scr[...]) + bu_ref[0, 0], -SWIGLU_LIMIT, SWIGLU_LIMIT)
        act = glu * _sigmoid(SWIGLU_ALPHA * glu) * (lin + 1.0)
        y = _dot(act.astype(BF16), wd_scr[...]) + bd_ref[0, 0]
        row = blk * tm + lax.broadcasted_iota(jnp.int32, (tm, 1), 0)
        keep = jnp.logical_and(row >= lo, row < hi)
        o_ref[...] = jnp.where(keep, y.astype(o_ref.dtype), o_ref[...])


def _moe_sorted(layer, xs, counts, w_gate_up, b_gate, b_up, w_down, b_down):
    n_rows = xs.shape[0]
    tm = _row_tile(n_rows, MOE_ROWS)
    items = _moe_items(counts, n_rows, tm)
    n_items = items[0].shape[0]

    def by_expert(i, blk, ex, lo, hi):
        return (layer, ex[i], 0, 0)

    def by_block(i, blk, ex, lo, hi):
        return (blk[i], 0)

    grid_spec = pltpu.PrefetchScalarGridSpec(
        num_scalar_prefetch=4,
        grid=(n_items,),
        in_specs=[pl.BlockSpec((tm, D_MODEL), by_block),
                  pl.BlockSpec((1, 1, D_MODEL, 2 * D_FF), by_expert),
                  pl.BlockSpec((1, 1, D_FF, D_MODEL), by_expert),
                  pl.BlockSpec((1, 1, 1, D_FF), by_expert),
                  pl.BlockSpec((1, 1, 1, D_FF), by_expert),
                  pl.BlockSpec((1, 1, 1, D_MODEL), by_expert)],
        out_specs=pl.BlockSpec((tm, D_MODEL), by_block),
        scratch_shapes=[pltpu.VMEM((D_FF, D_MODEL), BF16),
                        pltpu.VMEM((D_FF, D_MODEL), BF16),
                        pltpu.VMEM((D_FF, D_MODEL), BF16)],
    )
    return pl.pallas_call(
        _moe_kernel,
        grid_spec=grid_spec,
        out_shape=jax.ShapeDtypeStruct((n_rows, D_MODEL), BF16),
        compiler_params=_params("arbitrary", vmem=MOE_VMEM_LIMIT),
        name="moe_ffn",
    )(*items, xs, w_gate_up, w_down, b_gate, b_up, b_down)


def _moe_items(counts, n_rows, tm):
    n_blocks = n_rows // tm
    start = jnp.cumsum(counts) - counts
    cuts = jnp.sort(jnp.concatenate([jnp.arange(n_blocks, dtype=jnp.int32) * tm, start.astype(jnp.int32)]))
    lo = cuts
    hi = jnp.concatenate([cuts[1:], jnp.array([n_rows], jnp.int32)])
    blk = jnp.minimum(lo // tm, n_blocks - 1)
    ex = jnp.clip(jnp.sum((start[None, :] <= lo[:, None]).astype(jnp.int32), axis=1) - 1, 0, N_EXPERTS - 1)
    return blk.astype(jnp.int32), ex.astype(jnp.int32), lo.astype(jnp.int32), hi.astype(jnp.int32)


def _combine_ln_kernel(x_ref, y_ref, gate_ref, g_ref, b_ref, o_ref):
    gates = gate_ref[...]
    ffn = y_ref[0].astype(F32) * gates[:, 0:1]
    for kk in range(1, TOP_K):
        ffn = ffn + y_ref[kk].astype(F32) * gates[:, kk:kk + 1]
    o_ref[...] = _layer_norm_rows(DEEPNORM_ALPHA * x_ref[...] + ffn, g_ref[...], b_ref[...])


def _combine_ln(x1, y4, gates, ln_g, ln_b, tm):
    t = x1.shape[0]
    return pl.pallas_call(
        _combine_ln_kernel,
        grid=(t // tm,),
        in_specs=[pl.BlockSpec((tm, D_MODEL), lambda i: (i, 0)),
                  pl.BlockSpec((TOP_K, tm, D_MODEL), lambda i: (0, i, 0)),
                  pl.BlockSpec((tm, 2 * TOP_K), lambda i: (i, 0)),
                  pl.BlockSpec((1, D_MODEL), lambda i: (0, 0)),
                  pl.BlockSpec((1, D_MODEL), lambda i: (0, 0))],
        out_specs=pl.BlockSpec((tm, D_MODEL), lambda i: (i, 0)),
        out_shape=jax.ShapeDtypeStruct((t, D_MODEL), F32),
        compiler_params=_params("parallel"),
        name="combine_ln",
    )(x1, y4, gates, ln_g, ln_b)


def _moe_layer(layer, x1, x1b, meta_t, gates_t, counts, ln_g, ln_b, w_gate_up, b_gate, b_up, w_down, b_down):
    t = x1.shape[0]
    assert N_EXPERTS * t < 2 ** 31
    e = meta_t[0:TOP_K]
    pos = meta_t[TOP_K:2 * TOP_K]
    cnt = counts[:, 0].astype(jnp.int32)
    start = jnp.cumsum(cnt) - cnt
    experts = jnp.arange(N_EXPERTS, dtype=jnp.int32)
    dest = pos + jnp.sum(jnp.where(e[:, :, None] == experts, start, 0), axis=-1)
    keys = e * t + jnp.arange(t, dtype=jnp.int32)[None, :]
    tok_sorted = jnp.remainder(jnp.sort(keys.reshape(-1)), t)
    xs = x1b[tok_sorted]
    ys = _moe_sorted(layer, xs, cnt, w_gate_up, b_gate, b_up, w_down, b_down)
    y4 = ys[dest]
    return _combine_ln(x1, y4, gates_t.T, ln_g, ln_b, _row_tile(t, ROWS))


def kernel(x, even_w_in, even_sgu_ln_g, even_sgu_ln_b, even_sgu_w, even_sgu_b, even_conv_w, even_a_log, even_dt_bias, even_gdn_norm, even_w_out, odd_w_in, odd_q_norm, odd_w_uq, odd_kv_norm, odd_w_ukv, odd_w_out, ln_mix_g, ln_mix_b, ln_ffn_g, ln_ffn_b, moe_w_router, moe_b_router, moe_w_gate_up, moe_b_gate_up, moe_w_down, moe_b_down):
    bsz, seq, d = x.shape
    t = bsz * seq
    xt = x.reshape(t, d)
    tm = _row_tile(t, ROWS)
    b_gate = moe_b_gate_up[:, :, None, 0::2]
    b_up = moe_b_gate_up[:, :, None, 1::2]
    b_down = moe_b_down[:, :, None, :]

    inv_freq = ROPE_THETA ** (-jnp.arange(0, C_ROPE, 2, dtype=F32) / C_ROPE)
    ang = jnp.arange(seq, dtype=F32)[:, None] * inv_freq[None, :]
    cs = jnp.concatenate([jnp.cos(ang), jnp.cos(ang), jnp.sin(ang), jnp.sin(ang)], axis=1)

    pos_chunk = np.arange(A_BLOCK) // CHUNK
    allowed = jnp.asarray(pos_chunk[None, :] <= pos_chunk[:, None])

    def rot_cols(w):
        half = C_ROPE // 2
        return jnp.concatenate([-w[..., half:], w[..., :half]], axis=-1)

    for layer in range(DEPTH):
        i = layer // 2
        if layer % 2 == 0:
            w_in = even_w_in[i]
            w_all = jnp.pad(w_in, ((0, 0), (0, LANES - 2 * B_HEADS))).astype(BF16)
            sw = jnp.where(allowed[None], even_sgu_w[i], 0).astype(BF16)
            sb = jnp.broadcast_to(even_sgu_b[i][:, :, None], (A_GROUPS, A_BLOCK, A_GROUP_DIM)).astype(F32)
            y_a, h_b, ab = _even_in(xt, w_all, even_sgu_ln_g[i][None], even_sgu_ln_b[i][None], sw, sb, tm)
            abt = ab.reshape(bsz, seq, LANES)[:, :, 0:SUBLANES].transpose(0, 2, 1)
            hp = jnp.pad(jnp.stack([even_a_log[i], even_dt_bias[i]]), ((0, 0), (0, LANES - B_HEADS)))
            hpt = jnp.pad(jnp.stack([even_a_log[i], even_dt_bias[i]], axis=1), ((0, SUBLANES - B_HEADS), (0, 0)))
            y_b = _gdn(h_b, ab, abt, even_conv_w[i], hp, hpt, even_gdn_norm[i][None], bsz, seq)
            w_out = even_w_out[i].astype(BF16)
            a_list = [y_a, y_b]
            w_list = [w_out[0:A_WIDTH], w_out[A_WIDTH:]]
        else:
            w_in = odd_w_in[i]
            w_pe = w_in[:, Q_LORA + KV_LORA:]
            w_in2 = jnp.concatenate([w_in, rot_cols(w_pe)], axis=1).astype(BF16)
            w_uq = odd_w_uq[i].reshape(Q_LORA, C_HEADS, C_QK)
            w_uq2 = jnp.concatenate([w_uq, rot_cols(w_uq[..., C_NOPE:])], axis=-1)
            w_uq2 = w_uq2.reshape(Q_LORA, C_HEADS * (C_NOPE + 2 * C_ROPE)).astype(BF16)
            q, k, v = _mla_up(xt, w_in2, odd_q_norm[i][None], odd_kv_norm[i][None], w_uq2,
                              odd_w_ukv[i].astype(BF16), cs, bsz, seq, _row_tile(seq, ROWS))
            o = _attn(q, k, v)
            a_list = [o.reshape(t, C_HEADS * C_V)]
            w_list = [odd_w_out[i].astype(BF16)]

        w_r = moe_w_router[layer].T.astype(BF16)
        b_r = jnp.broadcast_to(moe_b_router[layer][:, None], (N_EXPERTS, LANES))
        x1, x1b, meta, gates, counts = _proj_ln_router(a_list, w_list, xt, ln_mix_g[layer][None],
                                                       ln_mix_b[layer][None], w_r, b_r, tm)
        xt = _moe_layer(layer, x1, x1b, meta, gates, counts, ln_ffn_g[layer][None], ln_ffn_b[layer][None],
                        moe_w_gate_up, b_gate, b_up, moe_w_down, b_down)
    return xt.reshape(bsz, seq, d)
```

```python
import functools

import jax
import jax.numpy as jnp
import numpy as np
from jax import lax
from jax.experimental import pallas as pl
from jax.experimental.pallas import tpu as pltpu

F32 = jnp.float32
BF16 = jnp.bfloat16

D_MODEL = 1024
DEPTH = 4
CHUNK = 64
CHUNK_SHIFT = CHUNK.bit_length() - 1
DEEPNORM_ALPHA = (2 * DEPTH) ** 0.25
LN_EPS = 1e-5
RMS_EPS = 1e-6

A_BLOCK = 128
A_GROUPS = 4
A_WIDTH = D_MODEL // 2
A_GROUP_DIM = A_WIDTH // A_GROUPS

B_HEADS = 4
B_HEAD_DIM = D_MODEL // 8
B_WIDTH = B_HEADS * B_HEAD_DIM
CONV_K = 4

C_HEADS = 8
C_NOPE = 128
C_ROPE = 64
C_V = D_MODEL // C_HEADS
C_QK = C_NOPE + C_ROPE
Q_LORA = 3 * D_MODEL // 8
KV_LORA = D_MODEL // 4
ROPE_THETA = 10000.0

N_EXPERTS = 32
TOP_K = 4
D_FF = D_MODEL
SWIGLU_LIMIT = 7.0
SWIGLU_ALPHA = 1.702

LANES = 128
SUBLANES = 8
VMEM_LIMIT = 48 * 1024 * 1024
MOE_VMEM_LIMIT = 58 * 1024 * 1024
NEG_BIG = -1e30
LOG2_E = 1.4426950408889634

ROWS = 512
GDN_ROWS = 256
INV_BASE_SHIFT = 1
ATTN_ROWS = 512
ATTN_HEADS_PER_STEP = 4
ATTN_SUB = 128
MOE_ROWS = 512
MOE_PART = 256
MOE_PREP_COLS = 512


def _row_tile(t, pref):
    tm = pref
    while t % tm:
        tm //= 2
    return tm


def _params(*sem, vmem=VMEM_LIMIT):
    return pltpu.CompilerParams(dimension_semantics=sem, vmem_limit_bytes=vmem)


def _dot(a, b):
    return jnp.dot(a, b, preferred_element_type=F32)


def _dot_nt(a, b):
    return lax.dot_general(a, b, (((1,), (1,)), ((), ())), preferred_element_type=F32)


def _dot_tn(a, b):
    return lax.dot_general(a, b, (((0,), (0,)), ((), ())), preferred_element_type=F32)


def _sigmoid(x):
    return 1.0 / (1.0 + jnp.exp(-x))


def _softplus(x):
    return jnp.maximum(x, 0.0) + jnp.log1p(jnp.exp(-jnp.abs(x)))


def _layer_norm_rows(y, g, b):
    mu = jnp.mean(y, axis=-1, keepdims=True)
    d = y - mu
    var = jnp.mean(d * d, axis=-1, keepdims=True)
    return d * lax.rsqrt(var + LN_EPS) * g + b


def _gelu(x):
    return 0.5 * x * (1.0 + lax.erf(x * (2.0 ** -0.5)))


def _even_in_kernel(x_ref, w_ref, g_ref, b_ref, sw_ref, sb_ref, o_ref, hb_ref, ab_ref):
    tm = x_ref.shape[0]
    h = _dot(x_ref[...].astype(BF16), w_ref[...])
    n_b = hb_ref.shape[1]
    hb_ref[...] = h[:, 2 * A_WIDTH:2 * A_WIDTH + n_b].astype(hb_ref.dtype)
    ab_ref[...] = h[:, 2 * A_WIDTH + n_b:]
    for grp in range(A_GROUPS):
        c0 = grp * A_GROUP_DIM
        u = _gelu(h[:, c0:c0 + A_GROUP_DIM])
        v = _gelu(h[:, A_WIDTH + c0:A_WIDTH + c0 + A_GROUP_DIM])
        vn = _layer_norm_rows(v, g_ref[:, c0:c0 + A_GROUP_DIM], b_ref[:, c0:c0 + A_GROUP_DIM]).astype(BF16)
        for blk in range(tm // A_BLOCK):
            r0 = blk * A_BLOCK
            mixed = _dot(sw_ref[grp], vn[r0:r0 + A_BLOCK]) + sb_ref[grp]
            o_ref[r0:r0 + A_BLOCK, c0:c0 + A_GROUP_DIM] = (u[r0:r0 + A_BLOCK] * mixed).astype(o_ref.dtype)


def _even_in(x, w_all, ln_g, ln_b, sw, sb, tm):
    t = x.shape[0]
    n_all = w_all.shape[1]
    return pl.pallas_call(
        _even_in_kernel,
        grid=(t // tm,),
        in_specs=[pl.BlockSpec((tm, D_MODEL), lambda i: (i, 0)),
                  pl.BlockSpec((D_MODEL, n_all), lambda i: (0, 0)),
                  pl.BlockSpec((1, A_WIDTH), lambda i: (0, 0)),
                  pl.BlockSpec((1, A_WIDTH), lambda i: (0, 0)),
                  pl.BlockSpec((A_GROUPS, A_BLOCK, A_BLOCK), lambda i: (0, 0, 0)),
                  pl.BlockSpec((A_GROUPS, A_BLOCK, A_GROUP_DIM), lambda i: (0, 0, 0))],
        out_specs=[pl.BlockSpec((tm, A_WIDTH), lambda i: (i, 0)),
                   pl.BlockSpec((tm, 4 * B_WIDTH), lambda i: (i, 0)),
                   pl.BlockSpec((tm, LANES), lambda i: (i, 0))],
        out_shape=[jax.ShapeDtypeStruct((t, A_WIDTH), BF16),
                   jax.ShapeDtypeStruct((t, 4 * B_WIDTH), BF16),
                   jax.ShapeDtypeStruct((t, LANES), F32)],
        compiler_params=_params("parallel"),
        name="even_in",
    )(x, w_all, ln_g, ln_b, sw, sb)


def _gdn_kernel(h_ref, ab_ref, abt_ref, cw_ref, hp_ref, hpt_ref, nw_ref, o_ref, conv_scr, state_scr):
    tr = h_ref.shape[0]
    n_chunks = tr // CHUNK
    s_idx = pl.program_id(1)

    @pl.when(s_idx == 0)
    def _():
        conv_scr[0:SUBLANES, :] = jnp.zeros((SUBLANES, 3 * B_WIDTH), F32)
        state_scr[...] = jnp.zeros(state_scr.shape, F32)

    conv_scr[SUBLANES:SUBLANES + tr, :] = h_ref[:, 0:3 * B_WIDTH].astype(F32)
    acc = conv_scr[SUBLANES:SUBLANES + tr, :] * cw_ref[CONV_K - 1:CONV_K, :]
    for d in range(1, CONV_K):
        acc = acc + conv_scr[SUBLANES - d:SUBLANES - d + tr, :] * cw_ref[CONV_K - 1 - d:CONV_K - d, :]
    conv_scr[0:SUBLANES, :] = conv_scr[tr:tr + SUBLANES, :]
    qkv = acc * _sigmoid(acc)

    ab = ab_ref[...]
    sp_c = _softplus(ab + hp_ref[1:2, :])
    g_col = -jnp.exp(hp_ref[0:1, :]) * sp_c
    beta_col = _sigmoid(ab)
    abt = abt_ref[0]
    sp_r = _softplus(abt + hpt_ref[:, 1:2])
    g_row = -jnp.exp(hpt_ref[:, 0:1]) * sp_r

    ri = lax.broadcasted_iota(jnp.int32, (tr, tr), 0)
    ci = lax.broadcasted_iota(jnp.int32, (tr, tr), 1)
    same = jnp.right_shift(ri, CHUNK_SHIFT) == jnp.right_shift(ci, CHUNK_SHIFT)
    lower = jnp.where(same & (ri >= ci), 1.0, 0.0).astype(F32)
    upper = jnp.where(same & (ri <= ci), 1.0, 0.0).astype(F32)
    gc_col = jnp.dot(lower, g_col, preferred_element_type=F32, precision=lax.Precision.HIGHEST)
    gc_row = jnp.dot(g_row, upper, preferred_element_type=F32, precision=lax.Precision.HIGHEST)

    incl = same & (ri >= ci)
    strict = same & (ri > ci)
    eye = jnp.where(ri == ci, 1.0, 0.0).astype(F32)

    heads = range(B_HEADS)
    qs, ks, vs, betas, gccs, decays, k_bs, kbs, ps, ts = [], [], [], [], [], [], [], [], [], []
    for hd in heads:
        l0 = hd * B_HEAD_DIM
        q = qkv[:, l0:l0 + B_HEAD_DIM]
        k = qkv[:, B_WIDTH + l0:B_WIDTH + l0 + B_HEAD_DIM]
        qs.append(q * lax.rsqrt(jnp.sum(q * q, axis=-1, keepdims=True) + RMS_EPS) * (B_HEAD_DIM ** -0.5))
        ks.append(k * lax.rsqrt(jnp.sum(k * k, axis=-1, keepdims=True) + RMS_EPS))
        vs.append(qkv[:, 2 * B_WIDTH + l0:2 * B_WIDTH + l0 + B_HEAD_DIM])
        betas.append(beta_col[:, B_HEADS + hd:B_HEADS + hd + 1])
        gccs.append(gc_col[:, hd:hd + 1])
        gcr = gc_row[hd:hd + 1, :]
        decays.append(jnp.where(incl, jnp.exp(jnp.where(incl, gccs[hd] - gcr, 0.0)), 0.0))
        k_bs.append(ks[hd].astype(BF16))
        kbs.append(ks[hd] * betas[hd])
    lvl = [jnp.right_shift(ri, s) == jnp.right_shift(ci, s) for s in range(INV_BASE_SHIFT, CHUNK_SHIFT + 1)]
    a_mats = []
    for hd in heads:
        a_mat = jnp.where(strict, _dot_nt(kbs[hd].astype(BF16), k_bs[hd]) * decays[hd], 0.0)
        a_mats.append(a_mat)
        d_mat = jnp.where(lvl[0], a_mat, 0.0)
        ps.append(-d_mat)
        ts.append(eye - d_mat)
    for _ in range(INV_BASE_SHIFT - 1):
        for hd in heads:
            pb = ps[hd].astype(BF16)
            ps[hd] = _dot(pb, pb)
        for hd in heads:
            ts[hd] = ts[hd] + _dot(ts[hd].astype(BF16), ps[hd].astype(BF16))
    for j in range(1, len(lvl)):
        for hd in heads:
            join = jnp.where(lvl[j] & jnp.logical_not(lvl[j - 1]), a_mats[hd], 0.0).astype(BF16)
            t_b = ts[hd].astype(BF16)
            ps[hd] = _dot(_dot(t_b, join).astype(BF16), t_b)
        for hd in heads:
            ts[hd] = ts[hd] - ps[hd]
    us, ws, intras, q_decs = [], [], [], []
    for hd in heads:
        t_b = ts[hd].astype(BF16)
        e_gc = jnp.exp(gccs[hd])
        us.append(_dot(t_b, (vs[hd] * betas[hd]).astype(BF16)))
        ws.append(_dot(t_b, (kbs[hd] * e_gc).astype(BF16)).astype(BF16))
        intras.append(jnp.where(incl, _dot_nt(qs[hd].astype(BF16), k_bs[hd]) * decays[hd], 0.0).astype(BF16))
        q_decs.append((qs[hd] * e_gc).astype(BF16))
    for c in range(n_chunks):
        r0 = c * CHUNK
        for hd in heads:
            l0 = hd * B_HEAD_DIM
            g_last = gccs[hd][r0 + CHUNK - 1:r0 + CHUNK, :]
            k_dec = (ks[hd][r0:r0 + CHUNK] * jnp.exp(g_last - gccs[hd][r0:r0 + CHUNK])).astype(BF16)
            state = state_scr[hd]
            state_b = state.astype(BF16)
            v_new = (us[hd][r0:r0 + CHUNK] - _dot(ws[hd][r0:r0 + CHUNK], state_b)).astype(BF16)
            out = _dot(q_decs[hd][r0:r0 + CHUNK], state_b) + _dot(intras[hd][r0:r0 + CHUNK, r0:r0 + CHUNK], v_new)
            state_scr[hd] = state * jnp.exp(g_last) + _dot_tn(k_dec, v_new)
            z = h_ref[r0:r0 + CHUNK, 3 * B_WIDTH + l0:3 * B_WIDTH + l0 + B_HEAD_DIM].astype(F32)
            o_n = out * lax.rsqrt(jnp.mean(out * out, axis=-1, keepdims=True) + RMS_EPS) * nw_ref[...]
            o_ref[r0:r0 + CHUNK, l0:l0 + B_HEAD_DIM] = (o_n * (z * _sigmoid(z))).astype(o_ref.dtype)


def _gdn(h_b, ab, abt, conv_w, hp, hpt, norm_w, bsz, seq):
    tr = _row_tile(seq, GDN_ROWS)
    n_s = seq // tr
    return pl.pallas_call(
        _gdn_kernel,
        grid=(bsz, n_s),
        in_specs=[pl.BlockSpec((tr, 4 * B_WIDTH), lambda b, s: (b * n_s + s, 0)),
                  pl.BlockSpec((tr, LANES), lambda b, s: (b * n_s + s, 0)),
                  pl.BlockSpec((1, SUBLANES, tr), lambda b, s: (b, 0, s)),
                  pl.BlockSpec((CONV_K, 3 * B_WIDTH), lambda b, s: (0, 0)),
                  pl.BlockSpec((2, LANES), lambda b, s: (0, 0)),
                  pl.BlockSpec((SUBLANES, 2), lambda b, s: (0, 0)),
                  pl.BlockSpec((1, B_HEAD_DIM), lambda b, s: (0, 0))],
        out_specs=pl.BlockSpec((tr, B_WIDTH), lambda b, s: (b * n_s + s, 0)),
        out_shape=jax.ShapeDtypeStruct((bsz * seq, B_WIDTH), BF16),
        scratch_shapes=[pltpu.VMEM((tr + 2 * SUBLANES, 3 * B_WIDTH), F32),
                        pltpu.VMEM((B_HEADS, B_HEAD_DIM, B_HEAD_DIM), F32)],
        compiler_params=_params("parallel", "arbitrary"),
        name="gdn",
    )(h_b, ab, abt, conv_w, hp, hpt, norm_w)


def _mla_up_kernel(x_ref, win_ref, qn_ref, kvn_ref, wq_ref, wkv_ref, cs_ref, q_ref, k_ref, v_ref):
    h = _dot(x_ref[...].astype(BF16), win_ref[...])
    cs = cs_ref[...]

    def rms(x, g):
        return x * lax.rsqrt(jnp.mean(x * x, axis=-1, keepdims=True) + RMS_EPS) * g

    def rope(t):
        r = t * cs
        return r + pltpu.roll(r, C_ROPE, 1)

    cq = rms(h[:, 0:Q_LORA], qn_ref[...]).astype(BF16)
    ckv = rms(h[:, Q_LORA:Q_LORA + KV_LORA], kvn_ref[...]).astype(BF16)
    k_pe = rope(h[:, Q_LORA + KV_LORA:Q_LORA + KV_LORA + 2 * C_ROPE])[:, 0:C_ROPE].astype(BF16)
    qf = _dot(cq, wq_ref[...]) * (C_QK ** -0.5 * LOG2_E)
    kv = _dot(ckv, wkv_ref[...])
    for hd in range(C_HEADS):
        q0 = hd * (C_NOPE + 2 * C_ROPE)
        q_ref[0, hd, :, 0:C_NOPE] = qf[:, q0:q0 + C_NOPE].astype(BF16)
        q_ref[0, hd, :, C_NOPE:C_QK] = rope(qf[:, q0 + C_NOPE:q0 + C_NOPE + 2 * C_ROPE])[:, 0:C_ROPE].astype(BF16)
        k0 = hd * (C_NOPE + C_V)
        k_ref[0, hd, :, 0:C_NOPE] = kv[:, k0:k0 + C_NOPE].astype(BF16)
        k_ref[0, hd, :, C_NOPE:C_QK] = k_pe
        v_ref[0, hd, :, :] = kv[:, k0 + C_NOPE:k0 + C_NOPE + C_V].astype(BF16)


def _mla_up(x, w_in, q_norm, kv_norm, w_uq, w_ukv, cs, bsz, seq, tm):
    n_s = seq // tm
    qk_shape = jax.ShapeDtypeStruct((bsz, C_HEADS, seq, C_QK), BF16)
    return pl.pallas_call(
        _mla_up_kernel,
        grid=(bsz, n_s),
        in_specs=[pl.BlockSpec((tm, D_MODEL), lambda b, s: (b * n_s + s, 0)),
                  pl.BlockSpec(w_in.shape, lambda b, s: (0, 0)),
                  pl.BlockSpec((1, Q_LORA), lambda b, s: (0, 0)),
                  pl.BlockSpec((1, KV_LORA), lambda b, s: (0, 0)),
                  pl.BlockSpec(w_uq.shape, lambda b, s: (0, 0)),
                  pl.BlockSpec(w_ukv.shape, lambda b, s: (0, 0)),
                  pl.BlockSpec((tm, LANES), lambda b, s: (s, 0))],
        out_specs=[pl.BlockSpec((1, C_HEADS, tm, C_QK), lambda b, s: (b, 0, s, 0)),
                   pl.BlockSpec((1, C_HEADS, tm, C_QK), lambda b, s: (b, 0, s, 0)),
                   pl.BlockSpec((1, C_HEADS, tm, C_V), lambda b, s: (b, 0, s, 0))],
        out_shape=[qk_shape, qk_shape, jax.ShapeDtypeStruct((bsz, C_HEADS, seq, C_V), BF16)],
        compiler_params=_params("parallel", "parallel"),
        name="mla_up",
    )(x, w_in, q_norm, kv_norm, w_uq, w_ukv, cs)


def _attn_kernel(q_ref, k_ref, v_ref, o_ref, s_scr, p_scr, m_scr, l_scr, a_scr, acc_scr):
    nh, tq = q_ref.shape[1], q_ref.shape[2]
    n_lt = tq // LANES
    qi = pl.program_id(2)

    def scores(hd, k0):
        s_scr[hd] = _dot_nt(q_ref[0, hd], k_ref[0, hd, pl.ds(k0, tq), :])

    def softmax(hd, first):
        for r0 in range(0, tq, ATTN_SUB):
            rows = slice(r0, r0 + ATTN_SUB)
            n_vis = -(-(r0 + ATTN_SUB) // LANES) if first else n_lt
            tiles = []
            for c in range(n_vis):
                s = s_scr[hd, rows, c * LANES:(c + 1) * LANES]
                if first and ((c + 1) * LANES - 1) // CHUNK > r0 // CHUNK:
                    ri = jnp.right_shift(r0 + lax.broadcasted_iota(jnp.int32, (ATTN_SUB, LANES), 0), CHUNK_SHIFT)
                    ci = jnp.right_shift(c * LANES + lax.broadcasted_iota(jnp.int32, (ATTN_SUB, LANES), 1),
                                         CHUNK_SHIFT)
                    s = jnp.where(ci <= ri, s, NEG_BIG)
                tiles.append(s)
            mx = tiles[0]
            for c in range(1, n_vis):
                mx = jnp.maximum(mx, tiles[c])
            m_new = jnp.broadcast_to(jnp.max(mx, axis=-1, keepdims=True), (ATTN_SUB, LANES))
            if not first:
                m_prev = m_scr[hd, rows, :]
                m_new = jnp.maximum(m_prev, m_new)
                alpha = jnp.exp2(m_prev - m_new)
                a_scr[hd, rows, :] = alpha
            psum = jnp.zeros((ATTN_SUB, LANES), F32)
            for c in range(n_vis):
                p = jnp.exp2(tiles[c] - m_new)
                psum = psum + p
                p_scr[hd, rows, c * LANES:(c + 1) * LANES] = p.astype(BF16)
            for c in range(n_vis, n_lt):
                p_scr[hd, rows, c * LANES:(c + 1) * LANES] = jnp.zeros((ATTN_SUB, LANES), BF16)
            m_scr[hd, rows, :] = m_new
            l_new = jnp.broadcast_to(jnp.sum(psum, axis=-1, keepdims=True), (ATTN_SUB, LANES))
            l_scr[hd, rows, :] = l_new if first else alpha * l_scr[hd, rows, :] + l_new

    def values(hd, k0, first):
        pv = _dot(p_scr[hd], v_ref[0, hd, pl.ds(k0, tq), :])
        acc_scr[hd] = pv if first else a_scr[hd] * acc_scr[hd] + pv

    def block(k0, first):
        for hd in range(nh):
            scores(hd, k0)
        for hd in range(nh):
            softmax(hd, first)
        for hd in range(nh):
            values(hd, k0, first)

    def body(j, carry):
        block(pl.multiple_of(j * tq, tq), False)
        return carry

    block(pl.multiple_of(qi * tq, tq), True)
    lax.fori_loop(0, qi, body, 0)
    for hd in range(nh):
        o_ref[0, :, hd * C_V:(hd + 1) * C_V] = (acc_scr[hd] / l_scr[hd]).astype(o_ref.dtype)


def _attn(q, k, v):
    bsz, _, seq, _ = q.shape
    tq = _row_tile(seq, ATTN_ROWS)
    nh = ATTN_HEADS_PER_STEP
    return pl.pallas_call(
        _attn_kernel,
        grid=(bsz, C_HEADS // nh, seq // tq),
        in_specs=[pl.BlockSpec((1, nh, tq, C_QK), lambda b, h, i: (b, h, i, 0)),
                  pl.BlockSpec((1, nh, seq, C_QK), lambda b, h, i: (b, h, 0, 0)),
                  pl.BlockSpec((1, nh, seq, C_V), lambda b, h, i: (b, h, 0, 0))],
        out_specs=pl.BlockSpec((1, tq, nh * C_V), lambda b, h, i: (b, i, h)),
        out_shape=jax.ShapeDtypeStruct((bsz, seq, C_HEADS * C_V), BF16),
        scratch_shapes=[pltpu.VMEM((nh, tq, tq), F32),
                        pltpu.VMEM((nh, tq, tq), BF16),
                        pltpu.VMEM((nh, tq, LANES), F32),
                        pltpu.VMEM((nh, tq, LANES), F32),
                        pltpu.VMEM((nh, tq, LANES), F32),
                        pltpu.VMEM((nh, tq, C_V), F32)],
        compiler_params=_params("parallel", "parallel", "arbitrary"),
        name="attn",
    )(q, k, v)


def _proj_ln_router_kernel(n_in, *refs):
    a_refs = refs[0:n_in]
    w_refs = refs[n_in:2 * n_in]
    x_ref, g_ref, b_ref, wr_ref, br_ref = refs[2 * n_in:2 * n_in + 5]
    x1_ref, x1b_ref, meta_ref, gate_ref, cnt_ref = refs[2 * n_in + 5:2 * n_in + 10]
    base_scr = refs[2 * n_in + 10]
    tm = x_ref.shape[0]
    i = pl.program_id(0)

    @pl.when(i == 0)
    def _():
        base_scr[...] = jnp.zeros(base_scr.shape, F32)

    mix = _dot(a_refs[0][...], w_refs[0][...])
    for j in range(1, n_in):
        mix = mix + _dot(a_refs[j][...], w_refs[j][...])
    x1 = _layer_norm_rows(DEEPNORM_ALPHA * x_ref[...] + mix, g_ref[...], b_ref[...])
    x1_ref[...] = x1
    x1b = x1.astype(BF16)
    x1b_ref[...] = x1b

    work = _dot_nt(wr_ref[...], x1b) + br_ref[:, 0:1]
    eidx = lax.broadcasted_iota(jnp.int32, (N_EXPERTS, tm), 0).astype(F32)
    cnt = jnp.zeros((N_EXPERTS, tm), F32)
    tops, idxs, hots = [], [], []
    for _ in range(TOP_K):
        mx = jnp.max(work, axis=0, keepdims=True)
        idx = jnp.min(jnp.where(work == mx, eidx, float(N_EXPERTS)), axis=0, keepdims=True)
        hot = eidx == idx
        work = jnp.where(hot, NEG_BIG, work)
        cnt = cnt + jnp.where(hot, 1.0, 0.0)
        tops.append(mx)
        idxs.append(idx)
        hots.append(hot)
    exps = [jnp.exp(t - tops[0]) for t in tops]
    denom = exps[0] + exps[1] + exps[2] + exps[3]

    ri = lax.broadcasted_iota(jnp.int32, (tm, tm), 0)
    ci = lax.broadcasted_iota(jnp.int32, (tm, tm), 1)
    before = jnp.where(ri < ci, 1.0, 0.0).astype(BF16)
    base = base_scr[:, 0:1]
    rank = _dot(cnt.astype(BF16), before) + base
    row = lax.broadcasted_iota(jnp.int32, (2 * TOP_K, tm), 0)
    meta = jnp.zeros((2 * TOP_K, tm), jnp.int32)
    gates = jnp.zeros((2 * TOP_K, tm), F32)
    for kk in range(TOP_K):
        pos = jnp.sum(jnp.where(hots[kk], rank, 0.0), axis=0, keepdims=True).astype(jnp.int32)
        meta = jnp.where(row == kk, idxs[kk].astype(jnp.int32), meta)
        meta = jnp.where(row == TOP_K + kk, pos, meta)
        gates = jnp.where(row == kk, exps[kk] / denom, gates)
    meta_ref[...] = meta
    gate_ref[...] = gates
    total = base + jnp.sum(cnt, axis=1, keepdims=True)
    base_scr[...] = jnp.broadcast_to(total, base_scr.shape)
    cnt_ref[...] = jnp.broadcast_to(total, cnt_ref.shape)


def _proj_ln_router(a_list, w_list, x, ln_g, ln_b, w_r, b_r, tm):
    t = x.shape[0]
    n_in = len(a_list)
    in_specs = [pl.BlockSpec((tm, a.shape[1]), lambda i: (i, 0)) for a in a_list]
    in_specs += [pl.BlockSpec(w.shape, lambda i: (0, 0)) for w in w_list]
    in_specs += [pl.BlockSpec((tm, D_MODEL), lambda i: (i, 0)),
                 pl.BlockSpec((1, D_MODEL), lambda i: (0, 0)),
                 pl.BlockSpec((1, D_MODEL), lambda i: (0, 0)),
                 pl.BlockSpec((N_EXPERTS, D_MODEL), lambda i: (0, 0)),
                 pl.BlockSpec((N_EXPERTS, LANES), lambda i: (0, 0))]
    return pl.pallas_call(
        functools.partial(_proj_ln_router_kernel, n_in),
        grid=(t // tm,),
        in_specs=in_specs,
        out_specs=[pl.BlockSpec((tm, D_MODEL), lambda i: (i, 0)),
                   pl.BlockSpec((tm, D_MODEL), lambda i: (i, 0)),
                   pl.BlockSpec((2 * TOP_K, tm), lambda i: (0, i)),
                   pl.BlockSpec((2 * TOP_K, tm), lambda i: (0, i)),
                   pl.BlockSpec((N_EXPERTS, LANES), lambda i: (0, 0))],
        out_shape=[jax.ShapeDtypeStruct((t, D_MODEL), F32),
                   jax.ShapeDtypeStruct((t, D_MODEL), BF16),
                   jax.ShapeDtypeStruct((2 * TOP_K, t), jnp.int32),
                   jax.ShapeDtypeStruct((2 * TOP_K, t), F32),
                   jax.ShapeDtypeStruct((N_EXPERTS, LANES), F32)],
        scratch_shapes=[pltpu.VMEM((N_EXPERTS, LANES), F32)],
        compiler_params=_params("arbitrary"),
        name="proj_ln_router",
    )(*a_list, *w_list, x, ln_g, ln_b, w_r, b_r)


def _moe_kernel(blk_ref, exp_ref, lo_ref, hi_ref, x_ref, wgu_ref, wd_ref, bg_ref, bu_ref, bd_ref, o_ref,
                wg_scr, wu_scr, wd_scr):
    i = pl.program_id(0)
    tm = x_ref.shape[0]
    blk = blk_ref[i]
    lo = lo_ref[i]
    hi = hi_ref[i]
    prev = jnp.maximum(i - 1, 0)
    first = jnp.logical_or(i == 0, blk != blk_ref[prev])
    new_expert = jnp.logical_or(i == 0, exp_ref[i] != exp_ref[prev])

    @pl.when(new_expert)
    def _():
        pc = MOE_PREP_COLS
        half = pc // 2
        for c in range(2 * D_FF // pc):
            for kc in range(D_MODEL // LANES):
                k0 = kc * LANES
                tb = wgu_ref[0, 0, k0:k0 + LANES, c * pc:(c + 1) * pc].astype(BF16).T
                pair = pltpu.bitcast(tb, jnp.uint32)
                gate = pltpu.bitcast(jnp.left_shift(pair, 16), F32)
                up = pltpu.bitcast(jnp.bitwise_and(pair, jnp.uint32(0xFFFF0000)), F32)
                wg_scr[c * half:(c + 1) * half, k0:k0 + LANES] = gate.astype(BF16)
                wu_scr[c * half:(c + 1) * half, k0:k0 + LANES] = up.astype(BF16)
        wd_scr[...] = wd_ref[0, 0].astype(BF16)

    def ffn(x):
        glu = jnp.minimum(_dot_nt(x, wg_scr[...]) + bg_ref[0, 0], SWIGLU_LIMIT)
        lin = jnp.clip(_dot_nt(x, wu_scr[...]) + bu_ref[0, 0], -SWIGLU_LIMIT, SWIGLU_LIMIT)
        act = glu * _sigmoid(SWIGLU_ALPHA * glu) * (lin + 1.0)
        return (_dot(act.astype(BF16), wd_scr[...]) + bd_ref[0, 0]).astype(o_ref.dtype)

    full = (hi - lo) == tm

    @pl.when(full)
    def _():
        o_ref[...] = ffn(x_ref[...])

    @pl.when(jnp.logical_and(first, jnp.logical_not(full)))
    def _():
        o_ref[...] = jnp.zeros(o_ref.shape, o_ref.dtype)

    for r0 in range(0, tm, MOE_PART):
        row0 = blk * tm + r0
        touched = jnp.logical_and(hi > lo, jnp.logical_and(lo < row0 + MOE_PART, hi > row0))

        @pl.when(jnp.logical_and(jnp.logical_not(full), touched))
        def _():
            y = ffn(x_ref[r0:r0 + MOE_PART, :])
            row = row0 + lax.broadcasted_iota(jnp.int32, (MOE_PART, 1), 0)
            keep = jnp.logical_and(row >= lo, row < hi)
            o_ref[r0:r0 + MOE_PART, :] = jnp.where(keep, y, o_ref[r0:r0 + MOE_PART, :])


def _moe_sorted(layer, xs, counts, w_gate_up, b_gate, b_up, w_down, b_down):
    n_rows = xs.shape[0]
    tm = _row_tile(n_rows, MOE_ROWS)
    items = _moe_items(counts, n_rows, tm)
    n_items = items[0].shape[0]

    def by_expert(i, blk, ex, lo, hi):
        return (layer, ex[i], 0, 0)

    def by_block(i, blk, ex, lo, hi):
        return (blk[i], 0)

    grid_spec = pltpu.PrefetchScalarGridSpec(
        num_scalar_prefetch=4,
        grid=(n_items,),
        in_specs=[pl.BlockSpec((tm, D_MODEL), by_block),
                  pl.BlockSpec((1, 1, D_MODEL, 2 * D_FF), by_expert),
                  pl.BlockSpec((1, 1, D_FF, D_MODEL), by_expert),
                  pl.BlockSpec((1, 1, 1, D_FF), by_expert),
                  pl.BlockSpec((1, 1, 1, D_FF), by_expert),
                  pl.BlockSpec((1, 1, 1, D_MODEL), by_expert)],
        out_specs=pl.BlockSpec((tm, D_MODEL), by_block),
        scratch_shapes=[pltpu.VMEM((D_FF, D_MODEL), BF16),
                        pltpu.VMEM((D_FF, D_MODEL), BF16),
                        pltpu.VMEM((D_FF, D_MODEL), BF16)],
    )
    return pl.pallas_call(
        _moe_kernel,
        grid_spec=grid_spec,
        out_shape=jax.ShapeDtypeStruct((n_rows, D_MODEL), BF16),
        compiler_params=_params("arbitrary", vmem=MOE_VMEM_LIMIT),
        name="moe_ffn",
    )(*items, xs, w_gate_up, w_down, b_gate, b_up, b_down)


def _moe_items(counts, n_rows, tm):
    n_blocks = n_rows // tm
    start = jnp.cumsum(counts) - counts
    cuts = jnp.sort(jnp.concatenate([jnp.arange(n_blocks, dtype=jnp.int32) * tm, start.astype(jnp.int32)]))
    lo = cuts
    hi = jnp.concatenate([cuts[1:], jnp.array([n_rows], jnp.int32)])
    blk = jnp.minimum(lo // tm, n_blocks - 1)
    ex = jnp.clip(jnp.sum((start[None, :] <= lo[:, None]).astype(jnp.int32), axis=1) - 1, 0, N_EXPERTS - 1)
    return blk.astype(jnp.int32), ex.astype(jnp.int32), lo.astype(jnp.int32), hi.astype(jnp.int32)


def _combine_ln_kernel(x_ref, y_ref, gate_ref, g_ref, b_ref, o_ref):
    gates = gate_ref[...]
    ffn = y_ref[0].astype(F32) * gates[:, 0:1]
    for kk in range(1, TOP_K):
        ffn = ffn + y_ref[kk].astype(F32) * gates[:, kk:kk + 1]
    o_ref[...] = _layer_norm_rows(DEEPNORM_ALPHA * x_ref[...] + ffn, g_ref[...], b_ref[...])


def _combine_ln(x1, y4, gates, ln_g, ln_b, tm):
    t = x1.shape[0]
    return pl.pallas_call(
        _combine_ln_kernel,
        grid=(t // tm,),
        in_specs=[pl.BlockSpec((tm, D_MODEL), lambda i: (i, 0)),
                  pl.BlockSpec((TOP_K, tm, D_MODEL), lambda i: (0, i, 0)),
                  pl.BlockSpec((tm, 2 * TOP_K), lambda i: (i, 0)),
                  pl.BlockSpec((1, D_MODEL), lambda i: (0, 0)),
                  pl.BlockSpec((1, D_MODEL), lambda i: (0, 0))],
        out_specs=pl.BlockSpec((tm, D_MODEL), lambda i: (i, 0)),
        out_shape=jax.ShapeDtypeStruct((t, D_MODEL), F32),
        compiler_params=_params("parallel"),
        name="combine_ln",
    )(x1, y4, gates, ln_g, ln_b)


def _moe_layer(layer, x1, x1b, meta_t, gates_t, counts, ln_g, ln_b, w_gate_up, b_gate, b_up, w_down, b_down):
    t = x1.shape[0]
    assert N_EXPERTS * t < 2 ** 31
    e = meta_t[0:TOP_K]
    pos = meta_t[TOP_K:2 * TOP_K]
    cnt = counts[:, 0].astype(jnp.int32)
    start = jnp.cumsum(cnt) - cnt
    experts = jnp.arange(N_EXPERTS, dtype=jnp.int32)
    dest = pos + jnp.sum(jnp.where(e[:, :, None] == experts, start, 0), axis=-1)
    keys = e * t + jnp.arange(t, dtype=jnp.int32)[None, :]
    tok_sorted = jnp.remainder(jnp.sort(keys.reshape(-1)), t)
    xs = x1b[tok_sorted]
    ys = _moe_sorted(layer, xs, cnt, w_gate_up, b_gate, b_up, w_down, b_down)
    y4 = ys[dest]
    return _combine_ln(x1, y4, gates_t.T, ln_g, ln_b, _row_tile(t, ROWS))


def kernel(x, even_w_in, even_sgu_ln_g, even_sgu_ln_b, even_sgu_w, even_sgu_b, even_conv_w, even_a_log, even_dt_bias, even_gdn_norm, even_w_out, odd_w_in, odd_q_norm, odd_w_uq, odd_kv_norm, odd_w_ukv, odd_w_out, ln_mix_g, ln_mix_b, ln_ffn_g, ln_ffn_b, moe_w_router, moe_b_router, moe_w_gate_up, moe_b_gate_up, moe_w_down, moe_b_down):
    bsz, seq, d = x.shape
    t = bsz * seq
    xt = x.reshape(t, d)
    tm = _row_tile(t, ROWS)
    b_gate = moe_b_gate_up[:, :, None, 0::2]
    b_up = moe_b_gate_up[:, :, None, 1::2]
    b_down = moe_b_down[:, :, None, :]

    inv_freq = ROPE_THETA ** (-jnp.arange(0, C_ROPE, 2, dtype=F32) / C_ROPE)
    ang = jnp.arange(seq, dtype=F32)[:, None] * inv_freq[None, :]
    cs = jnp.concatenate([jnp.cos(ang), jnp.cos(ang), jnp.sin(ang), jnp.sin(ang)], axis=1)

    pos_chunk = np.arange(A_BLOCK) // CHUNK
    allowed = jnp.asarray(pos_chunk[None, :] <= pos_chunk[:, None])

    def rot_cols(w):
        half = C_ROPE // 2
        return jnp.concatenate([-w[..., half:], w[..., :half]], axis=-1)

    for layer in range(DEPTH):
        i = layer // 2
        if layer % 2 == 0:
            w_in = even_w_in[i]
            w_all = jnp.pad(w_in, ((0, 0), (0, LANES - 2 * B_HEADS))).astype(BF16)
            sw = jnp.where(allowed[None], even_sgu_w[i], 0).astype(BF16)
            sb = jnp.broadcast_to(even_sgu_b[i][:, :, None], (A_GROUPS, A_BLOCK, A_GROUP_DIM)).astype(F32)
            y_a, h_b, ab = _even_in(xt, w_all, even_sgu_ln_g[i][None], even_sgu_ln_b[i][None], sw, sb, tm)
            abt = ab.reshape(bsz, seq, LANES)[:, :, 0:SUBLANES].transpose(0, 2, 1)
            hp = jnp.pad(jnp.stack([even_a_log[i], even_dt_bias[i]]), ((0, 0), (0, LANES - B_HEADS)))
            hpt = jnp.pad(jnp.stack([even_a_log[i], even_dt_bias[i]], axis=1), ((0, SUBLANES - B_HEADS), (0, 0)))
            y_b = _gdn(h_b, ab, abt, even_conv_w[i], hp, hpt, even_gdn_norm[i][None], bsz, seq)
            w_out = even_w_out[i].astype(BF16)
            a_list = [y_a, y_b]
            w_list = [w_out[0:A_WIDTH], w_out[A_WIDTH:]]
        else:
            w_in = odd_w_in[i]
            w_pe = w_in[:, Q_LORA + KV_LORA:]
            w_in2 = jnp.concatenate([w_in, rot_cols(w_pe)], axis=1).astype(BF16)
            w_uq = odd_w_uq[i].reshape(Q_LORA, C_HEADS, C_QK)
            w_uq2 = jnp.concatenate([w_uq, rot_cols(w_uq[..., C_NOPE:])], axis=-1)
            w_uq2 = w_uq2.reshape(Q_LORA, C_HEADS * (C_NOPE + 2 * C_ROPE)).astype(BF16)
            q, k, v = _mla_up(xt, w_in2, odd_q_norm[i][None], odd_kv_norm[i][None], w_uq2,
                              odd_w_ukv[i].astype(BF16), cs, bsz, seq, _row_tile(seq, ROWS))
            o = _attn(q, k, v)
            a_list = [o.reshape(t, C_HEADS * C_V)]
            w_list = [odd_w_out[i].astype(BF16)]

        w_r = moe_w_router[layer].T.astype(BF16)
        b_r = jnp.broadcast_to(moe_b_router[layer][:, None], (N_EXPERTS, LANES))
        x1, x1b, meta, gates, counts = _proj_ln_router(a_list, w_list, xt, ln_mix_g[layer][None],
                                                       ln_mix_b[layer][None], w_r, b_r, tm)
        xt = _moe_layer(layer, x1, x1b, meta, gates, counts, ln_ffn_g[layer][None], ln_ffn_b[layer][None],
                        moe_w_gate_up, b_gate, b_up, moe_w_down, b_down)
    return xt.reshape(bsz, seq, d)
```

```python
import functools

import jax
import jax.numpy as jnp
import numpy as np
from jax import lax
from jax.experimental import pallas as pl
from jax.experimental.pallas import tpu as pltpu

F32 = jnp.float32
BF16 = jnp.bfloat16

D_MODEL = 1024
DEPTH = 4
CHUNK = 64
CHUNK_SHIFT = CHUNK.bit_length() - 1
DEEPNORM_ALPHA = (2 * DEPTH) ** 0.25
LN_EPS = 1e-5
RMS_EPS = 1e-6

A_BLOCK = 128
A_GROUPS = 4
A_WIDTH = D_MODEL // 2
A_GROUP_DIM = A_WIDTH // A_GROUPS

B_HEADS = 4
B_HEAD_DIM = D_MODEL // 8
B_WIDTH = B_HEADS * B_HEAD_DIM
CONV_K = 4

C_HEADS = 8
C_NOPE = 128
C_ROPE = 64
C_V = D_MODEL // C_HEADS
C_QK = C_NOPE + C_ROPE
Q_LORA = 3 * D_MODEL // 8
KV_LORA = D_MODEL // 4
ROPE_THETA = 10000.0

N_EXPERTS = 32
TOP_K = 4
D_FF = D_MODEL
SWIGLU_LIMIT = 7.0
SWIGLU_ALPHA = 1.702

LANES = 128
SUBLANES = 8
VMEM_LIMIT = 48 * 1024 * 1024
MOE_VMEM_LIMIT = 58 * 1024 * 1024
NEG_BIG = -1e30
LOG2_E = 1.4426950408889634

ROWS = 512
GDN_ROWS = 256
GDN_BATCH = 2
INV_BASE_SHIFT = 1
ATTN_ROWS = 512
ATTN_HEADS_PER_STEP = 4
ATTN_SUB = 128
MOE_ROWS = 512
MOE_PART = 256
MOE_PREP_COLS = 512


def _row_tile(t, pref):
    tm = pref
    while t % tm:
        tm //= 2
    return tm


def _params(*sem, vmem=VMEM_LIMIT):
    return pltpu.CompilerParams(dimension_semantics=sem, vmem_limit_bytes=vmem)


def _dot(a, b):
    return jnp.dot(a, b, preferred_element_type=F32)


def _dot_nt(a, b):
    return lax.dot_general(a, b, (((1,), (1,)), ((), ())), preferred_element_type=F32)


def _dot_tn(a, b):
    return lax.dot_general(a, b, (((0,), (0,)), ((), ())), preferred_element_type=F32)


def _sigmoid(x):
    return 1.0 / (1.0 + jnp.exp(-x))


def _softplus(x):
    return jnp.maximum(x, 0.0) + jnp.log1p(jnp.exp(-jnp.abs(x)))


def _layer_norm_rows(y, g, b):
    mu = jnp.mean(y, axis=-1, keepdims=True)
    d = y - mu
    var = jnp.mean(d * d, axis=-1, keepdims=True)
    return d * lax.rsqrt(var + LN_EPS) * g + b


def _gelu(x):
    return 0.5 * x * (1.0 + lax.erf(x * (2.0 ** -0.5)))


def _even_in_kernel(x_ref, w_ref, g_ref, b_ref, sw_ref, sb_ref, o_ref, hb_ref, ab_ref):
    tm = x_ref.shape[0]
    h = _dot(x_ref[...].astype(BF16), w_ref[...])
    n_b = hb_ref.shape[1]
    hb_ref[...] = h[:, 2 * A_WIDTH:2 * A_WIDTH + n_b].astype(hb_ref.dtype)
    ab_ref[...] = h[:, 2 * A_WIDTH + n_b:]
    for grp in range(A_GROUPS):
        c0 = grp * A_GROUP_DIM
        u = _gelu(h[:, c0:c0 + A_GROUP_DIM])
        v = _gelu(h[:, A_WIDTH + c0:A_WIDTH + c0 + A_GROUP_DIM])
        vn = _layer_norm_rows(v, g_ref[:, c0:c0 + A_GROUP_DIM], b_ref[:, c0:c0 + A_GROUP_DIM]).astype(BF16)
        for blk in range(tm // A_BLOCK):
            r0 = blk * A_BLOCK
            mixed = _dot(sw_ref[grp], vn[r0:r0 + A_BLOCK]) + sb_ref[grp]
            o_ref[r0:r0 + A_BLOCK, c0:c0 + A_GROUP_DIM] = (u[r0:r0 + A_BLOCK] * mixed).astype(o_ref.dtype)


def _even_in(x, w_all, ln_g, ln_b, sw, sb, tm):
    t = x.shape[0]
    n_all = w_all.shape[1]
    return pl.pallas_call(
        _even_in_kernel,
        grid=(t // tm,),
        in_specs=[pl.BlockSpec((tm, D_MODEL), lambda i: (i, 0)),
                  pl.BlockSpec((D_MODEL, n_all), lambda i: (0, 0)),
                  pl.BlockSpec((1, A_WIDTH), lambda i: (0, 0)),
                  pl.BlockSpec((1, A_WIDTH), lambda i: (0, 0)),
                  pl.BlockSpec((A_GROUPS, A_BLOCK, A_BLOCK), lambda i: (0, 0, 0)),
                  pl.BlockSpec((A_GROUPS, A_BLOCK, A_GROUP_DIM), lambda i: (0, 0, 0))],
        out_specs=[pl.BlockSpec((tm, A_WIDTH), lambda i: (i, 0)),
                   pl.BlockSpec((tm, 4 * B_WIDTH), lambda i: (i, 0)),
                   pl.BlockSpec((tm, LANES), lambda i: (i, 0))],
        out_shape=[jax.ShapeDtypeStruct((t, A_WIDTH), BF16),
                   jax.ShapeDtypeStruct((t, 4 * B_WIDTH), BF16),
                   jax.ShapeDtypeStruct((t, LANES), F32)],
        compiler_params=_params("parallel"),
        name="even_in",
    )(x, w_all, ln_g, ln_b, sw, sb)


def _gdn_kernel(h_ref, ab_ref, abt_ref, cw_ref, hp_ref, hpt_ref, nw_ref, o_ref, conv_scr, state_scr):
    nb, tr = h_ref.shape[0], h_ref.shape[1]
    n_chunks = tr // CHUNK
    s_idx = pl.program_id(1)

    @pl.when(s_idx == 0)
    def _():
        conv_scr[:, 0:SUBLANES, :] = jnp.zeros((nb, SUBLANES, 3 * B_WIDTH), F32)
        state_scr[...] = jnp.zeros(state_scr.shape, F32)

    ri = lax.broadcasted_iota(jnp.int32, (tr, tr), 0)
    ci = lax.broadcasted_iota(jnp.int32, (tr, tr), 1)
    same = jnp.right_shift(ri, CHUNK_SHIFT) == jnp.right_shift(ci, CHUNK_SHIFT)
    lower = jnp.where(same & (ri >= ci), 1.0, 0.0).astype(F32)
    upper = jnp.where(same & (ri <= ci), 1.0, 0.0).astype(F32)
    incl = same & (ri >= ci)
    strict = same & (ri > ci)
    eye = jnp.where(ri == ci, 1.0, 0.0).astype(F32)

    chains = [(bi, hd) for bi in range(nb) for hd in range(B_HEADS)]
    qs, ks, vs, betas, gccs, decays, k_bs, kbs, ps, ts, a_mats = {}, {}, {}, {}, {}, {}, {}, {}, {}, {}, {}
    us, ws, intras, q_decs = {}, {}, {}, {}
    for bi in range(nb):
        conv_scr[bi, SUBLANES:SUBLANES + tr, :] = h_ref[bi, :, 0:3 * B_WIDTH].astype(F32)
        acc = conv_scr[bi, SUBLANES:SUBLANES + tr, :] * cw_ref[CONV_K - 1:CONV_K, :]
        for d in range(1, CONV_K):
            acc = acc + conv_scr[bi, SUBLANES - d:SUBLANES - d + tr, :] * cw_ref[CONV_K - 1 - d:CONV_K - d, :]
        conv_scr[bi, 0:SUBLANES, :] = conv_scr[bi, tr:tr + SUBLANES, :]
        qkv = acc * _sigmoid(acc)

        ab = ab_ref[bi]
        sp_c = _softplus(ab + hp_ref[1:2, :])
        g_col = -jnp.exp(hp_ref[0:1, :]) * sp_c
        beta_col = _sigmoid(ab)
        abt = abt_ref[bi]
        sp_r = _softplus(abt + hpt_ref[:, 1:2])
        g_row = -jnp.exp(hpt_ref[:, 0:1]) * sp_r
        gc_col = jnp.dot(lower, g_col, preferred_element_type=F32, precision=lax.Precision.HIGHEST)
        gc_row = jnp.dot(g_row, upper, preferred_element_type=F32, precision=lax.Precision.HIGHEST)
        for hd in range(B_HEADS):
            ch = (bi, hd)
            l0 = hd * B_HEAD_DIM
            q = qkv[:, l0:l0 + B_HEAD_DIM]
            k = qkv[:, B_WIDTH + l0:B_WIDTH + l0 + B_HEAD_DIM]
            qs[ch] = q * lax.rsqrt(jnp.sum(q * q, axis=-1, keepdims=True) + RMS_EPS) * (B_HEAD_DIM ** -0.5)
            ks[ch] = k * lax.rsqrt(jnp.sum(k * k, axis=-1, keepdims=True) + RMS_EPS)
            vs[ch] = qkv[:, 2 * B_WIDTH + l0:2 * B_WIDTH + l0 + B_HEAD_DIM]
            betas[ch] = beta_col[:, B_HEADS + hd:B_HEADS + hd + 1]
            gccs[ch] = gc_col[:, hd:hd + 1]
            gcr = gc_row[hd:hd + 1, :]
            decays[ch] = jnp.where(incl, jnp.exp(jnp.where(incl, gccs[ch] - gcr, 0.0)), 0.0)
            k_bs[ch] = ks[ch].astype(BF16)
            kbs[ch] = ks[ch] * betas[ch]
    lvl = [jnp.right_shift(ri, s) == jnp.right_shift(ci, s) for s in range(INV_BASE_SHIFT, CHUNK_SHIFT + 1)]
    for ch in chains:
        a_mat = jnp.where(strict, _dot_nt(kbs[ch].astype(BF16), k_bs[ch]) * decays[ch], 0.0)
        a_mats[ch] = a_mat
        d_mat = jnp.where(lvl[0], a_mat, 0.0)
        ps[ch] = -d_mat
        ts[ch] = eye - d_mat
    for _ in range(INV_BASE_SHIFT - 1):
        for ch in chains:
            pb = ps[ch].astype(BF16)
            ps[ch] = _dot(pb, pb)
        for ch in chains:
            ts[ch] = ts[ch] + _dot(ts[ch].astype(BF16), ps[ch].astype(BF16))
    for j in range(1, len(lvl)):
        for ch in chains:
            join = jnp.where(lvl[j] & jnp.logical_not(lvl[j - 1]), a_mats[ch], 0.0).astype(BF16)
            t_b = ts[ch].astype(BF16)
            ps[ch] = _dot(_dot(t_b, join).astype(BF16), t_b)
        for ch in chains:
            ts[ch] = ts[ch] - ps[ch]
    for ch in chains:
        t_b = ts[ch].astype(BF16)
        e_gc = jnp.exp(gccs[ch])
        us[ch] = _dot(t_b, (vs[ch] * betas[ch]).astype(BF16))
        ws[ch] = _dot(t_b, (kbs[ch] * e_gc).astype(BF16)).astype(BF16)
        intras[ch] = jnp.where(incl, _dot_nt(qs[ch].astype(BF16), k_bs[ch]) * decays[ch], 0.0).astype(BF16)
        q_decs[ch] = (qs[ch] * e_gc).astype(BF16)
    for c in range(n_chunks):
        r0 = c * CHUNK
        for ch in chains:
            bi, hd = ch
            l0 = hd * B_HEAD_DIM
            g_last = gccs[ch][r0 + CHUNK - 1:r0 + CHUNK, :]
            k_dec = (ks[ch][r0:r0 + CHUNK] * jnp.exp(g_last - gccs[ch][r0:r0 + CHUNK])).astype(BF16)
            state = state_scr[bi * B_HEADS + hd]
            state_b = state.astype(BF16)
            v_new = (us[ch][r0:r0 + CHUNK] - _dot(ws[ch][r0:r0 + CHUNK], state_b)).astype(BF16)
            out = _dot(q_decs[ch][r0:r0 + CHUNK], state_b) + _dot(intras[ch][r0:r0 + CHUNK, r0:r0 + CHUNK], v_new)
            state_scr[bi * B_HEADS + hd] = state * jnp.exp(g_last) + _dot_tn(k_dec, v_new)
            z = h_ref[bi, r0:r0 + CHUNK, 3 * B_WIDTH + l0:3 * B_WIDTH + l0 + B_HEAD_DIM].astype(F32)
            o_n = out * lax.rsqrt(jnp.mean(out * out, axis=-1, keepdims=True) + RMS_EPS) * nw_ref[...]
            o_ref[bi, r0:r0 + CHUNK, l0:l0 + B_HEAD_DIM] = (o_n * (z * _sigmoid(z))).astype(o_ref.dtype)


def _gdn(h_b, ab, abt, conv_w, hp, hpt, norm_w, bsz, seq):
    tr = _row_tile(seq, GDN_ROWS)
    nb = GDN_BATCH if bsz % GDN_BATCH == 0 else 1
    n_s = seq // tr
    out = pl.pallas_call(
        _gdn_kernel,
        grid=(bsz // nb, n_s),
        in_specs=[pl.BlockSpec((nb, tr, 4 * B_WIDTH), lambda b, s: (b, s, 0)),
                  pl.BlockSpec((nb, tr, LANES), lambda b, s: (b, s, 0)),
                  pl.BlockSpec((nb, SUBLANES, tr), lambda b, s: (b, 0, s)),
                  pl.BlockSpec((CONV_K, 3 * B_WIDTH), lambda b, s: (0, 0)),
                  pl.BlockSpec((2, LANES), lambda b, s: (0, 0)),
                  pl.BlockSpec((SUBLANES, 2), lambda b, s: (0, 0)),
                  pl.BlockSpec((1, B_HEAD_DIM), lambda b, s: (0, 0))],
        out_specs=pl.BlockSpec((nb, tr, B_WIDTH), lambda b, s: (b, s, 0)),
        out_shape=jax.ShapeDtypeStruct((bsz, seq, B_WIDTH), BF16),
        scratch_shapes=[pltpu.VMEM((nb, tr + 2 * SUBLANES, 3 * B_WIDTH), F32),
                        pltpu.VMEM((nb * B_HEADS, B_HEAD_DIM, B_HEAD_DIM), F32)],
        compiler_params=_params("parallel", "arbitrary"),
        name="gdn",
    )(h_b.reshape(bsz, seq, 4 * B_WIDTH), ab.reshape(bsz, seq, LANES), abt, conv_w, hp, hpt, norm_w)
    return out.reshape(bsz * seq, B_WIDTH)


def _mla_up_kernel(x_ref, win_ref, qn_ref, kvn_ref, wq_ref, wkv_ref, cs_ref, q_ref, k_ref, v_ref):
    h = _dot(x_ref[...].astype(BF16), win_ref[...])
    cs = cs_ref[...]

    def rms(x, g):
        return x * lax.rsqrt(jnp.mean(x * x, axis=-1, keepdims=True) + RMS_EPS) * g

    def rope(t):
        r = t * cs
        return r + pltpu.roll(r, C_ROPE, 1)

    cq = rms(h[:, 0:Q_LORA], qn_ref[...]).astype(BF16)
    ckv = rms(h[:, Q_LORA:Q_LORA + KV_LORA], kvn_ref[...]).astype(BF16)
    k_pe = rope(h[:, Q_LORA + KV_LORA:Q_LORA + KV_LORA + 2 * C_ROPE])[:, 0:C_ROPE].astype(BF16)
    qf = _dot(cq, wq_ref[...]) * (C_QK ** -0.5 * LOG2_E)
    kv = _dot(ckv, wkv_ref[...])
    for hd in range(C_HEADS):
        q0 = hd * (C_NOPE + 2 * C_ROPE)
        q_ref[0, hd, :, 0:C_NOPE] = qf[:, q0:q0 + C_NOPE].astype(BF16)
        q_ref[0, hd, :, C_NOPE:C_QK] = rope(qf[:, q0 + C_NOPE:q0 + C_NOPE + 2 * C_ROPE])[:, 0:C_ROPE].astype(BF16)
        k0 = hd * (C_NOPE + C_V)
        k_ref[0, hd, :, 0:C_NOPE] = kv[:, k0:k0 + C_NOPE].astype(BF16)
        k_ref[0, hd, :, C_NOPE:C_QK] = k_pe
        v_ref[0, hd, :, :] = kv[:, k0 + C_NOPE:k0 + C_NOPE + C_V].astype(BF16)


def _mla_up(x, w_in, q_norm, kv_norm, w_uq, w_ukv, cs, bsz, seq, tm):
    n_s = seq // tm
    qk_shape = jax.ShapeDtypeStruct((bsz, C_HEADS, seq, C_QK), BF16)
    return pl.pallas_call(
        _mla_up_kernel,
        grid=(bsz, n_s),
        in_specs=[pl.BlockSpec((tm, D_MODEL), lambda b, s: (b * n_s + s, 0)),
                  pl.BlockSpec(w_in.shape, lambda b, s: (0, 0)),
                  pl.BlockSpec((1, Q_LORA), lambda b, s: (0, 0)),
                  pl.BlockSpec((1, KV_LORA), lambda b, s: (0, 0)),
                  pl.BlockSpec(w_uq.shape, lambda b, s: (0, 0)),
                  pl.BlockSpec(w_ukv.shape, lambda b, s: (0, 0)),
                  pl.BlockSpec((tm, LANES), lambda b, s: (s, 0))],
        out_specs=[pl.BlockSpec((1, C_HEADS, tm, C_QK), lambda b, s: (b, 0, s, 0)),
                   pl.BlockSpec((1, C_HEADS, tm, C_QK), lambda b, s: (b, 0, s, 0)),
                   pl.BlockSpec((1, C_HEADS, tm, C_V), lambda b, s: (b, 0, s, 0))],
        out_shape=[qk_shape, qk_shape, jax.ShapeDtypeStruct((bsz, C_HEADS, seq, C_V), BF16)],
        compiler_params=_params("parallel", "parallel"),
        name="mla_up",
    )(x, w_in, q_norm, kv_norm, w_uq, w_ukv, cs)


def _attn_kernel(q_ref, k_ref, v_ref, o_ref, s_scr, p_scr, m_scr, l_scr, a_scr, acc_scr):
    nh, tq = q_ref.shape[1], q_ref.shape[2]
    n_lt = tq // LANES
    qi = pl.program_id(2)

    def scores(hd, k0):
        s_scr[hd] = _dot_nt(q_ref[0, hd], k_ref[0, hd, pl.ds(k0, tq), :])

    def softmax(hd, first):
        for r0 in range(0, tq, ATTN_SUB):
            rows = slice(r0, r0 + ATTN_SUB)
            n_vis = -(-(r0 + ATTN_SUB) // LANES) if first else n_lt
            tiles = []
            for c in range(n_vis):
                s = s_scr[hd, rows, c * LANES:(c + 1) * LANES]
                if first and ((c + 1) * LANES - 1) // CHUNK > r0 // CHUNK:
                    ri = jnp.right_shift(r0 + lax.broadcasted_iota(jnp.int32, (ATTN_SUB, LANES), 0), CHUNK_SHIFT)
                    ci = jnp.right_shift(c * LANES + lax.broadcasted_iota(jnp.int32, (ATTN_SUB, LANES), 1),
                                         CHUNK_SHIFT)
                    s = jnp.where(ci <= ri, s, NEG_BIG)
                tiles.append(s)
            mx = tiles[0]
            for c in range(1, n_vis):
                mx = jnp.maximum(mx, tiles[c])
            m_new = jnp.broadcast_to(jnp.max(mx, axis=-1, keepdims=True), (ATTN_SUB, LANES))
            if not first:
                m_prev = m_scr[hd, rows, :]
                m_new = jnp.maximum(m_prev, m_new)
                alpha = jnp.exp2(m_prev - m_new)
                a_scr[hd, rows, :] = alpha
            psum = jnp.zeros((ATTN_SUB, LANES), F32)
            for c in range(n_vis):
                p = jnp.exp2(tiles[c] - m_new)
                psum = psum + p
                p_scr[hd, rows, c * LANES:(c + 1) * LANES] = p.astype(BF16)
            for c in range(n_vis, n_lt):
                p_scr[hd, rows, c * LANES:(c + 1) * LANES] = jnp.zeros((ATTN_SUB, LANES), BF16)
            m_scr[hd, rows, :] = m_new
            l_new = jnp.broadcast_to(jnp.sum(psum, axis=-1, keepdims=True), (ATTN_SUB, LANES))
            l_scr[hd, rows, :] = l_new if first else alpha * l_scr[hd, rows, :] + l_new

    def values(hd, k0, first):
        pv = _dot(p_scr[hd], v_ref[0, hd, pl.ds(k0, tq), :])
        acc_scr[hd] = pv if first else a_scr[hd] * acc_scr[hd] + pv

    def block(k0, first):
        for hd in range(nh):
            scores(hd, k0)
        for hd in range(nh):
            softmax(hd, first)
        for hd in range(nh):
            values(hd, k0, first)

    def body(j, carry):
        block(pl.multiple_of(j * tq, tq), False)
        return carry

    block(pl.multiple_of(qi * tq, tq), True)
    lax.fori_loop(0, qi, body, 0)
    for hd in range(nh):
        o_ref[0, :, hd * C_V:(hd + 1) * C_V] = (acc_scr[hd] / l_scr[hd]).astype(o_ref.dtype)


def _attn(q, k, v):
    bsz, _, seq, _ = q.shape
    tq = _row_tile(seq, ATTN_ROWS)
    nh = ATTN_HEADS_PER_STEP
    return pl.pallas_call(
        _attn_kernel,
        grid=(bsz, C_HEADS // nh, seq // tq),
        in_specs=[pl.BlockSpec((1, nh, tq, C_QK), lambda b, h, i: (b, h, i, 0)),
                  pl.BlockSpec((1, nh, seq, C_QK), lambda b, h, i: (b, h, 0, 0)),
                  pl.BlockSpec((1, nh, seq, C_V), lambda b, h, i: (b, h, 0, 0))],
        out_specs=pl.BlockSpec((1, tq, nh * C_V), lambda b, h, i: (b, i, h)),
        out_shape=jax.ShapeDtypeStruct((bsz, seq, C_HEADS * C_V), BF16),
        scratch_shapes=[pltpu.VMEM((nh, tq, tq), F32),
                        pltpu.VMEM((nh, tq, tq), BF16),
                        pltpu.VMEM((nh, tq, LANES), F32),
                        pltpu.VMEM((nh, tq, LANES), F32),
                        pltpu.VMEM((nh, tq, LANES), F32),
                        pltpu.VMEM((nh, tq, C_V), F32)],
        compiler_params=_params("parallel", "parallel", "arbitrary"),
        name="attn",
    )(q, k, v)


def _proj_ln_router_kernel(n_in, *refs):
    a_refs = refs[0:n_in]
    w_refs = refs[n_in:2 * n_in]
    x_ref, g_ref, b_ref, wr_ref, br_ref = refs[2 * n_in:2 * n_in + 5]
    x1_ref, x1b_ref, meta_ref, gate_ref, cnt_ref = refs[2 * n_in + 5:2 * n_in + 10]
    base_scr = refs[2 * n_in + 10]
    tm = x_ref.shape[0]
    i = pl.program_id(0)

    @pl.when(i == 0)
    def _():
        base_scr[...] = jnp.zeros(base_scr.shape, F32)

    mix = _dot(a_refs[0][...], w_refs[0][...])
    for j in range(1, n_in):
        mix = mix + _dot(a_refs[j][...], w_refs[j][...])
    x1 = _layer_norm_rows(DEEPNORM_ALPHA * x_ref[...] + mix, g_ref[...], b_ref[...])
    x1_ref[...] = x1
    x1b = x1.astype(BF16)
    x1b_ref[...] = x1b

    work = _dot_nt(wr_ref[...], x1b) + br_ref[:, 0:1]
    eidx = lax.broadcasted_iota(jnp.int32, (N_EXPERTS, tm), 0).astype(F32)
    cnt = jnp.zeros((N_EXPERTS, tm), F32)
    tops, idxs, hots = [], [], []
    for _ in range(TOP_K):
        mx = jnp.max(work, axis=0, keepdims=True)
        idx = jnp.min(jnp.where(work == mx, eidx, float(N_EXPERTS)), axis=0, keepdims=True)
        hot = eidx == idx
        work = jnp.where(hot, NEG_BIG, work)
        cnt = cnt + jnp.where(hot, 1.0, 0.0)
        tops.append(mx)
        idxs.append(idx)
        hots.append(hot)
    exps = [jnp.exp(t - tops[0]) for t in tops]
    denom = exps[0] + exps[1] + exps[2] + exps[3]

    ri = lax.broadcasted_iota(jnp.int32, (tm, tm), 0)
    ci = lax.broadcasted_iota(jnp.int32, (tm, tm), 1)
    before = jnp.where(ri < ci, 1.0, 0.0).astype(BF16)
    base = base_scr[:, 0:1]
    rank = _dot(cnt.astype(BF16), before) + base
    row = lax.broadcasted_iota(jnp.int32, (2 * TOP_K, tm), 0)
    meta = jnp.zeros((2 * TOP_K, tm), jnp.int32)
    gates = jnp.zeros((2 * TOP_K, tm), F32)
    for kk in range(TOP_K):
        pos = jnp.sum(jnp.where(hots[kk], rank, 0.0), axis=0, keepdims=True).astype(jnp.int32)
        meta = jnp.where(row == kk, idxs[kk].astype(jnp.int32), meta)
        meta = jnp.where(row == TOP_K + kk, pos, meta)
        gates = jnp.where(row == kk, exps[kk] / denom, gates)
    meta_ref[...] = meta
    gate_ref[...] = gates
    total = base + jnp.sum(cnt, axis=1, keepdims=True)
    base_scr[...] = jnp.broadcast_to(total, base_scr.shape)
    cnt_ref[...] = jnp.broadcast_to(total, cnt_ref.shape)


def _proj_ln_router(a_list, w_list, x, ln_g, ln_b, w_r, b_r, tm):
    t = x.shape[0]
    n_in = len(a_list)
    in_specs = [pl.BlockSpec((tm, a.shape[1]), lambda i: (i, 0)) for a in a_list]
    in_specs += [pl.BlockSpec(w.shape, lambda i: (0, 0)) for w in w_list]
    in_specs += [pl.BlockSpec((tm, D_MODEL), lambda i: (i, 0)),
                 pl.BlockSpec((1, D_MODEL), lambda i: (0, 0)),
                 pl.BlockSpec((1, D_MODEL), lambda i: (0, 0)),
                 pl.BlockSpec((N_EXPERTS, D_MODEL), lambda i: (0, 0)),
                 pl.BlockSpec((N_EXPERTS, LANES), lambda i: (0, 0))]
    return pl.pallas_call(
        functools.partial(_proj_ln_router_kernel, n_in),
        grid=(t // tm,),
        in_specs=in_specs,
        out_specs=[pl.BlockSpec((tm, D_MODEL), lambda i: (i, 0)),
                   pl.BlockSpec((tm, D_MODEL), lambda i: (i, 0)),
                   pl.BlockSpec((2 * TOP_K, tm), lambda i: (0, i)),
                   pl.BlockSpec((2 * TOP_K, tm), lambda i: (0, i)),
                   pl.BlockSpec((N_EXPERTS, LANES), lambda i: (0, 0))],
        out_shape=[jax.ShapeDtypeStruct((t, D_MODEL), F32),
                   jax.ShapeDtypeStruct((t, D_MODEL), BF16),
                   jax.ShapeDtypeStruct((2 * TOP_K, t), jnp.int32),
                   jax.ShapeDtypeStruct((2 * TOP_K, t), F32),
                   jax.ShapeDtypeStruct((N_EXPERTS, LANES), F32)],
        scratch_shapes=[pltpu.VMEM((N_EXPERTS, LANES), F32)],
        compiler_params=_params("arbitrary"),
        name="proj_ln_router",
    )(*a_list, *w_list, x, ln_g, ln_b, w_r, b_r)


def _moe_kernel(blk_ref, exp_ref, lo_ref, hi_ref, x_ref, wgu_ref, wd_ref, bg_ref, bu_ref, bd_ref, o_ref,
                wg_scr, wu_scr, wd_scr):
    i = pl.program_id(0)
    tm = x_ref.shape[0]
    blk = blk_ref[i]
    lo = lo_ref[i]
    hi = hi_ref[i]
    prev = jnp.maximum(i - 1, 0)
    first = jnp.logical_or(i == 0, blk != blk_ref[prev])
    new_expert = jnp.logical_or(i == 0, exp_ref[i] != exp_ref[prev])

    @pl.when(new_expert)
    def _():
        pc = MOE_PREP_COLS
        half = pc // 2
        for c in range(2 * D_FF // pc):
            for kc in range(D_MODEL // LANES):
                k0 = kc * LANES
                tb = wgu_ref[0, 0, k0:k0 + LANES, c * pc:(c + 1) * pc].astype(BF16).T
                pair = pltpu.bitcast(tb, jnp.uint32)
                gate = pltpu.bitcast(jnp.left_shift(pair, 16), F32)
                up = pltpu.bitcast(jnp.bitwise_and(pair, jnp.uint32(0xFFFF0000)), F32)
                wg_scr[c * half:(c + 1) * half, k0:k0 + LANES] = gate.astype(BF16)
                wu_scr[c * half:(c + 1) * half, k0:k0 + LANES] = up.astype(BF16)
        wd_scr[...] = wd_ref[0, 0].astype(BF16)

    def ffn(x):
        glu = jnp.minimum(_dot_nt(x, wg_scr[...]) + bg_ref[0, 0], SWIGLU_LIMIT)
        lin = jnp.clip(_dot_nt(x, wu_scr[...]) + bu_ref[0, 0], -SWIGLU_LIMIT, SWIGLU_LIMIT)
        act = glu * _sigmoid(SWIGLU_ALPHA * glu) * (lin + 1.0)
        return (_dot(act.astype(BF16), wd_scr[...]) + bd_ref[0, 0]).astype(o_ref.dtype)

    full = (hi - lo) == tm

    @pl.when(full)
    def _():
        o_ref[...] = ffn(x_ref[...])

    @pl.when(jnp.logical_and(first, jnp.logical_not(full)))
    def _():
        o_ref[...] = jnp.zeros(o_ref.shape, o_ref.dtype)

    for r0 in range(0, tm, MOE_PART):
        row0 = blk * tm + r0
        touched = jnp.logical_and(hi > lo, jnp.logical_and(lo < row0 + MOE_PART, hi > row0))

        @pl.when(jnp.logical_and(jnp.logical_not(full), touched))
        def _():
            y = ffn(x_ref[r0:r0 + MOE_PART, :])
            row = row0 + lax.broadcasted_iota(jnp.int32, (MOE_PART, 1), 0)
            keep = jnp.logical_and(row >= lo, row < hi)
            o_ref[r0:r0 + MOE_PART, :] = jnp.where(keep, y, o_ref[r0:r0 + MOE_PART, :])


def _moe_sorted(layer, xs, counts, w_gate_up, b_gate, b_up, w_down, b_down):
    n_rows = xs.shape[0]
    tm = _row_tile(n_rows, MOE_ROWS)
    items = _moe_items(counts, n_rows, tm)
    n_items = items[0].shape[0]

    def by_expert(i, blk, ex, lo, hi):
        return (layer, ex[i], 0, 0)

    def by_block(i, blk, ex, lo, hi):
        return (blk[i], 0)

    grid_spec = pltpu.PrefetchScalarGridSpec(
        num_scalar_prefetch=4,
        grid=(n_items,),
        in_specs=[pl.BlockSpec((tm, D_MODEL), by_block),
                  pl.BlockSpec((1, 1, D_MODEL, 2 * D_FF), by_expert),
                  pl.BlockSpec((1, 1, D_FF, D_MODEL), by_expert),
                  pl.BlockSpec((1, 1, 1, D_FF), by_expert),
                  pl.BlockSpec((1, 1, 1, D_FF), by_expert),
                  pl.BlockSpec((1, 1, 1, D_MODEL), by_expert)],
        out_specs=pl.BlockSpec((tm, D_MODEL), by_block),
        scratch_shapes=[pltpu.VMEM((D_FF, D_MODEL), BF16),
                        pltpu.VMEM((D_FF, D_MODEL), BF16),
                        pltpu.VMEM((D_FF, D_MODEL), BF16)],
    )
    return pl.pallas_call(
        _moe_kernel,
        grid_spec=grid_spec,
        out_shape=jax.ShapeDtypeStruct((n_rows, D_MODEL), BF16),
        compiler_params=_params("arbitrary", vmem=MOE_VMEM_LIMIT),
        name="moe_ffn",
    )(*items, xs, w_gate_up, w_down, b_gate, b_up, b_down)


def _moe_items(counts, n_rows, tm):
    n_blocks = n_rows // tm
    start = jnp.cumsum(counts) - counts
    cuts = jnp.sort(jnp.concatenate([jnp.arange(n_blocks, dtype=jnp.int32) * tm, start.astype(jnp.int32)]))
    lo = cuts
    hi = jnp.concatenate([cuts[1:], jnp.array([n_rows], jnp.int32)])
    blk = jnp.minimum(lo // tm, n_blocks - 1)
    ex = jnp.clip(jnp.sum((start[None, :] <= lo[:, None]).astype(jnp.int32), axis=1) - 1, 0, N_EXPERTS - 1)
    return blk.astype(jnp.int32), ex.astype(jnp.int32), lo.astype(jnp.int32), hi.astype(jnp.int32)


def _combine_ln_kernel(x_ref, y_ref, gate_ref, g_ref, b_ref, o_ref):
    gates = gate_ref[...]
    ffn = y_ref[0].astype(F32) * gates[:, 0:1]
    for kk in range(1, TOP_K):
        ffn = ffn + y_ref[kk].astype(F32) * gates[:, kk:kk + 1]
    o_ref[...] = _layer_norm_rows(DEEPNORM_ALPHA * x_ref[...] + ffn, g_ref[...], b_ref[...])


def _combine_ln(x1, y4, gates, ln_g, ln_b, tm):
    t = x1.shape[0]
    return pl.pallas_call(
        _combine_ln_kernel,
        grid=(t // tm,),
        in_specs=[pl.BlockSpec((tm, D_MODEL), lambda i: (i, 0)),
                  pl.BlockSpec((TOP_K, tm, D_MODEL), lambda i: (0, i, 0)),
                  pl.BlockSpec((tm, 2 * TOP_K), lambda i: (i, 0)),
                  pl.BlockSpec((1, D_MODEL), lambda i: (0, 0)),
                  pl.BlockSpec((1, D_MODEL), lambda i: (0, 0))],
        out_specs=pl.BlockSpec((tm, D_MODEL), lambda i: (i, 0)),
        out_shape=jax.ShapeDtypeStruct((t, D_MODEL), F32),
        compiler_params=_params("parallel"),
        name="combine_ln",
    )(x1, y4, gates, ln_g, ln_b)


def _moe_layer(layer, x1, x1b, meta_t, gates_t, counts, ln_g, ln_b, w_gate_up, b_gate, b_up, w_down, b_down):
    t = x1.shape[0]
    assert N_EXPERTS * t < 2 ** 31
    e = meta_t[0:TOP_K]
    pos = meta_t[TOP_K:2 * TOP_K]
    cnt = counts[:, 0].astype(jnp.int32)
    start = jnp.cumsum(cnt) - cnt
    experts = jnp.arange(N_EXPERTS, dtype=jnp.int32)
    dest = pos + jnp.sum(jnp.where(e[:, :, None] == experts, start, 0), axis=-1)
    keys = e * t + jnp.arange(t, dtype=jnp.int32)[None, :]
    tok_sorted = jnp.remainder(jnp.sort(keys.reshape(-1)), t)
    xs = x1b[tok_sorted]
    ys = _moe_sorted(layer, xs, cnt, w_gate_up, b_gate, b_up, w_down, b_down)
    y4 = ys[dest]
    return _combine_ln(x1, y4, gates_t.T, ln_g, ln_b, _row_tile(t, ROWS))


def kernel(x, even_w_in, even_sgu_ln_g, even_sgu_ln_b, even_sgu_w, even_sgu_b, even_conv_w, even_a_log, even_dt_bias, even_gdn_norm, even_w_out, odd_w_in, odd_q_norm, odd_w_uq, odd_kv_norm, odd_w_ukv, odd_w_out, ln_mix_g, ln_mix_b, ln_ffn_g, ln_ffn_b, moe_w_router, moe_b_router, moe_w_gate_up, moe_b_gate_up, moe_w_down, moe_b_down):
    bsz, seq, d = x.shape
    t = bsz * seq
    xt = x.reshape(t, d)
    tm = _row_tile(t, ROWS)
    b_gate = moe_b_gate_up[:, :, None, 0::2]
    b_up = moe_b_gate_up[:, :, None, 1::2]
    b_down = moe_b_down[:, :, None, :]

    inv_freq = ROPE_THETA ** (-jnp.arange(0, C_ROPE, 2, dtype=F32) / C_ROPE)
    ang = jnp.arange(seq, dtype=F32)[:, None] * inv_freq[None, :]
    cs = jnp.concatenate([jnp.cos(ang), jnp.cos(ang), jnp.sin(ang), jnp.sin(ang)], axis=1)

    pos_chunk = np.arange(A_BLOCK) // CHUNK
    allowed = jnp.asarray(pos_chunk[None, :] <= pos_chunk[:, None])

    def rot_cols(w):
        half = C_ROPE // 2
        return jnp.concatenate([-w[..., half:], w[..., :half]], axis=-1)

    for layer in range(DEPTH):
        i = layer // 2
        if layer % 2 == 0:
            w_in = even_w_in[i]
            w_all = jnp.pad(w_in, ((0, 0), (0, LANES - 2 * B_HEADS))).astype(BF16)
            sw = jnp.where(allowed[None], even_sgu_w[i], 0).astype(BF16)
            sb = jnp.broadcast_to(even_sgu_b[i][:, :, None], (A_GROUPS, A_BLOCK, A_GROUP_DIM)).astype(F32)
            y_a, h_b, ab = _even_in(xt, w_all, even_sgu_ln_g[i][None], even_sgu_ln_b[i][None], sw, sb, tm)
            abt = ab.reshape(bsz, seq, LANES)[:, :, 0:SUBLANES].transpose(0, 2, 1)
            hp = jnp.pad(jnp.stack([even_a_log[i], even_dt_bias[i]]), ((0, 0), (0, LANES - B_HEADS)))
            hpt = jnp.pad(jnp.stack([even_a_log[i], even_dt_bias[i]], axis=1), ((0, SUBLANES - B_HEADS), (0, 0)))
            y_b = _gdn(h_b, ab, abt, even_conv_w[i], hp, hpt, even_gdn_norm[i][None], bsz, seq)
            w_out = even_w_out[i].astype(BF16)
            a_list = [y_a, y_b]
            w_list = [w_out[0:A_WIDTH], w_out[A_WIDTH:]]
        else:
            w_in = odd_w_in[i]
            w_pe = w_in[:, Q_LORA + KV_LORA:]
            w_in2 = jnp.concatenate([w_in, rot_cols(w_pe)], axis=1).astype(BF16)
            w_uq = odd_w_uq[i].reshape(Q_LORA, C_HEADS, C_QK)
            w_uq2 = jnp.concatenate([w_uq, rot_cols(w_uq[..., C_NOPE:])], axis=-1)
            w_uq2 = w_uq2.reshape(Q_LORA, C_HEADS * (C_NOPE + 2 * C_ROPE)).astype(BF16)
            q, k, v = _mla_up(xt, w_in2, odd_q_norm[i][None], odd_kv_norm[i][None], w_uq2,
                              odd_w_ukv[i].astype(BF16), cs, bsz, seq, _row_tile(seq, ROWS))
            o = _attn(q, k, v)
            a_list = [o.reshape(t, C_HEADS * C_V)]
            w_list = [odd_w_out[i].astype(BF16)]

        w_r = moe_w_router[layer].T.astype(BF16)
        b_r = jnp.broadcast_to(moe_b_router[layer][:, None], (N_EXPERTS, LANES))
        x1, x1b, meta, gates, counts = _proj_ln_router(a_list, w_list, xt, ln_mix_g[layer][None],
                                                       ln_mix_b[layer][None], w_r, b_r, tm)
        xt = _moe_layer(layer, x1, x1b, meta, gates, counts, ln_ffn_g[layer][None], ln_ffn_b[layer][None],
                        moe_w_gate_up, b_gate, b_up, moe_w_down, b_down)
    return xt.reshape(bsz, seq, d)
```

```python
import functools

import jax
import jax.numpy as jnp
import numpy as np
from jax import lax
from jax.experimental import pallas as pl
from jax.experimental.pallas import tpu as pltpu

F32 = jnp.float32
BF16 = jnp.bfloat16

D_MODEL = 1024
DEPTH = 4
CHUNK = 64
CHUNK_SHIFT = CHUNK.bit_length() - 1
DEEPNORM_ALPHA = (2 * DEPTH) ** 0.25
LN_EPS = 1e-5
RMS_EPS = 1e-6

A_BLOCK = 128
A_GROUPS = 4
A_WIDTH = D_MODEL // 2
A_GROUP_DIM = A_WIDTH // A_GROUPS

B_HEADS = 4
B_HEAD_DIM = D_MODEL // 8
B_WIDTH = B_HEADS * B_HEAD_DIM
CONV_K = 4

C_HEADS = 8
C_NOPE = 128
C_ROPE = 64
C_V = D_MODEL // C_HEADS
C_QK = C_NOPE + C_ROPE
Q_LORA = 3 * D_MODEL // 8
KV_LORA = D_MODEL // 4
ROPE_THETA = 10000.0

N_EXPERTS = 32
TOP_K = 4
D_FF = D_MODEL
SWIGLU_LIMIT = 7.0
SWIGLU_ALPHA = 1.702

LANES = 128
SUBLANES = 8
VMEM_LIMIT = 48 * 1024 * 1024
MOE_VMEM_LIMIT = 58 * 1024 * 1024
NEG_BIG = -1e30
LOG2_E = 1.4426950408889634

ROWS = 512
GDN_ROWS = 256
GDN_BATCH = 2
INV_BASE_SHIFT = 1
ATTN_ROWS = 512
ATTN_HEADS_PER_STEP = 4
ATTN_SUB = 128
MOE_ROWS = 512
MOE_PART = 256
MOE_PREP_COLS = 512


def _row_tile(t, pref):
    tm = pref
    while t % tm:
        tm //= 2
    return tm


def _params(*sem, vmem=VMEM_LIMIT):
    return pltpu.CompilerParams(dimension_semantics=sem, vmem_limit_bytes=vmem)


def _dot(a, b):
    return jnp.dot(a, b, preferred_element_type=F32)


def _dot_nt(a, b):
    return lax.dot_general(a, b, (((1,), (1,)), ((), ())), preferred_element_type=F32)


def _dot_tn(a, b):
    return lax.dot_general(a, b, (((0,), (0,)), ((), ())), preferred_element_type=F32)


def _sigmoid(x):
    return 1.0 / (1.0 + jnp.exp(-x))


def _softplus(x):
    return jnp.maximum(x, 0.0) + jnp.log1p(jnp.exp(-jnp.abs(x)))


def _layer_norm_rows(y, g, b):
    mu = jnp.mean(y, axis=-1, keepdims=True)
    d = y - mu
    var = jnp.mean(d * d, axis=-1, keepdims=True)
    return d * lax.rsqrt(var + LN_EPS) * g + b


def _gelu(x):
    return 0.5 * x * (1.0 + lax.erf(x * (2.0 ** -0.5)))


def _even_in_kernel(x_ref, w_ref, g_ref, b_ref, sw_ref, sb_ref, o_ref, hb_ref, ab_ref):
    tm = x_ref.shape[0]
    h = _dot(x_ref[...].astype(BF16), w_ref[...])
    n_b = hb_ref.shape[1]
    hb_ref[...] = h[:, 2 * A_WIDTH:2 * A_WIDTH + n_b].astype(hb_ref.dtype)
    ab_ref[...] = h[:, 2 * A_WIDTH + n_b:]
    for grp in range(A_GROUPS):
        c0 = grp * A_GROUP_DIM
        u = _gelu(h[:, c0:c0 + A_GROUP_DIM])
        v = _gelu(h[:, A_WIDTH + c0:A_WIDTH + c0 + A_GROUP_DIM])
        vn = _layer_norm_rows(v, g_ref[:, c0:c0 + A_GROUP_DIM], b_ref[:, c0:c0 + A_GROUP_DIM]).astype(BF16)
        for blk in range(tm // A_BLOCK):
            r0 = blk * A_BLOCK
            mixed = _dot(sw_ref[grp], vn[r0:r0 + A_BLOCK]) + sb_ref[grp]
            o_ref[r0:r0 + A_BLOCK, c0:c0 + A_GROUP_DIM] = (u[r0:r0 + A_BLOCK] * mixed).astype(o_ref.dtype)


def _even_in(x, w_all, ln_g, ln_b, sw, sb, tm):
    t = x.shape[0]
    n_all = w_all.shape[1]
    return pl.pallas_call(
        _even_in_kernel,
        grid=(t // tm,),
        in_specs=[pl.BlockSpec((tm, D_MODEL), lambda i: (i, 0)),
                  pl.BlockSpec((D_MODEL, n_all), lambda i: (0, 0)),
                  pl.BlockSpec((1, A_WIDTH), lambda i: (0, 0)),
                  pl.BlockSpec((1, A_WIDTH), lambda i: (0, 0)),
                  pl.BlockSpec((A_GROUPS, A_BLOCK, A_BLOCK), lambda i: (0, 0, 0)),
                  pl.BlockSpec((A_GROUPS, A_BLOCK, A_GROUP_DIM), lambda i: (0, 0, 0))],
        out_specs=[pl.BlockSpec((tm, A_WIDTH), lambda i: (i, 0)),
                   pl.BlockSpec((tm, 4 * B_WIDTH), lambda i: (i, 0)),
                   pl.BlockSpec((tm, LANES), lambda i: (i, 0))],
        out_shape=[jax.ShapeDtypeStruct((t, A_WIDTH), BF16),
                   jax.ShapeDtypeStruct((t, 4 * B_WIDTH), BF16),
                   jax.ShapeDtypeStruct((t, LANES), F32)],
        compiler_params=_params("parallel"),
        name="even_in",
    )(x, w_all, ln_g, ln_b, sw, sb)


def _gdn_kernel(h_ref, ab_ref, abt_ref, cw_ref, hp_ref, hpt_ref, nw_ref, o_ref, conv_scr, state_scr):
    nb, tr = h_ref.shape[0], h_ref.shape[1]
    n_chunks = tr // CHUNK
    s_idx = pl.program_id(1)

    @pl.when(s_idx == 0)
    def _():
        conv_scr[:, 0:SUBLANES, :] = jnp.zeros((nb, SUBLANES, 3 * B_WIDTH), F32)
        state_scr[...] = jnp.zeros(state_scr.shape, F32)

    ri = lax.broadcasted_iota(jnp.int32, (tr, tr), 0)
    ci = lax.broadcasted_iota(jnp.int32, (tr, tr), 1)
    same = jnp.right_shift(ri, CHUNK_SHIFT) == jnp.right_shift(ci, CHUNK_SHIFT)
    lower = jnp.where(same & (ri >= ci), 1.0, 0.0).astype(F32)
    upper = jnp.where(same & (ri <= ci), 1.0, 0.0).astype(F32)
    incl = same & (ri >= ci)
    strict = same & (ri > ci)
    eye = jnp.where(ri == ci, 1.0, 0.0).astype(F32)

    chains = [(bi, hd) for bi in range(nb) for hd in range(B_HEADS)]
    qs, ks, vs, betas, gccs, decays, k_bs, kbs, ps, ts, a_mats = {}, {}, {}, {}, {}, {}, {}, {}, {}, {}, {}
    us, ws, intras, q_decs = {}, {}, {}, {}
    for bi in range(nb):
        conv_scr[bi, SUBLANES:SUBLANES + tr, :] = h_ref[bi, :, 0:3 * B_WIDTH].astype(F32)
        acc = conv_scr[bi, SUBLANES:SUBLANES + tr, :] * cw_ref[CONV_K - 1:CONV_K, :]
        for d in range(1, CONV_K):
            acc = acc + conv_scr[bi, SUBLANES - d:SUBLANES - d + tr, :] * cw_ref[CONV_K - 1 - d:CONV_K - d, :]
        conv_scr[bi, 0:SUBLANES, :] = conv_scr[bi, tr:tr + SUBLANES, :]
        qkv = acc * _sigmoid(acc)

        ab = ab_ref[bi]
        sp_c = _softplus(ab + hp_ref[1:2, :])
        g_col = -jnp.exp(hp_ref[0:1, :]) * sp_c
        beta_col = _sigmoid(ab)
        abt = abt_ref[bi]
        sp_r = _softplus(abt + hpt_ref[:, 1:2])
        g_row = -jnp.exp(hpt_ref[:, 0:1]) * sp_r
        gc_col = jnp.dot(lower, g_col, preferred_element_type=F32, precision=lax.Precision.HIGHEST)
        gc_row = jnp.dot(g_row, upper, preferred_element_type=F32, precision=lax.Precision.HIGHEST)
        for hd in range(B_HEADS):
            ch = (bi, hd)
            l0 = hd * B_HEAD_DIM
            q = qkv[:, l0:l0 + B_HEAD_DIM]
            k = qkv[:, B_WIDTH + l0:B_WIDTH + l0 + B_HEAD_DIM]
            qs[ch] = q * lax.rsqrt(jnp.sum(q * q, axis=-1, keepdims=True) + RMS_EPS) * (B_HEAD_DIM ** -0.5)
            ks[ch] = k * lax.rsqrt(jnp.sum(k * k, axis=-1, keepdims=True) + RMS_EPS)
            vs[ch] = qkv[:, 2 * B_WIDTH + l0:2 * B_WIDTH + l0 + B_HEAD_DIM]
            betas[ch] = beta_col[:, B_HEADS + hd:B_HEADS + hd + 1]
            gccs[ch] = gc_col[:, hd:hd + 1]
            gcr = gc_row[hd:hd + 1, :]
            decays[ch] = jnp.where(incl, jnp.exp(jnp.where(incl, gccs[ch] - gcr, 0.0)), 0.0)
            k_bs[ch] = ks[ch].astype(BF16)
            kbs[ch] = ks[ch] * betas[ch]
    lvl = [jnp.right_shift(ri, s) == jnp.right_shift(ci, s) for s in range(INV_BASE_SHIFT, CHUNK_SHIFT + 1)]
    for ch in chains:
        a_mat = jnp.where(strict, _dot_nt(kbs[ch].astype(BF16), k_bs[ch]) * decays[ch], 0.0)
        a_mats[ch] = a_mat
        d_mat = jnp.where(lvl[0], a_mat, 0.0)
        ps[ch] = -d_mat
        ts[ch] = eye - d_mat
    for _ in range(INV_BASE_SHIFT - 1):
        for ch in chains:
            pb = ps[ch].astype(BF16)
            ps[ch] = _dot(pb, pb)
        for ch in chains:
            ts[ch] = ts[ch] + _dot(ts[ch].astype(BF16), ps[ch].astype(BF16))
    for j in range(1, len(lvl)):
        for ch in chains:
            join = jnp.where(lvl[j] & jnp.logical_not(lvl[j - 1]), a_mats[ch], 0.0).astype(BF16)
            t_b = ts[ch].astype(BF16)
            ps[ch] = _dot(_dot(t_b, join).astype(BF16), t_b)
        for ch in chains:
            ts[ch] = ts[ch] - ps[ch]
    for ch in chains:
        t_b = ts[ch].astype(BF16)
        e_gc = jnp.exp(gccs[ch])
        us[ch] = _dot(t_b, (vs[ch] * betas[ch]).astype(BF16))
        ws[ch] = _dot(t_b, (kbs[ch] * e_gc).astype(BF16)).astype(BF16)
        intras[ch] = jnp.where(incl, _dot_nt(qs[ch].astype(BF16), k_bs[ch]) * decays[ch], 0.0).astype(BF16)
        q_decs[ch] = (qs[ch] * e_gc).astype(BF16)
    for c in range(n_chunks):
        r0 = c * CHUNK
        for ch in chains:
            bi, hd = ch
            l0 = hd * B_HEAD_DIM
            g_last = gccs[ch][r0 + CHUNK - 1:r0 + CHUNK, :]
            k_dec = (ks[ch][r0:r0 + CHUNK] * jnp.exp(g_last - gccs[ch][r0:r0 + CHUNK])).astype(BF16)
            state = state_scr[bi * B_HEADS + hd]
            state_b = state.astype(BF16)
            v_new = (us[ch][r0:r0 + CHUNK] - _dot(ws[ch][r0:r0 + CHUNK], state_b)).astype(BF16)
            out = _dot(q_decs[ch][r0:r0 + CHUNK], state_b) + _dot(intras[ch][r0:r0 + CHUNK, r0:r0 + CHUNK], v_new)
            state_scr[bi * B_HEADS + hd] = state * jnp.exp(g_last) + _dot_tn(k_dec, v_new)
            z = h_ref[bi, r0:r0 + CHUNK, 3 * B_WIDTH + l0:3 * B_WIDTH + l0 + B_HEAD_DIM].astype(F32)
            o_n = out * lax.rsqrt(jnp.mean(out * out, axis=-1, keepdims=True) + RMS_EPS) * nw_ref[...]
            o_ref[bi, r0:r0 + CHUNK, l0:l0 + B_HEAD_DIM] = (o_n * (z * _sigmoid(z))).astype(o_ref.dtype)


def _gdn(h_b, ab, abt, conv_w, hp, hpt, norm_w, bsz, seq):
    tr = _row_tile(seq, GDN_ROWS)
    nb = GDN_BATCH if bsz % GDN_BATCH == 0 else 1
    n_s = seq // tr
    out = pl.pallas_call(
        _gdn_kernel,
        grid=(bsz // nb, n_s),
        in_specs=[pl.BlockSpec((nb, tr, 4 * B_WIDTH), lambda b, s: (b, s, 0)),
                  pl.BlockSpec((nb, tr, LANES), lambda b, s: (b, s, 0)),
                  pl.BlockSpec((nb, SUBLANES, tr), lambda b, s: (b, 0, s)),
                  pl.BlockSpec((CONV_K, 3 * B_WIDTH), lambda b, s: (0, 0)),
                  pl.BlockSpec((2, LANES), lambda b, s: (0, 0)),
                  pl.BlockSpec((SUBLANES, 2), lambda b, s: (0, 0)),
                  pl.BlockSpec((1, B_HEAD_DIM), lambda b, s: (0, 0))],
        out_specs=pl.BlockSpec((nb, tr, B_WIDTH), lambda b, s: (b, s, 0)),
        out_shape=jax.ShapeDtypeStruct((bsz, seq, B_WIDTH), BF16),
        scratch_shapes=[pltpu.VMEM((nb, tr + 2 * SUBLANES, 3 * B_WIDTH), F32),
                        pltpu.VMEM((nb * B_HEADS, B_HEAD_DIM, B_HEAD_DIM), F32)],
        compiler_params=_params("parallel", "arbitrary"),
        name="gdn",
    )(h_b.reshape(bsz, seq, 4 * B_WIDTH), ab.reshape(bsz, seq, LANES), abt, conv_w, hp, hpt, norm_w)
    return out.reshape(bsz * seq, B_WIDTH)


def _mla_up_kernel(x_ref, win_ref, qn_ref, kvn_ref, wq_ref, wkv_ref, cs_ref, q_ref, k_ref, v_ref):
    h = _dot(x_ref[...].astype(BF16), win_ref[...])
    cs = cs_ref[...]

    def rms(x, g):
        return x * lax.rsqrt(jnp.mean(x * x, axis=-1, keepdims=True) + RMS_EPS) * g

    def rope(t):
        r = t * cs
        return r + pltpu.roll(r, C_ROPE, 1)

    cq = rms(h[:, 0:Q_LORA], qn_ref[...]).astype(BF16)
    ckv = rms(h[:, Q_LORA:Q_LORA + KV_LORA], kvn_ref[...]).astype(BF16)
    k_pe = rope(h[:, Q_LORA + KV_LORA:Q_LORA + KV_LORA + 2 * C_ROPE])[:, 0:C_ROPE].astype(BF16)
    qf = _dot(cq, wq_ref[...]) * (C_QK ** -0.5 * LOG2_E)
    kv = _dot(ckv, wkv_ref[...])
    for hd in range(C_HEADS):
        q0 = hd * (C_NOPE + 2 * C_ROPE)
        q_ref[0, hd, :, 0:C_NOPE] = qf[:, q0:q0 + C_NOPE].astype(BF16)
        q_ref[0, hd, :, C_NOPE:C_QK] = rope(qf[:, q0 + C_NOPE:q0 + C_NOPE + 2 * C_ROPE])[:, 0:C_ROPE].astype(BF16)
        k0 = hd * (C_NOPE + C_V)
        k_ref[0, hd, :, 0:C_NOPE] = kv[:, k0:k0 + C_NOPE].astype(BF16)
        k_ref[0, hd, :, C_NOPE:C_QK] = k_pe
        v_ref[0, hd, :, :] = kv[:, k0 + C_NOPE:k0 + C_NOPE + C_V].astype(BF16)


def _mla_up(x, w_in, q_norm, kv_norm, w_uq, w_ukv, cs, bsz, seq, tm):
    n_s = seq // tm
    qk_shape = jax.ShapeDtypeStruct((bsz, C_HEADS, seq, C_QK), BF16)
    return pl.pallas_call(
        _mla_up_kernel,
        grid=(bsz, n_s),
        in_specs=[pl.BlockSpec((tm, D_MODEL), lambda b, s: (b * n_s + s, 0)),
                  pl.BlockSpec(w_in.shape, lambda b, s: (0, 0)),
                  pl.BlockSpec((1, Q_LORA), lambda b, s: (0, 0)),
                  pl.BlockSpec((1, KV_LORA), lambda b, s: (0, 0)),
                  pl.BlockSpec(w_uq.shape, lambda b, s: (0, 0)),
                  pl.BlockSpec(w_ukv.shape, lambda b, s: (0, 0)),
                  pl.BlockSpec((tm, LANES), lambda b, s: (s, 0))],
        out_specs=[pl.BlockSpec((1, C_HEADS, tm, C_QK), lambda b, s: (b, 0, s, 0)),
                   pl.BlockSpec((1, C_HEADS, tm, C_QK), lambda b, s: (b, 0, s, 0)),
                   pl.BlockSpec((1, C_HEADS, tm, C_V), lambda b, s: (b, 0, s, 0))],
        out_shape=[qk_shape, qk_shape, jax.ShapeDtypeStruct((bsz, C_HEADS, seq, C_V), BF16)],
        compiler_params=_params("parallel", "parallel"),
        name="mla_up",
    )(x, w_in, q_norm, kv_norm, w_uq, w_ukv, cs)


def _attn_kernel(q_ref, k_ref, v_ref, o_ref, s_scr, p_scr, m_scr, l_scr, a_scr, acc_scr):
    nh, tq = q_ref.shape[1], q_ref.shape[2]
    n_lt = tq // LANES
    qi = pl.program_id(2)

    def scores(hd, k0):
        s_scr[hd] = _dot_nt(q_ref[0, hd], k_ref[0, hd, pl.ds(k0, tq), :])

    def softmax(hd, first):
        for r0 in range(0, tq, ATTN_SUB):
            rows = slice(r0, r0 + ATTN_SUB)
            n_vis = -(-(r0 + ATTN_SUB) // LANES) if first else n_lt
            tiles = []
            for c in range(n_vis):
                s = s_scr[hd, rows, c * LANES:(c + 1) * LANES]
                if first and ((c + 1) * LANES - 1) // CHUNK > r0 // CHUNK:
                    ri = jnp.right_shift(r0 + lax.broadcasted_iota(jnp.int32, (ATTN_SUB, LANES), 0), CHUNK_SHIFT)
                    ci = jnp.right_shift(c * LANES + lax.broadcasted_iota(jnp.int32, (ATTN_SUB, LANES), 1),
                                         CHUNK_SHIFT)
                    s = jnp.where(ci <= ri, s, NEG_BIG)
                tiles.append(s)
            mx = tiles[0]
            for c in range(1, n_vis):
                mx = jnp.maximum(mx, tiles[c])
            m_new = jnp.broadcast_to(jnp.max(mx, axis=-1, keepdims=True), (ATTN_SUB, LANES))
            if not first:
                m_prev = m_scr[hd, rows, :]
                m_new = jnp.maximum(m_prev, m_new)
                alpha = jnp.exp2(m_prev - m_new)
                a_scr[hd, rows, :] = alpha
            psum = jnp.zeros((ATTN_SUB, LANES), F32)
            for c in range(n_vis):
                p = jnp.exp2(tiles[c] - m_new)
                psum = psum + p
                p_scr[hd, rows, c * LANES:(c + 1) * LANES] = p.astype(BF16)
            for c in range(n_vis, n_lt):
                p_scr[hd, rows, c * LANES:(c + 1) * LANES] = jnp.zeros((ATTN_SUB, LANES), BF16)
            m_scr[hd, rows, :] = m_new
            l_new = jnp.broadcast_to(jnp.sum(psum, axis=-1, keepdims=True), (ATTN_SUB, LANES))
            l_scr[hd, rows, :] = l_new if first else alpha * l_scr[hd, rows, :] + l_new

    def values(hd, k0, first):
        pv = _dot(p_scr[hd], v_ref[0, hd, pl.ds(k0, tq), :])
        acc_scr[hd] = pv if first else a_scr[hd] * acc_scr[hd] + pv

    def block(k0, first):
        for hd in range(nh):
            scores(hd, k0)
        for hd in range(nh):
            softmax(hd, first)
        for hd in range(nh):
            values(hd, k0, first)

    def body(j, carry):
        block(pl.multiple_of(j * tq, tq), False)
        return carry

    block(pl.multiple_of(qi * tq, tq), True)
    lax.fori_loop(0, qi, body, 0)
    for hd in range(nh):
        o_ref[0, :, hd * C_V:(hd + 1) * C_V] = (acc_scr[hd] / l_scr[hd]).astype(o_ref.dtype)


def _attn(q, k, v):
    bsz, _, seq, _ = q.shape
    tq = _row_tile(seq, ATTN_ROWS)
    nh = ATTN_HEADS_PER_STEP
    return pl.pallas_call(
        _attn_kernel,
        grid=(bsz, C_HEADS // nh, seq // tq),
        in_specs=[pl.BlockSpec((1, nh, tq, C_QK), lambda b, h, i: (b, h, i, 0)),
                  pl.BlockSpec((1, nh, seq, C_QK), lambda b, h, i: (b, h, 0, 0)),
                  pl.BlockSpec((1, nh, seq, C_V), lambda b, h, i: (b, h, 0, 0))],
        out_specs=pl.BlockSpec((1, tq, nh * C_V), lambda b, h, i: (b, i, h)),
        out_shape=jax.ShapeDtypeStruct((bsz, seq, C_HEADS * C_V), BF16),
        scratch_shapes=[pltpu.VMEM((nh, tq, tq), F32),
                        pltpu.VMEM((nh, tq, tq), BF16),
                        pltpu.VMEM((nh, tq, LANES), F32),
                        pltpu.VMEM((nh, tq, LANES), F32),
                        pltpu.VMEM((nh, tq, LANES), F32),
                        pltpu.VMEM((nh, tq, C_V), F32)],
        compiler_params=_params("parallel", "parallel", "arbitrary"),
        name="attn",
    )(q, k, v)


def _proj_ln_router_kernel(n_in, *refs):
    a_refs = refs[0:n_in]
    w_refs = refs[n_in:2 * n_in]
    x_ref, g_ref, b_ref, wr_ref, br_ref = refs[2 * n_in:2 * n_in + 5]
    x1_ref, x1b_ref, meta_ref, gate_ref, cnt_ref = refs[2 * n_in + 5:2 * n_in + 10]
    base_scr = refs[2 * n_in + 10]
    tm = x_ref.shape[0]
    i = pl.program_id(0)

    @pl.when(i == 0)
    def _():
        base_scr[...] = jnp.zeros(base_scr.shape, F32)

    mix = _dot(a_refs[0][...], w_refs[0][...])
    for j in range(1, n_in):
        mix = mix + _dot(a_refs[j][...], w_refs[j][...])
    x1 = _layer_norm_rows(DEEPNORM_ALPHA * x_ref[...] + mix, g_ref[...], b_ref[...])
    x1_ref[...] = x1
    x1b = x1.astype(BF16)
    x1b_ref[...] = x1b

    work = _dot_nt(wr_ref[...], x1b) + br_ref[:, 0:1]
    eidx = lax.broadcasted_iota(jnp.int32, (N_EXPERTS, tm), 0).astype(F32)
    cnt = jnp.zeros((N_EXPERTS, tm), F32)
    tops, idxs, hots = [], [], []
    for _ in range(TOP_K):
        mx = jnp.max(work, axis=0, keepdims=True)
        idx = jnp.min(jnp.where(work == mx, eidx, float(N_EXPERTS)), axis=0, keepdims=True)
        hot = eidx == idx
        work = jnp.where(hot, NEG_BIG, work)
        cnt = cnt + jnp.where(hot, 1.0, 0.0)
        tops.append(mx)
        idxs.append(idx)
        hots.append(hot)
    exps = [jnp.exp(t - tops[0]) for t in tops]
    denom = exps[0] + exps[1] + exps[2] + exps[3]

    ri = lax.broadcasted_iota(jnp.int32, (tm, tm), 0)
    ci = lax.broadcasted_iota(jnp.int32, (tm, tm), 1)
    before = jnp.where(ri < ci, 1.0, 0.0).astype(BF16)
    base = base_scr[:, 0:1]
    rank = _dot(cnt.astype(BF16), before) + base
    row = lax.broadcasted_iota(jnp.int32, (2 * TOP_K, tm), 0)
    meta = jnp.zeros((2 * TOP_K, tm), jnp.int32)
    gates = jnp.zeros((2 * TOP_K, tm), F32)
    for kk in range(TOP_K):
        pos = jnp.sum(jnp.where(hots[kk], rank, 0.0), axis=0, keepdims=True).astype(jnp.int32)
        meta = jnp.where(row == kk, idxs[kk].astype(jnp.int32), meta)
        meta = jnp.where(row == TOP_K + kk, pos, meta)
        gates = jnp.where(row == kk, exps[kk] / denom, gates)
    meta_ref[...] = meta
    gate_ref[...] = gates
    total = base + jnp.sum(cnt, axis=1, keepdims=True)
    base_scr[...] = jnp.broadcast_to(total, base_scr.shape)
    cnt_ref[...] = jnp.broadcast_to(total, cnt_ref.shape)


def _proj_ln_router(a_list, w_list, x, ln_g, ln_b, w_r, b_r, tm):
    t = x.shape[0]
    n_in = len(a_list)
    in_specs = [pl.BlockSpec((tm, a.shape[1]), lambda i: (i, 0)) for a in a_list]
    in_specs += [pl.BlockSpec(w.shape, lambda i: (0, 0)) for w in w_list]
    in_specs += [pl.BlockSpec((tm, D_MODEL), lambda i: (i, 0)),
                 pl.BlockSpec((1, D_MODEL), lambda i: (0, 0)),
                 pl.BlockSpec((1, D_MODEL), lambda i: (0, 0)),
                 pl.BlockSpec((N_EXPERTS, D_MODEL), lambda i: (0, 0)),
                 pl.BlockSpec((N_EXPERTS, LANES), lambda i: (0, 0))]
    return pl.pallas_call(
        functools.partial(_proj_ln_router_kernel, n_in),
        grid=(t // tm,),
        in_specs=in_specs,
        out_specs=[pl.BlockSpec((tm, D_MODEL), lambda i: (i, 0)),
                   pl.BlockSpec((tm, D_MODEL), lambda i: (i, 0)),
                   pl.BlockSpec((2 * TOP_K, tm), lambda i: (0, i)),
                   pl.BlockSpec((2 * TOP_K, tm), lambda i: (0, i)),
                   pl.BlockSpec((N_EXPERTS, LANES), lambda i: (0, 0))],
        out_shape=[jax.ShapeDtypeStruct((t, D_MODEL), F32),
                   jax.ShapeDtypeStruct((t, D_MODEL), BF16),
                   jax.ShapeDtypeStruct((2 * TOP_K, t), jnp.int32),
                   jax.ShapeDtypeStruct((2 * TOP_K, t), F32),
                   jax.ShapeDtypeStruct((N_EXPERTS, LANES), F32)],
        scratch_shapes=[pltpu.VMEM((N_EXPERTS, LANES), F32)],
        compiler_params=_params("arbitrary"),
        name="proj_ln_router",
    )(*a_list, *w_list, x, ln_g, ln_b, w_r, b_r)


def _moe_kernel(blk_ref, exp_ref, lo_ref, hi_ref, x_ref, wgu_ref, wd_ref, bg_ref, bu_ref, bd_ref, o_ref,
                wg_scr, wu_scr, wd_scr):
    i = pl.program_id(0)
    tm = x_ref.shape[0]
    blk = blk_ref[i]
    lo = lo_ref[i]
    hi = hi_ref[i]
    prev = jnp.maximum(i - 1, 0)
    first = jnp.logical_or(i == 0, blk != blk_ref[prev])
    new_expert = jnp.logical_or(i == 0, exp_ref[i] != exp_ref[prev])

    @pl.when(new_expert)
    def _():
        pc = MOE_PREP_COLS
        half = pc // 2
        for c in range(2 * D_FF // pc):
            for kc in range(D_MODEL // LANES):
                k0 = kc * LANES
                tb = wgu_ref[0, 0, k0:k0 + LANES, c * pc:(c + 1) * pc].astype(BF16).T
                pair = pltpu.bitcast(tb, jnp.uint32)
                gate = pltpu.bitcast(jnp.left_shift(pair, 16), F32)
                up = pltpu.bitcast(jnp.bitwise_and(pair, jnp.uint32(0xFFFF0000)), F32)
                wg_scr[c * half:(c + 1) * half, k0:k0 + LANES] = gate.astype(BF16)
                wu_scr[c * half:(c + 1) * half, k0:k0 + LANES] = up.astype(BF16)
        wd_scr[...] = wd_ref[0, 0].astype(BF16)

    def ffn(x):
        glu = jnp.minimum(_dot_nt(x, wg_scr[...]) + bg_ref[0, 0], SWIGLU_LIMIT)
        lin = jnp.clip(_dot_nt(x, wu_scr[...]) + bu_ref[0, 0], -SWIGLU_LIMIT, SWIGLU_LIMIT)
        act = glu * _sigmoid(SWIGLU_ALPHA * glu) * (lin + 1.0)
        return (_dot(act.astype(BF16), wd_scr[...]) + bd_ref[0, 0]).astype(o_ref.dtype)

    full = (hi - lo) == tm

    @pl.when(full)
    def _():
        o_ref[...] = ffn(x_ref[...])

    @pl.when(jnp.logical_and(first, jnp.logical_not(full)))
    def _():
        o_ref[...] = jnp.zeros(o_ref.shape, o_ref.dtype)

    for r0 in range(0, tm, MOE_PART):
        row0 = blk * tm + r0
        touched = jnp.logical_and(hi > lo, jnp.logical_and(lo < row0 + MOE_PART, hi > row0))

        @pl.when(jnp.logical_and(jnp.logical_not(full), touched))
        def _():
            y = ffn(x_ref[r0:r0 + MOE_PART, :])
            row = row0 + lax.broadcasted_iota(jnp.int32, (MOE_PART, 1), 0)
            keep = jnp.logical_and(row >= lo, row < hi)
            o_ref[r0:r0 + MOE_PART, :] = jnp.where(keep, y, o_ref[r0:r0 + MOE_PART, :])


def _moe_sorted(layer, xs, counts, w_gate_up, b_gate, b_up, w_down, b_down):
    n_rows = xs.shape[0]
    tm = _row_tile(n_rows, MOE_ROWS)
    items = _moe_items(counts, n_rows, tm)
    n_items = items[0].shape[0]

    def by_expert(i, blk, ex, lo, hi):
        return (layer, ex[i], 0, 0)

    def by_block(i, blk, ex, lo, hi):
        return (blk[i], 0)

    grid_spec = pltpu.PrefetchScalarGridSpec(
        num_scalar_prefetch=4,
        grid=(n_items,),
        in_specs=[pl.BlockSpec((tm, D_MODEL), by_block),
                  pl.BlockSpec((1, 1, D_MODEL, 2 * D_FF), by_expert),
                  pl.BlockSpec((1, 1, D_FF, D_MODEL), by_expert),
                  pl.BlockSpec((1, 1, 1, D_FF), by_expert),
                  pl.BlockSpec((1, 1, 1, D_FF), by_expert),
                  pl.BlockSpec((1, 1, 1, D_MODEL), by_expert)],
        out_specs=pl.BlockSpec((tm, D_MODEL), by_block),
        scratch_shapes=[pltpu.VMEM((D_FF, D_MODEL), BF16),
                        pltpu.VMEM((D_FF, D_MODEL), BF16),
                        pltpu.VMEM((D_FF, D_MODEL), BF16)],
    )
    return pl.pallas_call(
        _moe_kernel,
        grid_spec=grid_spec,
        out_shape=jax.ShapeDtypeStruct((n_rows, D_MODEL), BF16),
        compiler_params=_params("arbitrary", vmem=MOE_VMEM_LIMIT),
        name="moe_ffn",
    )(*items, xs, w_gate_up, w_down, b_gate, b_up, b_down)


def _moe_items(counts, n_rows, tm):
    n_blocks = n_rows // tm
    start = jnp.cumsum(counts) - counts
    cuts = jnp.sort(jnp.concatenate([jnp.arange(n_blocks, dtype=jnp.int32) * tm, start.astype(jnp.int32)]))
    lo = cuts
    hi = jnp.concatenate([cuts[1:], jnp.array([n_rows], jnp.int32)])
    blk = jnp.minimum(lo // tm, n_blocks - 1)
    ex = jnp.clip(jnp.sum((start[None, :] <= lo[:, None]).astype(jnp.int32), axis=1) - 1, 0, N_EXPERTS - 1)
    return blk.astype(jnp.int32), ex.astype(jnp.int32), lo.astype(jnp.int32), hi.astype(jnp.int32)


def _combine_ln_kernel(x_ref, y_ref, gate_ref, g_ref, b_ref, o_ref):
    n_g = gate_ref.shape[0]
    eye = jnp.where(lax.broadcasted_iota(jnp.int32, (n_g, n_g), 0) == lax.broadcasted_iota(jnp.int32, (n_g, n_g), 1),
                    1.0, 0.0).astype(F32)
    gates = lax.dot_general(gate_ref[...], eye, (((0,), (0,)), ((), ())), preferred_element_type=F32,
                            precision=lax.Precision.HIGHEST)
    ffn = y_ref[0].astype(F32) * gates[:, 0:1]
    for kk in range(1, TOP_K):
        ffn = ffn + y_ref[kk].astype(F32) * gates[:, kk:kk + 1]
    o_ref[...] = _layer_norm_rows(DEEPNORM_ALPHA * x_ref[...] + ffn, g_ref[...], b_ref[...])


def _combine_ln(x1, y4, gates, ln_g, ln_b, tm):
    t = x1.shape[0]
    return pl.pallas_call(
        _combine_ln_kernel,
        grid=(t // tm,),
        in_specs=[pl.BlockSpec((tm, D_MODEL), lambda i: (i, 0)),
                  pl.BlockSpec((TOP_K, tm, D_MODEL), lambda i: (0, i, 0)),
                  pl.BlockSpec((2 * TOP_K, tm), lambda i: (0, i)),
                  pl.BlockSpec((1, D_MODEL), lambda i: (0, 0)),
                  pl.BlockSpec((1, D_MODEL), lambda i: (0, 0))],
        out_specs=pl.BlockSpec((tm, D_MODEL), lambda i: (i, 0)),
        out_shape=jax.ShapeDtypeStruct((t, D_MODEL), F32),
        compiler_params=_params("parallel"),
        name="combine_ln",
    )(x1, y4, gates, ln_g, ln_b)


def _moe_layer(layer, x1, x1b, meta_t, gates_t, counts, ln_g, ln_b, w_gate_up, b_gate, b_up, w_down, b_down):
    t = x1.shape[0]
    assert N_EXPERTS * t < 2 ** 31
    e = meta_t[0:TOP_K]
    pos = meta_t[TOP_K:2 * TOP_K]
    cnt = counts[:, 0].astype(jnp.int32)
    start = jnp.cumsum(cnt) - cnt
    experts = jnp.arange(N_EXPERTS, dtype=jnp.int32)
    dest = pos + jnp.sum(jnp.where(e[:, :, None] == experts, start, 0), axis=-1)
    keys = e * t + jnp.arange(t, dtype=jnp.int32)[None, :]
    tok_sorted = jnp.remainder(jnp.sort(keys.reshape(-1)), t)
    xs = x1b[tok_sorted]
    ys = _moe_sorted(layer, xs, cnt, w_gate_up, b_gate, b_up, w_down, b_down)
    y4 = ys[dest]
    return _combine_ln(x1, y4, gates_t, ln_g, ln_b, _row_tile(t, ROWS))


def kernel(x, even_w_in, even_sgu_ln_g, even_sgu_ln_b, even_sgu_w, even_sgu_b, even_conv_w, even_a_log, even_dt_bias, even_gdn_norm, even_w_out, odd_w_in, odd_q_norm, odd_w_uq, odd_kv_norm, odd_w_ukv, odd_w_out, ln_mix_g, ln_mix_b, ln_ffn_g, ln_ffn_b, moe_w_router, moe_b_router, moe_w_gate_up, moe_b_gate_up, moe_w_down, moe_b_down):
    bsz, seq, d = x.shape
    t = bsz * seq
    xt = x.reshape(t, d)
    tm = _row_tile(t, ROWS)
    b_gate = moe_b_gate_up[:, :, None, 0::2]
    b_up = moe_b_gate_up[:, :, None, 1::2]
    b_down = moe_b_down[:, :, None, :]

    inv_freq = ROPE_THETA ** (-jnp.arange(0, C_ROPE, 2, dtype=F32) / C_ROPE)
    ang = jnp.arange(seq, dtype=F32)[:, None] * inv_freq[None, :]
    cs = jnp.concatenate([jnp.cos(ang), jnp.cos(ang), jnp.sin(ang), jnp.sin(ang)], axis=1)

    pos_chunk = np.arange(A_BLOCK) // CHUNK
    allowed = jnp.asarray(pos_chunk[None, :] <= pos_chunk[:, None])

    def rot_cols(w):
        half = C_ROPE // 2
        return jnp.concatenate([-w[..., half:], w[..., :half]], axis=-1)

    for layer in range(DEPTH):
        i = layer // 2
        if layer % 2 == 0:
            w_in = even_w_in[i]
            w_all = jnp.pad(w_in, ((0, 0), (0, LANES - 2 * B_HEADS))).astype(BF16)
            sw = jnp.where(allowed[None], even_sgu_w[i], 0).astype(BF16)
            sb = jnp.broadcast_to(even_sgu_b[i][:, :, None], (A_GROUPS, A_BLOCK, A_GROUP_DIM)).astype(F32)
            y_a, h_b, ab = _even_in(xt, w_all, even_sgu_ln_g[i][None], even_sgu_ln_b[i][None], sw, sb, tm)
            abt = ab.reshape(bsz, seq, LANES)[:, :, 0:SUBLANES].transpose(0, 2, 1)
            hp = jnp.pad(jnp.stack([even_a_log[i], even_dt_bias[i]]), ((0, 0), (0, LANES - B_HEADS)))
            hpt = jnp.pad(jnp.stack([even_a_log[i], even_dt_bias[i]], axis=1), ((0, SUBLANES - B_HEADS), (0, 0)))
            y_b = _gdn(h_b, ab, abt, even_conv_w[i], hp, hpt, even_gdn_norm[i][None], bsz, seq)
            w_out = even_w_out[i].astype(BF16)
            a_list = [y_a, y_b]
            w_list = [w_out[0:A_WIDTH], w_out[A_WIDTH:]]
        else:
            w_in = odd_w_in[i]
            w_pe = w_in[:, Q_LORA + KV_LORA:]
            w_in2 = jnp.concatenate([w_in, rot_cols(w_pe)], axis=1).astype(BF16)
            w_uq = odd_w_uq[i].reshape(Q_LORA, C_HEADS, C_QK)
            w_uq2 = jnp.concatenate([w_uq, rot_cols(w_uq[..., C_NOPE:])], axis=-1)
            w_uq2 = w_uq2.reshape(Q_LORA, C_HEADS * (C_NOPE + 2 * C_ROPE)).astype(BF16)
            q, k, v = _mla_up(xt, w_in2, odd_q_norm[i][None], odd_kv_norm[i][None], w_uq2,
                              odd_w_ukv[i].astype(BF16), cs, bsz, seq, _row_tile(seq, ROWS))
            o = _attn(q, k, v)
            a_list = [o.reshape(t, C_HEADS * C_V)]
            w_list = [odd_w_out[i].astype(BF16)]

        w_r = moe_w_router[layer].T.astype(BF16)
        b_r = jnp.broadcast_to(moe_b_router[layer][:, None], (N_EXPERTS, LANES))
        x1, x1b, meta, gates, counts = _proj_ln_router(a_list, w_list, xt, ln_mix_g[layer][None],
                                                       ln_mix_b[layer][None], w_r, b_r, tm)
        xt = _moe_layer(layer, x1, x1b, meta, gates, counts, ln_ffn_g[layer][None], ln_ffn_b[layer][None],
                        moe_w_gate_up, b_gate, b_up, moe_w_down, b_down)
    return xt.reshape(bsz, seq, d)
```

```python
import functools

import jax
import jax.numpy as jnp
import numpy as np
from jax import lax
from jax.experimental import pallas as pl
from jax.experimental.pallas import tpu as pltpu

F32 = jnp.float32
BF16 = jnp.bfloat16

D_MODEL = 1024
DEPTH = 4
CHUNK = 64
CHUNK_SHIFT = CHUNK.bit_length() - 1
DEEPNORM_ALPHA = (2 * DEPTH) ** 0.25
LN_EPS = 1e-5
RMS_EPS = 1e-6

A_BLOCK = 128
A_GROUPS = 4
A_WIDTH = D_MODEL // 2
A_GROUP_DIM = A_WIDTH // A_GROUPS

B_HEADS = 4
B_HEAD_DIM = D_MODEL // 8
B_WIDTH = B_HEADS * B_HEAD_DIM
CONV_K = 4

C_HEADS = 8
C_NOPE = 128
C_ROPE = 64
C_V = D_MODEL // C_HEADS
C_QK = C_NOPE + C_ROPE
Q_LORA = 3 * D_MODEL // 8
KV_LORA = D_MODEL // 4
ROPE_THETA = 10000.0

N_EXPERTS = 32
TOP_K = 4
D_FF = D_MODEL
SWIGLU_LIMIT = 7.0
SWIGLU_ALPHA = 1.702

LANES = 128
SUBLANES = 8
VMEM_LIMIT = 48 * 1024 * 1024
MOE_VMEM_LIMIT = 58 * 1024 * 1024
NEG_BIG = -1e30
LOG2_E = 1.4426950408889634

ROWS = 512
GDN_ROWS = 256
GDN_BATCH = 2
INV_BASE_SHIFT = 1
ATTN_ROWS = 512
ATTN_HEADS_PER_STEP = 4
ATTN_SUB = 128
MOE_ROWS = 512
MOE_PART = 256
MOE_PREP_COLS = 512


def _row_tile(t, pref):
    tm = pref
    while t % tm:
        tm //= 2
    return tm


def _params(*sem, vmem=VMEM_LIMIT):
    return pltpu.CompilerParams(dimension_semantics=sem, vmem_limit_bytes=vmem)


def _dot(a, b):
    return jnp.dot(a, b, preferred_element_type=F32)


def _dot_nt(a, b):
    return lax.dot_general(a, b, (((1,), (1,)), ((), ())), preferred_element_type=F32)


def _dot_tn(a, b):
    return lax.dot_general(a, b, (((0,), (0,)), ((), ())), preferred_element_type=F32)


def _sigmoid(x):
    return 1.0 / (1.0 + jnp.exp(-x))


def _softplus(x):
    return jnp.maximum(x, 0.0) + jnp.log1p(jnp.exp(-jnp.abs(x)))


def _layer_norm_rows(y, g, b):
    mu = jnp.mean(y, axis=-1, keepdims=True)
    d = y - mu
    var = jnp.mean(d * d, axis=-1, keepdims=True)
    return d * lax.rsqrt(var + LN_EPS) * g + b


def _gelu(x):
    return 0.5 * x * (1.0 + lax.erf(x * (2.0 ** -0.5)))


def _combine_rows(x_ref, y_ref, gate_ref, g_ref, b_ref):
    n_g = gate_ref.shape[0]
    eye = jnp.where(lax.broadcasted_iota(jnp.int32, (n_g, n_g), 0) == lax.broadcasted_iota(jnp.int32, (n_g, n_g), 1),
                    1.0, 0.0).astype(F32)
    gates = lax.dot_general(gate_ref[...], eye, (((0,), (0,)), ((), ())), preferred_element_type=F32,
                            precision=lax.Precision.HIGHEST)
    ffn = y_ref[0].astype(F32) * gates[:, 0:1]
    for kk in range(1, TOP_K):
        ffn = ffn + y_ref[kk].astype(F32) * gates[:, kk:kk + 1]
    return _layer_norm_rows(DEEPNORM_ALPHA * x_ref[...] + ffn, g_ref[...], b_ref[...])


def _combine_specs(tm, row_block):
    return [pl.BlockSpec((tm, D_MODEL), lambda *g: (row_block(*g), 0)),
            pl.BlockSpec((TOP_K, tm, D_MODEL), lambda *g: (0, row_block(*g), 0)),
            pl.BlockSpec((2 * TOP_K, tm), lambda *g: (0, row_block(*g))),
            pl.BlockSpec((1, D_MODEL), lambda *g: (0, 0)),
            pl.BlockSpec((1, D_MODEL), lambda *g: (0, 0))]


def _even_in_kernel(fused, *refs):
    if fused:
        w_ref, g_ref, b_ref, sw_ref, sb_ref, x2_ref, o_ref, hb_ref, ab_ref = refs[5:]
        x = _combine_rows(*refs[0:5])
        x2_ref[...] = x
    else:
        x_ref, w_ref, g_ref, b_ref, sw_ref, sb_ref, o_ref, hb_ref, ab_ref = refs
        x = x_ref[...]
    tm = x.shape[0]
    h = _dot(x.astype(BF16), w_ref[...])
    n_b = hb_ref.shape[1]
    hb_ref[...] = h[:, 2 * A_WIDTH:2 * A_WIDTH + n_b].astype(hb_ref.dtype)
    ab_ref[...] = h[:, 2 * A_WIDTH + n_b:]
    for grp in range(A_GROUPS):
        c0 = grp * A_GROUP_DIM
        u = _gelu(h[:, c0:c0 + A_GROUP_DIM])
        v = _gelu(h[:, A_WIDTH + c0:A_WIDTH + c0 + A_GROUP_DIM])
        vn = _layer_norm_rows(v, g_ref[:, c0:c0 + A_GROUP_DIM], b_ref[:, c0:c0 + A_GROUP_DIM]).astype(BF16)
        for blk in range(tm // A_BLOCK):
            r0 = blk * A_BLOCK
            mixed = _dot(sw_ref[grp], vn[r0:r0 + A_BLOCK]) + sb_ref[grp]
            o_ref[r0:r0 + A_BLOCK, c0:c0 + A_GROUP_DIM] = (u[r0:r0 + A_BLOCK] * mixed).astype(o_ref.dtype)


def _even_in(x_or_pending, w_all, ln_g, ln_b, sw, sb, t, tm):
    fused = isinstance(x_or_pending, tuple)
    n_all = w_all.shape[1]
    x_args = list(x_or_pending) if fused else [x_or_pending]
    x_specs = _combine_specs(tm, lambda i: i) if fused else [pl.BlockSpec((tm, D_MODEL), lambda i: (i, 0))]
    out_specs = [pl.BlockSpec((tm, A_WIDTH), lambda i: (i, 0)),
                 pl.BlockSpec((tm, 4 * B_WIDTH), lambda i: (i, 0)),
                 pl.BlockSpec((tm, LANES), lambda i: (i, 0))]
    out_shape = [jax.ShapeDtypeStruct((t, A_WIDTH), BF16),
                 jax.ShapeDtypeStruct((t, 4 * B_WIDTH), BF16),
                 jax.ShapeDtypeStruct((t, LANES), F32)]
    if fused:
        out_specs = [pl.BlockSpec((tm, D_MODEL), lambda i: (i, 0))] + out_specs
        out_shape = [jax.ShapeDtypeStruct((t, D_MODEL), F32)] + out_shape
    outs = pl.pallas_call(
        functools.partial(_even_in_kernel, fused),
        grid=(t // tm,),
        in_specs=x_specs + [pl.BlockSpec((D_MODEL, n_all), lambda i: (0, 0)),
                            pl.BlockSpec((1, A_WIDTH), lambda i: (0, 0)),
                            pl.BlockSpec((1, A_WIDTH), lambda i: (0, 0)),
                            pl.BlockSpec((A_GROUPS, A_BLOCK, A_BLOCK), lambda i: (0, 0, 0)),
                            pl.BlockSpec((A_GROUPS, A_BLOCK, A_GROUP_DIM), lambda i: (0, 0, 0))],
        out_specs=out_specs,
        out_shape=out_shape,
        compiler_params=_params("parallel"),
        name="even_in",
    )(*x_args, w_all, ln_g, ln_b, sw, sb)
    return tuple(outs) if fused else (x_or_pending, *outs)


def _gdn_kernel(h_ref, ab_ref, abt_ref, cw_ref, hp_ref, hpt_ref, nw_ref, o_ref, conv_scr, state_scr):
    nb, tr = h_ref.shape[0], h_ref.shape[1]
    n_chunks = tr // CHUNK
    s_idx = pl.program_id(1)

    @pl.when(s_idx == 0)
    def _():
        conv_scr[:, 0:SUBLANES, :] = jnp.zeros((nb, SUBLANES, 3 * B_WIDTH), F32)
        state_scr[...] = jnp.zeros(state_scr.shape, F32)

    ri = lax.broadcasted_iota(jnp.int32, (tr, tr), 0)
    ci = lax.broadcasted_iota(jnp.int32, (tr, tr), 1)
    same = jnp.right_shift(ri, CHUNK_SHIFT) == jnp.right_shift(ci, CHUNK_SHIFT)
    lower = jnp.where(same & (ri >= ci), 1.0, 0.0).astype(F32)
    upper = jnp.where(same & (ri <= ci), 1.0, 0.0).astype(F32)
    incl = same & (ri >= ci)
    strict = same & (ri > ci)
    eye = jnp.where(ri == ci, 1.0, 0.0).astype(F32)

    chains = [(bi, hd) for bi in range(nb) for hd in range(B_HEADS)]
    qs, ks, vs, betas, gccs, decays, k_bs, kbs, ps, ts, a_mats = {}, {}, {}, {}, {}, {}, {}, {}, {}, {}, {}
    us, ws, intras, q_decs = {}, {}, {}, {}
    for bi in range(nb):
        conv_scr[bi, SUBLANES:SUBLANES + tr, :] = h_ref[bi, :, 0:3 * B_WIDTH].astype(F32)
        acc = conv_scr[bi, SUBLANES:SUBLANES + tr, :] * cw_ref[CONV_K - 1:CONV_K, :]
        for d in range(1, CONV_K):
            acc = acc + conv_scr[bi, SUBLANES - d:SUBLANES - d + tr, :] * cw_ref[CONV_K - 1 - d:CONV_K - d, :]
        conv_scr[bi, 0:SUBLANES, :] = conv_scr[bi, tr:tr + SUBLANES, :]
        qkv = acc * _sigmoid(acc)

        ab = ab_ref[bi]
        sp_c = _softplus(ab + hp_ref[1:2, :])
        g_col = -jnp.exp(hp_ref[0:1, :]) * sp_c
        beta_col = _sigmoid(ab)
        abt = abt_ref[bi]
        sp_r = _softplus(abt + hpt_ref[:, 1:2])
        g_row = -jnp.exp(hpt_ref[:, 0:1]) * sp_r
        gc_col = jnp.dot(lower, g_col, preferred_element_type=F32, precision=lax.Precision.HIGHEST)
        gc_row = jnp.dot(g_row, upper, preferred_element_type=F32, precision=lax.Precision.HIGHEST)
        for hd in range(B_HEADS):
            ch = (bi, hd)
            l0 = hd * B_HEAD_DIM
            q = qkv[:, l0:l0 + B_HEAD_DIM]
            k = qkv[:, B_WIDTH + l0:B_WIDTH + l0 + B_HEAD_DIM]
            qs[ch] = q * lax.rsqrt(jnp.sum(q * q, axis=-1, keepdims=True) + RMS_EPS) * (B_HEAD_DIM ** -0.5)
            ks[ch] = k * lax.rsqrt(jnp.sum(k * k, axis=-1, keepdims=True) + RMS_EPS)
            vs[ch] = qkv[:, 2 * B_WIDTH + l0:2 * B_WIDTH + l0 + B_HEAD_DIM]
            betas[ch] = beta_col[:, B_HEADS + hd:B_HEADS + hd + 1]
            gccs[ch] = gc_col[:, hd:hd + 1]
            gcr = gc_row[hd:hd + 1, :]
            decays[ch] = jnp.where(incl, jnp.exp(jnp.where(incl, gccs[ch] - gcr, 0.0)), 0.0)
            k_bs[ch] = ks[ch].astype(BF16)
            kbs[ch] = ks[ch] * betas[ch]
    lvl = [jnp.right_shift(ri, s) == jnp.right_shift(ci, s) for s in range(INV_BASE_SHIFT, CHUNK_SHIFT + 1)]
    for ch in chains:
        a_mat = jnp.where(strict, _dot_nt(kbs[ch].astype(BF16), k_bs[ch]) * decays[ch], 0.0)
        a_mats[ch] = a_mat
        d_mat = jnp.where(lvl[0], a_mat, 0.0)
        ps[ch] = -d_mat
        ts[ch] = eye - d_mat
    for _ in range(INV_BASE_SHIFT - 1):
        for ch in chains:
            pb = ps[ch].astype(BF16)
            ps[ch] = _dot(pb, pb)
        for ch in chains:
            ts[ch] = ts[ch] + _dot(ts[ch].astype(BF16), ps[ch].astype(BF16))
    for j in range(1, len(lvl)):
        for ch in chains:
            join = jnp.where(lvl[j] & jnp.logical_not(lvl[j - 1]), a_mats[ch], 0.0).astype(BF16)
            t_b = ts[ch].astype(BF16)
            ps[ch] = _dot(_dot(t_b, join).astype(BF16), t_b)
        for ch in chains:
            ts[ch] = ts[ch] - ps[ch]
    for ch in chains:
        t_b = ts[ch].astype(BF16)
        e_gc = jnp.exp(gccs[ch])
        us[ch] = _dot(t_b, (vs[ch] * betas[ch]).astype(BF16))
        ws[ch] = _dot(t_b, (kbs[ch] * e_gc).astype(BF16)).astype(BF16)
        intras[ch] = jnp.where(incl, _dot_nt(qs[ch].astype(BF16), k_bs[ch]) * decays[ch], 0.0).astype(BF16)
        q_decs[ch] = (qs[ch] * e_gc).astype(BF16)
    for c in range(n_chunks):
        r0 = c * CHUNK
        for ch in chains:
            bi, hd = ch
            l0 = hd * B_HEAD_DIM
            g_last = gccs[ch][r0 + CHUNK - 1:r0 + CHUNK, :]
            k_dec = (ks[ch][r0:r0 + CHUNK] * jnp.exp(g_last - gccs[ch][r0:r0 + CHUNK])).astype(BF16)
            state = state_scr[bi * B_HEADS + hd]
            state_b = state.astype(BF16)
            v_new = (us[ch][r0:r0 + CHUNK] - _dot(ws[ch][r0:r0 + CHUNK], state_b)).astype(BF16)
            out = _dot(q_decs[ch][r0:r0 + CHUNK], state_b) + _dot(intras[ch][r0:r0 + CHUNK, r0:r0 + CHUNK], v_new)
            state_scr[bi * B_HEADS + hd] = state * jnp.exp(g_last) + _dot_tn(k_dec, v_new)
            z = h_ref[bi, r0:r0 + CHUNK, 3 * B_WIDTH + l0:3 * B_WIDTH + l0 + B_HEAD_DIM].astype(F32)
            o_n = out * lax.rsqrt(jnp.mean(out * out, axis=-1, keepdims=True) + RMS_EPS) * nw_ref[...]
            o_ref[bi, r0:r0 + CHUNK, l0:l0 + B_HEAD_DIM] = (o_n * (z * _sigmoid(z))).astype(o_ref.dtype)


def _gdn(h_b, ab, abt, conv_w, hp, hpt, norm_w, bsz, seq):
    tr = _row_tile(seq, GDN_ROWS)
    nb = GDN_BATCH if bsz % GDN_BATCH == 0 else 1
    n_s = seq // tr
    out = pl.pallas_call(
        _gdn_kernel,
        grid=(bsz // nb, n_s),
        in_specs=[pl.BlockSpec((nb, tr, 4 * B_WIDTH), lambda b, s: (b, s, 0)),
                  pl.BlockSpec((nb, tr, LANES), lambda b, s: (b, s, 0)),
                  pl.BlockSpec((nb, SUBLANES, tr), lambda b, s: (b, 0, s)),
                  pl.BlockSpec((CONV_K, 3 * B_WIDTH), lambda b, s: (0, 0)),
                  pl.BlockSpec((2, LANES), lambda b, s: (0, 0)),
                  pl.BlockSpec((SUBLANES, 2), lambda b, s: (0, 0)),
                  pl.BlockSpec((1, B_HEAD_DIM), lambda b, s: (0, 0))],
        out_specs=pl.BlockSpec((nb, tr, B_WIDTH), lambda b, s: (b, s, 0)),
        out_shape=jax.ShapeDtypeStruct((bsz, seq, B_WIDTH), BF16),
        scratch_shapes=[pltpu.VMEM((nb, tr + 2 * SUBLANES, 3 * B_WIDTH), F32),
                        pltpu.VMEM((nb * B_HEADS, B_HEAD_DIM, B_HEAD_DIM), F32)],
        compiler_params=_params("parallel", "arbitrary"),
        name="gdn",
    )(h_b.reshape(bsz, seq, 4 * B_WIDTH), ab.reshape(bsz, seq, LANES), abt, conv_w, hp, hpt, norm_w)
    return out.reshape(bsz * seq, B_WIDTH)


def _mla_up_kernel(fused, *refs):
    if fused:
        win_ref, qn_ref, kvn_ref, wq_ref, wkv_ref, cs_ref, x2_ref, q_ref, k_ref, v_ref = refs[5:]
        x = _combine_rows(*refs[0:5])
        x2_ref[...] = x
    else:
        x_ref, win_ref, qn_ref, kvn_ref, wq_ref, wkv_ref, cs_ref, q_ref, k_ref, v_ref = refs
        x = x_ref[...]
    h = _dot(x.astype(BF16), win_ref[...])
    cs = cs_ref[...]

    def rms(x, g):
        return x * lax.rsqrt(jnp.mean(x * x, axis=-1, keepdims=True) + RMS_EPS) * g

    def rope(t):
        r = t * cs
        return r + pltpu.roll(r, C_ROPE, 1)

    cq = rms(h[:, 0:Q_LORA], qn_ref[...]).astype(BF16)
    ckv = rms(h[:, Q_LORA:Q_LORA + KV_LORA], kvn_ref[...]).astype(BF16)
    k_pe = rope(h[:, Q_LORA + KV_LORA:Q_LORA + KV_LORA + 2 * C_ROPE])[:, 0:C_ROPE].astype(BF16)
    qf = _dot(cq, wq_ref[...]) * (C_QK ** -0.5 * LOG2_E)
    kv = _dot(ckv, wkv_ref[...])
    for hd in range(C_HEADS):
        q0 = hd * (C_NOPE + 2 * C_ROPE)
        q_ref[0, hd, :, 0:C_NOPE] = qf[:, q0:q0 + C_NOPE].astype(BF16)
        q_ref[0, hd, :, C_NOPE:C_QK] = rope(qf[:, q0 + C_NOPE:q0 + C_NOPE + 2 * C_ROPE])[:, 0:C_ROPE].astype(BF16)
        k0 = hd * (C_NOPE + C_V)
        k_ref[0, hd, :, 0:C_NOPE] = kv[:, k0:k0 + C_NOPE].astype(BF16)
        k_ref[0, hd, :, C_NOPE:C_QK] = k_pe
        v_ref[0, hd, :, :] = kv[:, k0 + C_NOPE:k0 + C_NOPE + C_V].astype(BF16)


def _mla_up(x_or_pending, w_in, q_norm, kv_norm, w_uq, w_ukv, cs, bsz, seq, tm):
    fused = isinstance(x_or_pending, tuple)
    n_s = seq // tm
    qk_shape = jax.ShapeDtypeStruct((bsz, C_HEADS, seq, C_QK), BF16)
    x_args = list(x_or_pending) if fused else [x_or_pending]
    x_specs = (_combine_specs(tm, lambda b, s: b * n_s + s) if fused
               else [pl.BlockSpec((tm, D_MODEL), lambda b, s: (b * n_s + s, 0))])
    out_specs = [pl.BlockSpec((1, C_HEADS, tm, C_QK), lambda b, s: (b, 0, s, 0)),
                 pl.BlockSpec((1, C_HEADS, tm, C_QK), lambda b, s: (b, 0, s, 0)),
                 pl.BlockSpec((1, C_HEADS, tm, C_V), lambda b, s: (b, 0, s, 0))]
    out_shape = [qk_shape, qk_shape, jax.ShapeDtypeStruct((bsz, C_HEADS, seq, C_V), BF16)]
    if fused:
        out_specs = [pl.BlockSpec((tm, D_MODEL), lambda b, s: (b * n_s + s, 0))] + out_specs
        out_shape = [jax.ShapeDtypeStruct((bsz * seq, D_MODEL), F32)] + out_shape
    outs = pl.pallas_call(
        functools.partial(_mla_up_kernel, fused),
        grid=(bsz, n_s),
        in_specs=x_specs + [pl.BlockSpec(w_in.shape, lambda b, s: (0, 0)),
                            pl.BlockSpec((1, Q_LORA), lambda b, s: (0, 0)),
                            pl.BlockSpec((1, KV_LORA), lambda b, s: (0, 0)),
                            pl.BlockSpec(w_uq.shape, lambda b, s: (0, 0)),
                            pl.BlockSpec(w_ukv.shape, lambda b, s: (0, 0)),
                            pl.BlockSpec((tm, LANES), lambda b, s: (s, 0))],
        out_specs=out_specs,
        out_shape=out_shape,
        compiler_params=_params("parallel", "parallel"),
        name="mla_up",
    )(*x_args, w_in, q_norm, kv_norm, w_uq, w_ukv, cs)
    return tuple(outs) if fused else (x_or_pending, *outs)


def _attn_kernel(q_ref, k_ref, v_ref, o_ref, s_scr, p_scr, m_scr, l_scr, a_scr, acc_scr):
    nh, tq = q_ref.shape[1], q_ref.shape[2]
    n_lt = tq // LANES
    qi = pl.program_id(2)

    def scores(hd, k0):
        s_scr[hd] = _dot_nt(q_ref[0, hd], k_ref[0, hd, pl.ds(k0, tq), :])

    def softmax(hd, first):
        for r0 in range(0, tq, ATTN_SUB):
            rows = slice(r0, r0 + ATTN_SUB)
            n_vis = -(-(r0 + ATTN_SUB) // LANES) if first else n_lt
            tiles = []
            for c in range(n_vis):
                s = s_scr[hd, rows, c * LANES:(c + 1) * LANES]
                if first and ((c + 1) * LANES - 1) // CHUNK > r0 // CHUNK:
                    ri = jnp.right_shift(r0 + lax.broadcasted_iota(jnp.int32, (ATTN_SUB, LANES), 0), CHUNK_SHIFT)
                    ci = jnp.right_shift(c * LANES + lax.broadcasted_iota(jnp.int32, (ATTN_SUB, LANES), 1),
                                         CHUNK_SHIFT)
                    s = jnp.where(ci <= ri, s, NEG_BIG)
                tiles.append(s)
            mx = tiles[0]
            for c in range(1, n_vis):
                mx = jnp.maximum(mx, tiles[c])
            m_new = jnp.broadcast_to(jnp.max(mx, axis=-1, keepdims=True), (ATTN_SUB, LANES))
            if not first:
                m_prev = m_scr[hd, rows, :]
                m_new = jnp.maximum(m_prev, m_new)
                alpha = jnp.exp2(m_prev - m_new)
                a_scr[hd, rows, :] = alpha
            psum = jnp.zeros((ATTN_SUB, LANES), F32)
            for c in range(n_vis):
                p = jnp.exp2(tiles[c] - m_new)
                psum = psum + p
                p_scr[hd, rows, c * LANES:(c + 1) * LANES] = p.astype(BF16)
            for c in range(n_vis, n_lt):
                p_scr[hd, rows, c * LANES:(c + 1) * LANES] = jnp.zeros((ATTN_SUB, LANES), BF16)
            m_scr[hd, rows, :] = m_new
            l_new = jnp.broadcast_to(jnp.sum(psum, axis=-1, keepdims=True), (ATTN_SUB, LANES))
            l_scr[hd, rows, :] = l_new if first else alpha * l_scr[hd, rows, :] + l_new

    def values(hd, k0, first):
        pv = _dot(p_scr[hd], v_ref[0, hd, pl.ds(k0, tq), :])
        acc_scr[hd] = pv if first else a_scr[hd] * acc_scr[hd] + pv

    def block(k0, first):
        for hd in range(nh):
            scores(hd, k0)
        for hd in range(nh):
            softmax(hd, first)
        for hd in range(nh):
            values(hd, k0, first)

    def body(j, carry):
        block(pl.multiple_of(j * tq, tq), False)
        return carry

    block(pl.multiple_of(qi * tq, tq), True)
    lax.fori_loop(0, qi, body, 0)
    for hd in range(nh):
        o_ref[0, :, hd * C_V:(hd + 1) * C_V] = (acc_scr[hd] / l_scr[hd]).astype(o_ref.dtype)


def _attn(q, k, v):
    bsz, _, seq, _ = q.shape
    tq = _row_tile(seq, ATTN_ROWS)
    nh = ATTN_HEADS_PER_STEP
    return pl.pallas_call(
        _attn_kernel,
        grid=(bsz, C_HEADS // nh, seq // tq),
        in_specs=[pl.BlockSpec((1, nh, tq, C_QK), lambda b, h, i: (b, h, i, 0)),
                  pl.BlockSpec((1, nh, seq, C_QK), lambda b, h, i: (b, h, 0, 0)),
                  pl.BlockSpec((1, nh, seq, C_V), lambda b, h, i: (b, h, 0, 0))],
        out_specs=pl.BlockSpec((1, tq, nh * C_V), lambda b, h, i: (b, i, h)),
        out_shape=jax.ShapeDtypeStruct((bsz, seq, C_HEADS * C_V), BF16),
        scratch_shapes=[pltpu.VMEM((nh, tq, tq), F32),
                        pltpu.VMEM((nh, tq, tq), BF16),
                        pltpu.VMEM((nh, tq, LANES), F32),
                        pltpu.VMEM((nh, tq, LANES), F32),
                        pltpu.VMEM((nh, tq, LANES), F32),
                        pltpu.VMEM((nh, tq, C_V), F32)],
        compiler_params=_params("parallel", "parallel", "arbitrary"),
        name="attn",
    )(q, k, v)


def _proj_ln_router_kernel(n_in, *refs):
    a_refs = refs[0:n_in]
    w_refs = refs[n_in:2 * n_in]
    x_ref, g_ref, b_ref, wr_ref, br_ref = refs[2 * n_in:2 * n_in + 5]
    x1_ref, x1b_ref, meta_ref, gate_ref, cnt_ref = refs[2 * n_in + 5:2 * n_in + 10]
    base_scr = refs[2 * n_in + 10]
    tm = x_ref.shape[0]
    i = pl.program_id(0)

    @pl.when(i == 0)
    def _():
        base_scr[...] = jnp.zeros(base_scr.shape, F32)

    mix = _dot(a_refs[0][...], w_refs[0][...])
    for j in range(1, n_in):
        mix = mix + _dot(a_refs[j][...], w_refs[j][...])
    x1 = _layer_norm_rows(DEEPNORM_ALPHA * x_ref[...] + mix, g_ref[...], b_ref[...])
    x1_ref[...] = x1
    x1b = x1.astype(BF16)
    x1b_ref[...] = x1b

    work = _dot_nt(wr_ref[...], x1b) + br_ref[:, 0:1]
    eidx = lax.broadcasted_iota(jnp.int32, (N_EXPERTS, tm), 0).astype(F32)
    cnt = jnp.zeros((N_EXPERTS, tm), F32)
    tops, idxs, hots = [], [], []
    for _ in range(TOP_K):
        mx = jnp.max(work, axis=0, keepdims=True)
        idx = jnp.min(jnp.where(work == mx, eidx, float(N_EXPERTS)), axis=0, keepdims=True)
        hot = eidx == idx
        work = jnp.where(hot, NEG_BIG, work)
        cnt = cnt + jnp.where(hot, 1.0, 0.0)
        tops.append(mx)
        idxs.append(idx)
        hots.append(hot)
    exps = [jnp.exp(t - tops[0]) for t in tops]
    denom = exps[0] + exps[1] + exps[2] + exps[3]

    ri = lax.broadcasted_iota(jnp.int32, (tm, tm), 0)
    ci = lax.broadcasted_iota(jnp.int32, (tm, tm), 1)
    before = jnp.where(ri < ci, 1.0, 0.0).astype(BF16)
    base = base_scr[:, 0:1]
    rank = _dot(cnt.astype(BF16), before) + base
    row = lax.broadcasted_iota(jnp.int32, (2 * TOP_K, tm), 0)
    meta = jnp.zeros((2 * TOP_K, tm), jnp.int32)
    gates = jnp.zeros((2 * TOP_K, tm), F32)
    for kk in range(TOP_K):
        pos = jnp.sum(jnp.where(hots[kk], rank, 0.0), axis=0, keepdims=True).astype(jnp.int32)
        meta = jnp.where(row == kk, idxs[kk].astype(jnp.int32), meta)
        meta = jnp.where(row == TOP_K + kk, pos, meta)
        gates = jnp.where(row == kk, exps[kk] / denom, gates)
    meta_ref[...] = meta
    gate_ref[...] = gates
    total = base + jnp.sum(cnt, axis=1, keepdims=True)
    base_scr[...] = jnp.broadcast_to(total, base_scr.shape)
    cnt_ref[...] = jnp.broadcast_to(total, cnt_ref.shape)


def _proj_ln_router(a_list, w_list, x, ln_g, ln_b, w_r, b_r, tm):
    t = x.shape[0]
    n_in = len(a_list)
    in_specs = [pl.BlockSpec((tm, a.shape[1]), lambda i: (i, 0)) for a in a_list]
    in_specs += [pl.BlockSpec(w.shape, lambda i: (0, 0)) for w in w_list]
    in_specs += [pl.BlockSpec((tm, D_MODEL), lambda i: (i, 0)),
                 pl.BlockSpec((1, D_MODEL), lambda i: (0, 0)),
                 pl.BlockSpec((1, D_MODEL), lambda i: (0, 0)),
                 pl.BlockSpec((N_EXPERTS, D_MODEL), lambda i: (0, 0)),
                 pl.BlockSpec((N_EXPERTS, LANES), lambda i: (0, 0))]
    return pl.pallas_call(
        functools.partial(_proj_ln_router_kernel, n_in),
        grid=(t // tm,),
        in_specs=in_specs,
        out_specs=[pl.BlockSpec((tm, D_MODEL), lambda i: (i, 0)),
                   pl.BlockSpec((tm, D_MODEL), lambda i: (i, 0)),
                   pl.BlockSpec((2 * TOP_K, tm), lambda i: (0, i)),
                   pl.BlockSpec((2 * TOP_K, tm), lambda i: (0, i)),
                   pl.BlockSpec((N_EXPERTS, LANES), lambda i: (0, 0))],
        out_shape=[jax.ShapeDtypeStruct((t, D_MODEL), F32),
                   jax.ShapeDtypeStruct((t, D_MODEL), BF16),
                   jax.ShapeDtypeStruct((2 * TOP_K, t), jnp.int32),
                   jax.ShapeDtypeStruct((2 * TOP_K, t), F32),
                   jax.ShapeDtypeStruct((N_EXPERTS, LANES), F32)],
        scratch_shapes=[pltpu.VMEM((N_EXPERTS, LANES), F32)],
        compiler_params=_params("arbitrary"),
        name="proj_ln_router",
    )(*a_list, *w_list, x, ln_g, ln_b, w_r, b_r)


def _moe_kernel(blk_ref, exp_ref, lo_ref, hi_ref, x_ref, wgu_ref, wd_ref, bg_ref, bu_ref, bd_ref, o_ref,
                wg_scr, wu_scr, wd_scr):
    i = pl.program_id(0)
    tm = x_ref.shape[0]
    blk = blk_ref[i]
    lo = lo_ref[i]
    hi = hi_ref[i]
    prev = jnp.maximum(i - 1, 0)
    first = jnp.logical_or(i == 0, blk != blk_ref[prev])
    new_expert = jnp.logical_or(i == 0, exp_ref[i] != exp_ref[prev])

    @pl.when(new_expert)
    def _():
        pc = MOE_PREP_COLS
        half = pc // 2
        for c in range(2 * D_FF // pc):
            for kc in range(D_MODEL // LANES):
                k0 = kc * LANES
                tb = wgu_ref[0, 0, k0:k0 + LANES, c * pc:(c + 1) * pc].astype(BF16).T
                pair = pltpu.bitcast(tb, jnp.uint32)
                gate = pltpu.bitcast(jnp.left_shift(pair, 16), F32)
                up = pltpu.bitcast(jnp.bitwise_and(pair, jnp.uint32(0xFFFF0000)), F32)
                wg_scr[c * half:(c + 1) * half, k0:k0 + LANES] = gate.astype(BF16)
                wu_scr[c * half:(c + 1) * half, k0:k0 + LANES] = up.astype(BF16)
        wd_scr[...] = wd_ref[0, 0].astype(BF16)

    def ffn(x):
        glu = jnp.minimum(_dot_nt(x, wg_scr[...]) + bg_ref[0, 0], SWIGLU_LIMIT)
        lin = jnp.clip(_dot_nt(x, wu_scr[...]) + bu_ref[0, 0], -SWIGLU_LIMIT, SWIGLU_LIMIT)
        act = glu * _sigmoid(SWIGLU_ALPHA * glu) * (lin + 1.0)
        return (_dot(act.astype(BF16), wd_scr[...]) + bd_ref[0, 0]).astype(o_ref.dtype)

    full = (hi - lo) == tm

    @pl.when(full)
    def _():
        o_ref[...] = ffn(x_ref[...])

    @pl.when(jnp.logical_and(first, jnp.logical_not(full)))
    def _():
        o_ref[...] = jnp.zeros(o_ref.shape, o_ref.dtype)

    for r0 in range(0, tm, MOE_PART):
        row0 = blk * tm + r0
        touched = jnp.logical_and(hi > lo, jnp.logical_and(lo < row0 + MOE_PART, hi > row0))

        @pl.when(jnp.logical_and(jnp.logical_not(full), touched))
        def _():
            y = ffn(x_ref[r0:r0 + MOE_PART, :])
            row = row0 + lax.broadcasted_iota(jnp.int32, (MOE_PART, 1), 0)
            keep = jnp.logical_and(row >= lo, row < hi)
            o_ref[r0:r0 + MOE_PART, :] = jnp.where(keep, y, o_ref[r0:r0 + MOE_PART, :])


def _moe_sorted(layer, xs, counts, w_gate_up, b_gate, b_up, w_down, b_down):
    n_rows = xs.shape[0]
    tm = _row_tile(n_rows, MOE_ROWS)
    items = _moe_items(counts, n_rows, tm)
    n_items = items[0].shape[0]

    def by_expert(i, blk, ex, lo, hi):
        return (layer, ex[i], 0, 0)

    def by_block(i, blk, ex, lo, hi):
        return (blk[i], 0)

    grid_spec = pltpu.PrefetchScalarGridSpec(
        num_scalar_prefetch=4,
        grid=(n_items,),
        in_specs=[pl.BlockSpec((tm, D_MODEL), by_block),
                  pl.BlockSpec((1, 1, D_MODEL, 2 * D_FF), by_expert),
                  pl.BlockSpec((1, 1, D_FF, D_MODEL), by_expert),
                  pl.BlockSpec((1, 1, 1, D_FF), by_expert),
                  pl.BlockSpec((1, 1, 1, D_FF), by_expert),
                  pl.BlockSpec((1, 1, 1, D_MODEL), by_expert)],
        out_specs=pl.BlockSpec((tm, D_MODEL), by_block),
        scratch_shapes=[pltpu.VMEM((D_FF, D_MODEL), BF16),
                        pltpu.VMEM((D_FF, D_MODEL), BF16),
                        pltpu.VMEM((D_FF, D_MODEL), BF16)],
    )
    return pl.pallas_call(
        _moe_kernel,
        grid_spec=grid_spec,
        out_shape=jax.ShapeDtypeStruct((n_rows, D_MODEL), BF16),
        compiler_params=_params("arbitrary", vmem=MOE_VMEM_LIMIT),
        name="moe_ffn",
    )(*items, xs, w_gate_up, w_down, b_gate, b_up, b_down)


def _moe_items(counts, n_rows, tm):
    n_blocks = n_rows // tm
    start = jnp.cumsum(counts) - counts
    cuts = jnp.sort(jnp.concatenate([jnp.arange(n_blocks, dtype=jnp.int32) * tm, start.astype(jnp.int32)]))
    lo = cuts
    hi = jnp.concatenate([cuts[1:], jnp.array([n_rows], jnp.int32)])
    blk = jnp.minimum(lo // tm, n_blocks - 1)
    ex = jnp.clip(jnp.sum((start[None, :] <= lo[:, None]).astype(jnp.int32), axis=1) - 1, 0, N_EXPERTS - 1)
    return blk.astype(jnp.int32), ex.astype(jnp.int32), lo.astype(jnp.int32), hi.astype(jnp.int32)


def _combine_ln_kernel(x_ref, y_ref, gate_ref, g_ref, b_ref, o_ref):
    o_ref[...] = _combine_rows(x_ref, y_ref, gate_ref, g_ref, b_ref)


def _combine_ln(x1, y4, gates, ln_g, ln_b, tm):
    t = x1.shape[0]
    return pl.pallas_call(
        _combine_ln_kernel,
        grid=(t // tm,),
        in_specs=[pl.BlockSpec((tm, D_MODEL), lambda i: (i, 0)),
                  pl.BlockSpec((TOP_K, tm, D_MODEL), lambda i: (0, i, 0)),
                  pl.BlockSpec((2 * TOP_K, tm), lambda i: (0, i)),
                  pl.BlockSpec((1, D_MODEL), lambda i: (0, 0)),
                  pl.BlockSpec((1, D_MODEL), lambda i: (0, 0))],
        out_specs=pl.BlockSpec((tm, D_MODEL), lambda i: (i, 0)),
        out_shape=jax.ShapeDtypeStruct((t, D_MODEL), F32),
        compiler_params=_params("parallel"),
        name="combine_ln",
    )(x1, y4, gates, ln_g, ln_b)


def _moe_layer(layer, x1, x1b, meta_t, gates_t, counts, ln_g, ln_b, w_gate_up, b_gate, b_up, w_down, b_down):
    t = x1.shape[0]
    assert N_EXPERTS * t < 2 ** 31
    e = meta_t[0:TOP_K]
    pos = meta_t[TOP_K:2 * TOP_K]
    cnt = counts[:, 0].astype(jnp.int32)
    start = jnp.cumsum(cnt) - cnt
    experts = jnp.arange(N_EXPERTS, dtype=jnp.int32)
    dest = pos + jnp.sum(jnp.where(e[:, :, None] == experts, start, 0), axis=-1)
    keys = e * t + jnp.arange(t, dtype=jnp.int32)[None, :]
    tok_sorted = jnp.remainder(jnp.sort(keys.reshape(-1)), t)
    xs = x1b[tok_sorted]
    ys = _moe_sorted(layer, xs, cnt, w_gate_up, b_gate, b_up, w_down, b_down)
    y4 = ys[dest]
    return (x1, y4, gates_t, ln_g, ln_b)


def kernel(x, even_w_in, even_sgu_ln_g, even_sgu_ln_b, even_sgu_w, even_sgu_b, even_conv_w, even_a_log, even_dt_bias, even_gdn_norm, even_w_out, odd_w_in, odd_q_norm, odd_w_uq, odd_kv_norm, odd_w_ukv, odd_w_out, ln_mix_g, ln_mix_b, ln_ffn_g, ln_ffn_b, moe_w_router, moe_b_router, moe_w_gate_up, moe_b_gate_up, moe_w_down, moe_b_down):
    bsz, seq, d = x.shape
    t = bsz * seq
    xt = x.reshape(t, d)
    tm = _row_tile(t, ROWS)
    b_gate = moe_b_gate_up[:, :, None, 0::2]
    b_up = moe_b_gate_up[:, :, None, 1::2]
    b_down = moe_b_down[:, :, None, :]

    inv_freq = ROPE_THETA ** (-jnp.arange(0, C_ROPE, 2, dtype=F32) / C_ROPE)
    ang = jnp.arange(seq, dtype=F32)[:, None] * inv_freq[None, :]
    cs = jnp.concatenate([jnp.cos(ang), jnp.cos(ang), jnp.sin(ang), jnp.sin(ang)], axis=1)

    pos_chunk = np.arange(A_BLOCK) // CHUNK
    allowed = jnp.asarray(pos_chunk[None, :] <= pos_chunk[:, None])

    def rot_cols(w):
        half = C_ROPE // 2
        return jnp.concatenate([-w[..., half:], w[..., :half]], axis=-1)

    for layer in range(DEPTH):
        i = layer // 2
        if layer % 2 == 0:
            w_in = even_w_in[i]
            w_all = jnp.pad(w_in, ((0, 0), (0, LANES - 2 * B_HEADS))).astype(BF16)
            sw = jnp.where(allowed[None], even_sgu_w[i], 0).astype(BF16)
            sb = jnp.broadcast_to(even_sgu_b[i][:, :, None], (A_GROUPS, A_BLOCK, A_GROUP_DIM)).astype(F32)
            xt, y_a, h_b, ab = _even_in(xt, w_all, even_sgu_ln_g[i][None], even_sgu_ln_b[i][None], sw, sb, t, tm)
            abt = ab.reshape(bsz, seq, LANES)[:, :, 0:SUBLANES].transpose(0, 2, 1)
            hp = jnp.pad(jnp.stack([even_a_log[i], even_dt_bias[i]]), ((0, 0), (0, LANES - B_HEADS)))
            hpt = jnp.pad(jnp.stack([even_a_log[i], even_dt_bias[i]], axis=1), ((0, SUBLANES - B_HEADS), (0, 0)))
            y_b = _gdn(h_b, ab, abt, even_conv_w[i], hp, hpt, even_gdn_norm[i][None], bsz, seq)
            w_out = even_w_out[i].astype(BF16)
            a_list = [y_a, y_b]
            w_list = [w_out[0:A_WIDTH], w_out[A_WIDTH:]]
        else:
            w_in = odd_w_in[i]
            w_pe = w_in[:, Q_LORA + KV_LORA:]
            w_in2 = jnp.concatenate([w_in, rot_cols(w_pe)], axis=1).astype(BF16)
            w_uq = odd_w_uq[i].reshape(Q_LORA, C_HEADS, C_QK)
            w_uq2 = jnp.concatenate([w_uq, rot_cols(w_uq[..., C_NOPE:])], axis=-1)
            w_uq2 = w_uq2.reshape(Q_LORA, C_HEADS * (C_NOPE + 2 * C_ROPE)).astype(BF16)
            xt, q, k, v = _mla_up(xt, w_in2, odd_q_norm[i][None], odd_kv_norm[i][None], w_uq2,
                                  odd_w_ukv[i].astype(BF16), cs, bsz, seq, _row_tile(seq, ROWS))
            o = _attn(q, k, v)
            a_list = [o.reshape(t, C_HEADS * C_V)]
            w_list = [odd_w_out[i].astype(BF16)]

        w_r = moe_w_router[layer].T.astype(BF16)
        b_r = jnp.broadcast_to(moe_b_router[layer][:, None], (N_EXPERTS, LANES))
        x1, x1b, meta, gates, counts = _proj_ln_router(a_list, w_list, xt, ln_mix_g[layer][None],
                                                       ln_mix_b[layer][None], w_r, b_r, tm)
        xt = _moe_layer(layer, x1, x1b, meta, gates, counts, ln_ffn_g[layer][None], ln_ffn_b[layer][None],
                        moe_w_gate_up, b_gate, b_up, moe_w_down, b_down)
    return _combine_ln(*xt, tm).reshape(bsz, seq, d)
```

```python
import functools

import jax
import jax.numpy as jnp
import numpy as np
from jax import lax
from jax.experimental import pallas as pl
from jax.experimental.pallas import tpu as pltpu

F32 = jnp.float32
BF16 = jnp.bfloat16

D_MODEL = 1024
DEPTH = 4
CHUNK = 64
CHUNK_SHIFT = CHUNK.bit_length() - 1
DEEPNORM_ALPHA = (2 * DEPTH) ** 0.25
LN_EPS = 1e-5
RMS_EPS = 1e-6

A_BLOCK = 128
A_GROUPS = 4
A_WIDTH = D_MODEL // 2
A_GROUP_DIM = A_WIDTH // A_GROUPS

B_HEADS = 4
B_HEAD_DIM = D_MODEL // 8
B_WIDTH = B_HEADS * B_HEAD_DIM
CONV_K = 4

C_HEADS = 8
C_NOPE = 128
C_ROPE = 64
C_V = D_MODEL // C_HEADS
C_QK = C_NOPE + C_ROPE
Q_LORA = 3 * D_MODEL // 8
KV_LORA = D_MODEL // 4
ROPE_THETA = 10000.0

N_EXPERTS = 32
TOP_K = 4
D_FF = D_MODEL
SWIGLU_LIMIT = 7.0
SWIGLU_ALPHA = 1.702

LANES = 128
SUBLANES = 8
VMEM_LIMIT = 48 * 1024 * 1024
MOE_VMEM_LIMIT = 58 * 1024 * 1024
NEG_BIG = -1e30
LOG2_E = 1.4426950408889634

ROWS = 512
GDN_ROWS = 256
GDN_BATCH = 2
INV_BASE_SHIFT = 1
ATTN_ROWS = 512
ATTN_HEADS_PER_STEP = 4
ATTN_SUB = 128
MOE_ROWS = 512
MOE_PART = 256
MOE_PREP_COLS = 512


def _row_tile(t, pref):
    tm = pref
    while t % tm:
        tm //= 2
    return tm


def _params(*sem, vmem=VMEM_LIMIT):
    return pltpu.CompilerParams(dimension_semantics=sem, vmem_limit_bytes=vmem)


def _dot(a, b):
    return jnp.dot(a, b, preferred_element_type=F32)


def _dot_nt(a, b):
    return lax.dot_general(a, b, (((1,), (1,)), ((), ())), preferred_element_type=F32)


def _dot_tn(a, b):
    return lax.dot_general(a, b, (((0,), (0,)), ((), ())), preferred_element_type=F32)


def _sigmoid(x):
    return 1.0 / (1.0 + jnp.exp(-x))


def _softplus(x):
    return jnp.maximum(x, 0.0) + jnp.log1p(jnp.exp(-jnp.abs(x)))


def _layer_norm_rows(y, g, b):
    mu = jnp.mean(y, axis=-1, keepdims=True)
    d = y - mu
    var = jnp.mean(d * d, axis=-1, keepdims=True)
    return d * lax.rsqrt(var + LN_EPS) * g + b


def _gelu(x):
    return 0.5 * x * (1.0 + lax.erf(x * (2.0 ** -0.5)))


def _combine_rows(x_ref, y_ref, gate_ref, g_ref, b_ref):
    n_g = gate_ref.shape[0]
    eye = jnp.where(lax.broadcasted_iota(jnp.int32, (n_g, n_g), 0) == lax.broadcasted_iota(jnp.int32, (n_g, n_g), 1),
                    1.0, 0.0).astype(F32)
    gates = lax.dot_general(gate_ref[...], eye, (((0,), (0,)), ((), ())), preferred_element_type=F32,
                            precision=lax.Precision.HIGHEST)
    ffn = y_ref[0].astype(F32) * gates[:, 0:1]
    for kk in range(1, TOP_K):
        ffn = ffn + y_ref[kk].astype(F32) * gates[:, kk:kk + 1]
    return _layer_norm_rows(DEEPNORM_ALPHA * x_ref[...] + ffn, g_ref[...], b_ref[...])


def _combine_specs(tm, row_block):
    return [pl.BlockSpec((tm, D_MODEL), lambda *g: (row_block(*g), 0)),
            pl.BlockSpec((TOP_K, tm, D_MODEL), lambda *g: (0, row_block(*g), 0)),
            pl.BlockSpec((2 * TOP_K, tm), lambda *g: (0, row_block(*g))),
            pl.BlockSpec((1, D_MODEL), lambda *g: (0, 0)),
            pl.BlockSpec((1, D_MODEL), lambda *g: (0, 0))]


def _even_in_kernel(fused, *refs):
    if fused:
        w_ref, wabt_ref, g_ref, b_ref, sw_ref, sb_ref, x2_ref, o_ref, hb_ref, ab_ref, abt_ref = refs[5:]
        x = _combine_rows(*refs[0:5])
        x2_ref[...] = x
    else:
        x_ref, w_ref, wabt_ref, g_ref, b_ref, sw_ref, sb_ref, o_ref, hb_ref, ab_ref, abt_ref = refs
        x = x_ref[...]
    tm = x.shape[0]
    xb = x.astype(BF16)
    h = _dot(xb, w_ref[...])
    abt_ref[0] = _dot_nt(wabt_ref[...], xb)
    n_b = hb_ref.shape[1]
    hb_ref[...] = h[:, 2 * A_WIDTH:2 * A_WIDTH + n_b].astype(hb_ref.dtype)
    ab_ref[...] = h[:, 2 * A_WIDTH + n_b:]
    for grp in range(A_GROUPS):
        c0 = grp * A_GROUP_DIM
        u = _gelu(h[:, c0:c0 + A_GROUP_DIM])
        v = _gelu(h[:, A_WIDTH + c0:A_WIDTH + c0 + A_GROUP_DIM])
        vn = _layer_norm_rows(v, g_ref[:, c0:c0 + A_GROUP_DIM], b_ref[:, c0:c0 + A_GROUP_DIM]).astype(BF16)
        for blk in range(tm // A_BLOCK):
            r0 = blk * A_BLOCK
            mixed = _dot(sw_ref[grp], vn[r0:r0 + A_BLOCK]) + sb_ref[grp]
            o_ref[r0:r0 + A_BLOCK, c0:c0 + A_GROUP_DIM] = (u[r0:r0 + A_BLOCK] * mixed).astype(o_ref.dtype)


def _even_in(x_or_pending, w_all, w_abt, ln_g, ln_b, sw, sb, bsz, seq, tm):
    fused = isinstance(x_or_pending, tuple)
    t = bsz * seq
    tps = seq // tm
    n_all = w_all.shape[1]
    x_args = list(x_or_pending) if fused else [x_or_pending]
    x_specs = _combine_specs(tm, lambda i: i) if fused else [pl.BlockSpec((tm, D_MODEL), lambda i: (i, 0))]
    out_specs = [pl.BlockSpec((tm, A_WIDTH), lambda i: (i, 0)),
                 pl.BlockSpec((tm, 4 * B_WIDTH), lambda i: (i, 0)),
                 pl.BlockSpec((tm, LANES), lambda i: (i, 0)),
                 pl.BlockSpec((1, SUBLANES, tm), lambda i: (i // tps, 0, i % tps))]
    out_shape = [jax.ShapeDtypeStruct((t, A_WIDTH), BF16),
                 jax.ShapeDtypeStruct((t, 4 * B_WIDTH), BF16),
                 jax.ShapeDtypeStruct((t, LANES), F32),
                 jax.ShapeDtypeStruct((bsz, SUBLANES, seq), F32)]
    if fused:
        out_specs = [pl.BlockSpec((tm, D_MODEL), lambda i: (i, 0))] + out_specs
        out_shape = [jax.ShapeDtypeStruct((t, D_MODEL), F32)] + out_shape
    outs = pl.pallas_call(
        functools.partial(_even_in_kernel, fused),
        grid=(t // tm,),
        in_specs=x_specs + [pl.BlockSpec((D_MODEL, n_all), lambda i: (0, 0)),
                            pl.BlockSpec((SUBLANES, D_MODEL), lambda i: (0, 0)),
                            pl.BlockSpec((1, A_WIDTH), lambda i: (0, 0)),
                            pl.BlockSpec((1, A_WIDTH), lambda i: (0, 0)),
                            pl.BlockSpec((A_GROUPS, A_BLOCK, A_BLOCK), lambda i: (0, 0, 0)),
                            pl.BlockSpec((A_GROUPS, A_BLOCK, A_GROUP_DIM), lambda i: (0, 0, 0))],
        out_specs=out_specs,
        out_shape=out_shape,
        compiler_params=_params("parallel"),
        name="even_in",
    )(*x_args, w_all, w_abt, ln_g, ln_b, sw, sb)
    return tuple(outs) if fused else (x_or_pending, *outs)


def _gdn_kernel(h_ref, ab_ref, abt_ref, cw_ref, hp_ref, hpt_ref, nw_ref, o_ref, conv_scr, state_scr):
    nb, tr = h_ref.shape[0], h_ref.shape[1]
    n_chunks = tr // CHUNK
    s_idx = pl.program_id(1)

    @pl.when(s_idx == 0)
    def _():
        conv_scr[:, 0:SUBLANES, :] = jnp.zeros((nb, SUBLANES, 3 * B_WIDTH), F32)
        state_scr[...] = jnp.zeros(state_scr.shape, F32)

    ri = lax.broadcasted_iota(jnp.int32, (tr, tr), 0)
    ci = lax.broadcasted_iota(jnp.int32, (tr, tr), 1)
    same = jnp.right_shift(ri, CHUNK_SHIFT) == jnp.right_shift(ci, CHUNK_SHIFT)
    lower = jnp.where(same & (ri >= ci), 1.0, 0.0).astype(F32)
    upper = jnp.where(same & (ri <= ci), 1.0, 0.0).astype(F32)
    incl = same & (ri >= ci)
    strict = same & (ri > ci)
    eye = jnp.where(ri == ci, 1.0, 0.0).astype(F32)

    chains = [(bi, hd) for bi in range(nb) for hd in range(B_HEADS)]
    qs, ks, vs, betas, gccs, decays, k_bs, kbs, ps, ts, a_mats = {}, {}, {}, {}, {}, {}, {}, {}, {}, {}, {}
    us, ws, intras, q_decs = {}, {}, {}, {}
    for bi in range(nb):
        conv_scr[bi, SUBLANES:SUBLANES + tr, :] = h_ref[bi, :, 0:3 * B_WIDTH].astype(F32)
        acc = conv_scr[bi, SUBLANES:SUBLANES + tr, :] * cw_ref[CONV_K - 1:CONV_K, :]
        for d in range(1, CONV_K):
            acc = acc + conv_scr[bi, SUBLANES - d:SUBLANES - d + tr, :] * cw_ref[CONV_K - 1 - d:CONV_K - d, :]
        conv_scr[bi, 0:SUBLANES, :] = conv_scr[bi, tr:tr + SUBLANES, :]
        qkv = acc * _sigmoid(acc)

        ab = ab_ref[bi]
        sp_c = _softplus(ab + hp_ref[1:2, :])
        g_col = -jnp.exp(hp_ref[0:1, :]) * sp_c
        beta_col = _sigmoid(ab)
        abt = abt_ref[bi]
        sp_r = _softplus(abt + hpt_ref[:, 1:2])
        g_row = -jnp.exp(hpt_ref[:, 0:1]) * sp_r
        gc_col = jnp.dot(lower, g_col, preferred_element_type=F32, precision=lax.Precision.HIGHEST)
        gc_row = jnp.dot(g_row, upper, preferred_element_type=F32, precision=lax.Precision.HIGHEST)
        for hd in range(B_HEADS):
            ch = (bi, hd)
            l0 = hd * B_HEAD_DIM
            q = qkv[:, l0:l0 + B_HEAD_DIM]
            k = qkv[:, B_WIDTH + l0:B_WIDTH + l0 + B_HEAD_DIM]
            qs[ch] = q * lax.rsqrt(jnp.sum(q * q, axis=-1, keepdims=True) + RMS_EPS) * (B_HEAD_DIM ** -0.5)
            ks[ch] = k * lax.rsqrt(jnp.sum(k * k, axis=-1, keepdims=True) + RMS_EPS)
            vs[ch] = qkv[:, 2 * B_WIDTH + l0:2 * B_WIDTH + l0 + B_HEAD_DIM]
            betas[ch] = beta_col[:, B_HEADS + hd:B_HEADS + hd + 1]
            gccs[ch] = gc_col[:, hd:hd + 1]
            gcr = gc_row[hd:hd + 1, :]
            decays[ch] = jnp.where(incl, jnp.exp(jnp.where(incl, gccs[ch] - gcr, 0.0)), 0.0)
            k_bs[ch] = ks[ch].astype(BF16)
            kbs[ch] = ks[ch] * betas[ch]
    lvl = [jnp.right_shift(ri, s) == jnp.right_shift(ci, s) for s in range(INV_BASE_SHIFT, CHUNK_SHIFT + 1)]
    for ch in chains:
        a_mat = jnp.where(strict, _dot_nt(kbs[ch].astype(BF16), k_bs[ch]) * decays[ch], 0.0)
        a_mats[ch] = a_mat
        d_mat = jnp.where(lvl[0], a_mat, 0.0)
        ps[ch] = -d_mat
        ts[ch] = eye - d_mat
    for _ in range(INV_BASE_SHIFT - 1):
        for ch in chains:
            pb = ps[ch].astype(BF16)
            ps[ch] = _dot(pb, pb)
        for ch in chains:
            ts[ch] = ts[ch] + _dot(ts[ch].astype(BF16), ps[ch].astype(BF16))
    for j in range(1, len(lvl)):
        for ch in chains:
            join = jnp.where(lvl[j] & jnp.logical_not(lvl[j - 1]), a_mats[ch], 0.0).astype(BF16)
            t_b = ts[ch].astype(BF16)
            ps[ch] = _dot(_dot(t_b, join).astype(BF16), t_b)
        for ch in chains:
            ts[ch] = ts[ch] - ps[ch]
    for ch in chains:
        t_b = ts[ch].astype(BF16)
        e_gc = jnp.exp(gccs[ch])
        us[ch] = _dot(t_b, (vs[ch] * betas[ch]).astype(BF16))
        ws[ch] = _dot(t_b, (kbs[ch] * e_gc).astype(BF16)).astype(BF16)
        intras[ch] = jnp.where(incl, _dot_nt(qs[ch].astype(BF16), k_bs[ch]) * decays[ch], 0.0).astype(BF16)
        q_decs[ch] = (qs[ch] * e_gc).astype(BF16)
    for c in range(n_chunks):
        r0 = c * CHUNK
        for ch in chains:
            bi, hd = ch
            l0 = hd * B_HEAD_DIM
            g_last = gccs[ch][r0 + CHUNK - 1:r0 + CHUNK, :]
            k_dec = (ks[ch][r0:r0 + CHUNK] * jnp.exp(g_last - gccs[ch][r0:r0 + CHUNK])).astype(BF16)
            state = state_scr[bi * B_HEADS + hd]
            state_b = state.astype(BF16)
            v_new = (us[ch][r0:r0 + CHUNK] - _dot(ws[ch][r0:r0 + CHUNK], state_b)).astype(BF16)
            out = _dot(q_decs[ch][r0:r0 + CHUNK], state_b) + _dot(intras[ch][r0:r0 + CHUNK, r0:r0 + CHUNK], v_new)
            state_scr[bi * B_HEADS + hd] = state * jnp.exp(g_last) + _dot_tn(k_dec, v_new)
            z = h_ref[bi, r0:r0 + CHUNK, 3 * B_WIDTH + l0:3 * B_WIDTH + l0 + B_HEAD_DIM].astype(F32)
            o_n = out * lax.rsqrt(jnp.mean(out * out, axis=-1, keepdims=True) + RMS_EPS) * nw_ref[...]
            o_ref[bi, r0:r0 + CHUNK, l0:l0 + B_HEAD_DIM] = (o_n * (z * _sigmoid(z))).astype(o_ref.dtype)


def _gdn(h_b, ab, abt, conv_w, hp, hpt, norm_w, bsz, seq):
    tr = _row_tile(seq, GDN_ROWS)
    nb = GDN_BATCH if bsz % GDN_BATCH == 0 else 1
    n_s = seq // tr
    out = pl.pallas_call(
        _gdn_kernel,
        grid=(bsz // nb, n_s),
        in_specs=[pl.BlockSpec((nb, tr, 4 * B_WIDTH), lambda b, s: (b, s, 0)),
                  pl.BlockSpec((nb, tr, LANES), lambda b, s: (b, s, 0)),
                  pl.BlockSpec((nb, SUBLANES, tr), lambda b, s: (b, 0, s)),
                  pl.BlockSpec((CONV_K, 3 * B_WIDTH), lambda b, s: (0, 0)),
                  pl.BlockSpec((2, LANES), lambda b, s: (0, 0)),
                  pl.BlockSpec((SUBLANES, 2), lambda b, s: (0, 0)),
                  pl.BlockSpec((1, B_HEAD_DIM), lambda b, s: (0, 0))],
        out_specs=pl.BlockSpec((nb, tr, B_WIDTH), lambda b, s: (b, s, 0)),
        out_shape=jax.ShapeDtypeStruct((bsz, seq, B_WIDTH), BF16),
        scratch_shapes=[pltpu.VMEM((nb, tr + 2 * SUBLANES, 3 * B_WIDTH), F32),
                        pltpu.VMEM((nb * B_HEADS, B_HEAD_DIM, B_HEAD_DIM), F32)],
        compiler_params=_params("parallel", "arbitrary"),
        name="gdn",
    )(h_b.reshape(bsz, seq, 4 * B_WIDTH), ab.reshape(bsz, seq, LANES), abt, conv_w, hp, hpt, norm_w)
    return out.reshape(bsz * seq, B_WIDTH)


def _mla_up_kernel(fused, *refs):
    if fused:
        win_ref, qn_ref, kvn_ref, wq_ref, wkv_ref, cs_ref, x2_ref, q_ref, k_ref, v_ref = refs[5:]
        x = _combine_rows(*refs[0:5])
        x2_ref[...] = x
    else:
        x_ref, win_ref, qn_ref, kvn_ref, wq_ref, wkv_ref, cs_ref, q_ref, k_ref, v_ref = refs
        x = x_ref[...]
    h = _dot(x.astype(BF16), win_ref[...])
    cs = cs_ref[...]

    def rms(x, g):
        return x * lax.rsqrt(jnp.mean(x * x, axis=-1, keepdims=True) + RMS_EPS) * g

    def rope(t):
        r = t * cs
        return r + pltpu.roll(r, C_ROPE, 1)

    cq = rms(h[:, 0:Q_LORA], qn_ref[...]).astype(BF16)
    ckv = rms(h[:, Q_LORA:Q_LORA + KV_LORA], kvn_ref[...]).astype(BF16)
    k_pe = rope(h[:, Q_LORA + KV_LORA:Q_LORA + KV_LORA + 2 * C_ROPE])[:, 0:C_ROPE].astype(BF16)
    qf = _dot(cq, wq_ref[...]) * (C_QK ** -0.5 * LOG2_E)
    kv = _dot(ckv, wkv_ref[...])
    for hd in range(C_HEADS):
        q0 = hd * (C_NOPE + 2 * C_ROPE)
        q_ref[0, hd, :, 0:C_NOPE] = qf[:, q0:q0 + C_NOPE].astype(BF16)
        q_ref[0, hd, :, C_NOPE:C_QK] = rope(qf[:, q0 + C_NOPE:q0 + C_NOPE + 2 * C_ROPE])[:, 0:C_ROPE].astype(BF16)
        k0 = hd * (C_NOPE + C_V)
        k_ref[0, hd, :, 0:C_NOPE] = kv[:, k0:k0 + C_NOPE].astype(BF16)
        k_ref[0, hd, :, C_NOPE:C_QK] = k_pe
        v_ref[0, hd, :, :] = kv[:, k0 + C_NOPE:k0 + C_NOPE + C_V].astype(BF16)


def _mla_up(x_or_pending, w_in, q_norm, kv_norm, w_uq, w_ukv, cs, bsz, seq, tm):
    fused = isinstance(x_or_pending, tuple)
    n_s = seq // tm
    qk_shape = jax.ShapeDtypeStruct((bsz, C_HEADS, seq, C_QK), BF16)
    x_args = list(x_or_pending) if fused else [x_or_pending]
    x_specs = (_combine_specs(tm, lambda b, s: b * n_s + s) if fused
               else [pl.BlockSpec((tm, D_MODEL), lambda b, s: (b * n_s + s, 0))])
    out_specs = [pl.BlockSpec((1, C_HEADS, tm, C_QK), lambda b, s: (b, 0, s, 0)),
                 pl.BlockSpec((1, C_HEADS, tm, C_QK), lambda b, s: (b, 0, s, 0)),
                 pl.BlockSpec((1, C_HEADS, tm, C_V), lambda b, s: (b, 0, s, 0))]
    out_shape = [qk_shape, qk_shape, jax.ShapeDtypeStruct((bsz, C_HEADS, seq, C_V), BF16)]
    if fused:
        out_specs = [pl.BlockSpec((tm, D_MODEL), lambda b, s: (b * n_s + s, 0))] + out_specs
        out_shape = [jax.ShapeDtypeStruct((bsz * seq, D_MODEL), F32)] + out_shape
    outs = pl.pallas_call(
        functools.partial(_mla_up_kernel, fused),
        grid=(bsz, n_s),
        in_specs=x_specs + [pl.BlockSpec(w_in.shape, lambda b, s: (0, 0)),
                            pl.BlockSpec((1, Q_LORA), lambda b, s: (0, 0)),
                            pl.BlockSpec((1, KV_LORA), lambda b, s: (0, 0)),
                            pl.BlockSpec(w_uq.shape, lambda b, s: (0, 0)),
                            pl.BlockSpec(w_ukv.shape, lambda b, s: (0, 0)),
                            pl.BlockSpec((tm, LANES), lambda b, s: (s, 0))],
        out_specs=out_specs,
        out_shape=out_shape,
        compiler_params=_params("parallel", "parallel"),
        name="mla_up",
    )(*x_args, w_in, q_norm, kv_norm, w_uq, w_ukv, cs)
    return tuple(outs) if fused else (x_or_pending, *outs)


def _attn_kernel(q_ref, k_ref, v_ref, o_ref, s_scr, p_scr, m_scr, l_scr, a_scr, acc_scr):
    nh, tq = q_ref.shape[1], q_ref.shape[2]
    n_lt = tq // LANES
    qi = pl.program_id(2)

    def scores(hd, k0):
        s_scr[hd] = _dot_nt(q_ref[0, hd], k_ref[0, hd, pl.ds(k0, tq), :])

    def softmax(hd, first):
        for r0 in range(0, tq, ATTN_SUB):
            rows = slice(r0, r0 + ATTN_SUB)
            n_vis = -(-(r0 + ATTN_SUB) // LANES) if first else n_lt
            tiles = []
            for c in range(n_vis):
                s = s_scr[hd, rows, c * LANES:(c + 1) * LANES]
                if first and ((c + 1) * LANES - 1) // CHUNK > r0 // CHUNK:
                    ri = jnp.right_shift(r0 + lax.broadcasted_iota(jnp.int32, (ATTN_SUB, LANES), 0), CHUNK_SHIFT)
                    ci = jnp.right_shift(c * LANES + lax.broadcasted_iota(jnp.int32, (ATTN_SUB, LANES), 1),
                                         CHUNK_SHIFT)
                    s = jnp.where(ci <= ri, s, NEG_BIG)
                tiles.append(s)
            mx = tiles[0]
            for c in range(1, n_vis):
                mx = jnp.maximum(mx, tiles[c])
            m_new = jnp.broadcast_to(jnp.max(mx, axis=-1, keepdims=True), (ATTN_SUB, LANES))
            if not first:
                m_prev = m_scr[hd, rows, :]
                m_new = jnp.maximum(m_prev, m_new)
                alpha = jnp.exp2(m_prev - m_new)
                a_scr[hd, rows, :] = alpha
            psum = jnp.zeros((ATTN_SUB, LANES), F32)
            for c in range(n_vis):
                p = jnp.exp2(tiles[c] - m_new)
                psum = psum + p
                p_scr[hd, rows, c * LANES:(c + 1) * LANES] = p.astype(BF16)
            for c in range(n_vis, n_lt):
                p_scr[hd, rows, c * LANES:(c + 1) * LANES] = jnp.zeros((ATTN_SUB, LANES), BF16)
            m_scr[hd, rows, :] = m_new
            l_new = jnp.broadcast_to(jnp.sum(psum, axis=-1, keepdims=True), (ATTN_SUB, LANES))
            l_scr[hd, rows, :] = l_new if first else alpha * l_scr[hd, rows, :] + l_new

    def values(hd, k0, first):
        pv = _dot(p_scr[hd], v_ref[0, hd, pl.ds(k0, tq), :])
        acc_scr[hd] = pv if first else a_scr[hd] * acc_scr[hd] + pv

    def block(k0, first):
        for hd in range(nh):
            scores(hd, k0)
        for hd in range(nh):
            softmax(hd, first)
        for hd in range(nh):
            values(hd, k0, first)

    def body(j, carry):
        block(pl.multiple_of(j * tq, tq), False)
        return carry

    block(pl.multiple_of(qi * tq, tq), True)
    lax.fori_loop(0, qi, body, 0)
    for hd in range(nh):
        o_ref[0, :, hd * C_V:(hd + 1) * C_V] = (acc_scr[hd] / l_scr[hd]).astype(o_ref.dtype)


def _attn(q, k, v):
    bsz, _, seq, _ = q.shape
    tq = _row_tile(seq, ATTN_ROWS)
    nh = ATTN_HEADS_PER_STEP
    return pl.pallas_call(
        _attn_kernel,
        grid=(bsz, C_HEADS // nh, seq // tq),
        in_specs=[pl.BlockSpec((1, nh, tq, C_QK), lambda b, h, i: (b, h, i, 0)),
                  pl.BlockSpec((1, nh, seq, C_QK), lambda b, h, i: (b, h, 0, 0)),
                  pl.BlockSpec((1, nh, seq, C_V), lambda b, h, i: (b, h, 0, 0))],
        out_specs=pl.BlockSpec((1, tq, nh * C_V), lambda b, h, i: (b, i, h)),
        out_shape=jax.ShapeDtypeStruct((bsz, seq, C_HEADS * C_V), BF16),
        scratch_shapes=[pltpu.VMEM((nh, tq, tq), F32),
                        pltpu.VMEM((nh, tq, tq), BF16),
                        pltpu.VMEM((nh, tq, LANES), F32),
                        pltpu.VMEM((nh, tq, LANES), F32),
                        pltpu.VMEM((nh, tq, LANES), F32),
                        pltpu.VMEM((nh, tq, C_V), F32)],
        compiler_params=_params("parallel", "parallel", "arbitrary"),
        name="attn",
    )(q, k, v)


def _proj_ln_router_kernel(n_in, *refs):
    a_refs = refs[0:n_in]
    w_refs = refs[n_in:2 * n_in]
    x_ref, g_ref, b_ref, wr_ref, br_ref = refs[2 * n_in:2 * n_in + 5]
    x1_ref, x1b_ref, meta_ref, gate_ref, cnt_ref = refs[2 * n_in + 5:2 * n_in + 10]
    base_scr = refs[2 * n_in + 10]
    tm = x_ref.shape[0]
    i = pl.program_id(0)

    @pl.when(i == 0)
    def _():
        base_scr[...] = jnp.zeros(base_scr.shape, F32)

    mix = _dot(a_refs[0][...], w_refs[0][...])
    for j in range(1, n_in):
        mix = mix + _dot(a_refs[j][...], w_refs[j][...])
    x1 = _layer_norm_rows(DEEPNORM_ALPHA * x_ref[...] + mix, g_ref[...], b_ref[...])
    x1_ref[...] = x1
    x1b = x1.astype(BF16)
    x1b_ref[...] = x1b

    work = _dot_nt(wr_ref[...], x1b) + br_ref[:, 0:1]
    eidx = lax.broadcasted_iota(jnp.int32, (N_EXPERTS, tm), 0).astype(F32)
    cnt = jnp.zeros((N_EXPERTS, tm), F32)
    tops, idxs, hots = [], [], []
    for _ in range(TOP_K):
        mx = jnp.max(work, axis=0, keepdims=True)
        idx = jnp.min(jnp.where(work == mx, eidx, float(N_EXPERTS)), axis=0, keepdims=True)
        hot = eidx == idx
        work = jnp.where(hot, NEG_BIG, work)
        cnt = cnt + jnp.where(hot, 1.0, 0.0)
        tops.append(mx)
        idxs.append(idx)
        hots.append(hot)
    exps = [jnp.exp(t - tops[0]) for t in tops]
    denom = exps[0] + exps[1] + exps[2] + exps[3]

    ri = lax.broadcasted_iota(jnp.int32, (tm, tm), 0)
    ci = lax.broadcasted_iota(jnp.int32, (tm, tm), 1)
    before = jnp.where(ri < ci, 1.0, 0.0).astype(BF16)
    base = base_scr[:, 0:1]
    rank = _dot(cnt.astype(BF16), before) + base
    row = lax.broadcasted_iota(jnp.int32, (2 * TOP_K, tm), 0)
    meta = jnp.zeros((2 * TOP_K, tm), jnp.int32)
    gates = jnp.zeros((2 * TOP_K, tm), F32)
    for kk in range(TOP_K):
        pos = jnp.sum(jnp.where(hots[kk], rank, 0.0), axis=0, keepdims=True).astype(jnp.int32)
        meta = jnp.where(row == kk, idxs[kk].astype(jnp.int32), meta)
        meta = jnp.where(row == TOP_K + kk, pos, meta)
        gates = jnp.where(row == kk, exps[kk] / denom, gates)
    meta_ref[...] = meta
    gate_ref[...] = gates
    total = base + jnp.sum(cnt, axis=1, keepdims=True)
    base_scr[...] = jnp.broadcast_to(total, base_scr.shape)
    cnt_ref[...] = jnp.broadcast_to(total, cnt_ref.shape)


def _proj_ln_router(a_list, w_list, x, ln_g, ln_b, w_r, b_r, tm):
    t = x.shape[0]
    n_in = len(a_list)
    in_specs = [pl.BlockSpec((tm, a.shape[1]), lambda i: (i, 0)) for a in a_list]
    in_specs += [pl.BlockSpec(w.shape, lambda i: (0, 0)) for w in w_list]
    in_specs += [pl.BlockSpec((tm, D_MODEL), lambda i: (i, 0)),
                 pl.BlockSpec((1, D_MODEL), lambda i: (0, 0)),
                 pl.BlockSpec((1, D_MODEL), lambda i: (0, 0)),
                 pl.BlockSpec((N_EXPERTS, D_MODEL), lambda i: (0, 0)),
                 pl.BlockSpec((N_EXPERTS, LANES), lambda i: (0, 0))]
    return pl.pallas_call(
        functools.partial(_proj_ln_router_kernel, n_in),
        grid=(t // tm,),
        in_specs=in_specs,
        out_specs=[pl.BlockSpec((tm, D_MODEL), lambda i: (i, 0)),
                   pl.BlockSpec((tm, D_MODEL), lambda i: (i, 0)),
                   pl.BlockSpec((2 * TOP_K, tm), lambda i: (0, i)),
                   pl.BlockSpec((2 * TOP_K, tm), lambda i: (0, i)),
                   pl.BlockSpec((N_EXPERTS, LANES), lambda i: (0, 0))],
        out_shape=[jax.ShapeDtypeStruct((t, D_MODEL), F32),
                   jax.ShapeDtypeStruct((t, D_MODEL), BF16),
                   jax.ShapeDtypeStruct((2 * TOP_K, t), jnp.int32),
                   jax.ShapeDtypeStruct((2 * TOP_K, t), F32),
                   jax.ShapeDtypeStruct((N_EXPERTS, LANES), F32)],
        scratch_shapes=[pltpu.VMEM((N_EXPERTS, LANES), F32)],
        compiler_params=_params("arbitrary"),
        name="proj_ln_router",
    )(*a_list, *w_list, x, ln_g, ln_b, w_r, b_r)


def _moe_kernel(blk_ref, exp_ref, lo_ref, hi_ref, x_ref, wgu_ref, wd_ref, bg_ref, bu_ref, bd_ref, o_ref,
                wg_scr, wu_scr, wd_scr):
    i = pl.program_id(0)
    tm = x_ref.shape[0]
    blk = blk_ref[i]
    lo = lo_ref[i]
    hi = hi_ref[i]
    prev = jnp.maximum(i - 1, 0)
    first = jnp.logical_or(i == 0, blk != blk_ref[prev])
    new_expert = jnp.logical_or(i == 0, exp_ref[i] != exp_ref[prev])

    @pl.when(new_expert)
    def _():
        pc = MOE_PREP_COLS
        half = pc // 2
        for c in range(2 * D_FF // pc):
            for kc in range(D_MODEL // LANES):
                k0 = kc * LANES
                tb = wgu_ref[0, 0, k0:k0 + LANES, c * pc:(c + 1) * pc].astype(BF16).T
                pair = pltpu.bitcast(tb, jnp.uint32)
                gate = pltpu.bitcast(jnp.left_shift(pair, 16), F32)
                up = pltpu.bitcast(jnp.bitwise_and(pair, jnp.uint32(0xFFFF0000)), F32)
                wg_scr[c * half:(c + 1) * half, k0:k0 + LANES] = gate.astype(BF16)
                wu_scr[c * half:(c + 1) * half, k0:k0 + LANES] = up.astype(BF16)
        wd_scr[...] = wd_ref[0, 0].astype(BF16)

    def ffn(x):
        glu = jnp.minimum(_dot_nt(x, wg_scr[...]) + bg_ref[0, 0], SWIGLU_LIMIT)
        lin = jnp.clip(_dot_nt(x, wu_scr[...]) + bu_ref[0, 0], -SWIGLU_LIMIT, SWIGLU_LIMIT)
        act = glu * _sigmoid(SWIGLU_ALPHA * glu) * (lin + 1.0)
        return (_dot(act.astype(BF16), wd_scr[...]) + bd_ref[0, 0]).astype(o_ref.dtype)

    full = (hi - lo) == tm

    @pl.when(full)
    def _():
        o_ref[...] = ffn(x_ref[...])

    @pl.when(jnp.logical_and(first, jnp.logical_not(full)))
    def _():
        o_ref[...] = jnp.zeros(o_ref.shape, o_ref.dtype)

    for r0 in range(0, tm, MOE_PART):
        row0 = blk * tm + r0
        touched = jnp.logical_and(hi > lo, jnp.logical_and(lo < row0 + MOE_PART, hi > row0))

        @pl.when(jnp.logical_and(jnp.logical_not(full), touched))
        def _():
            y = ffn(x_ref[r0:r0 + MOE_PART, :])
            row = row0 + lax.broadcasted_iota(jnp.int32, (MOE_PART, 1), 0)
            keep = jnp.logical_and(row >= lo, row < hi)
            o_ref[r0:r0 + MOE_PART, :] = jnp.where(keep, y, o_ref[r0:r0 + MOE_PART, :])


def _moe_sorted(layer, xs, counts, w_gate_up, b_gate, b_up, w_down, b_down):
    n_rows = xs.shape[0]
    tm = _row_tile(n_rows, MOE_ROWS)
    items = _moe_items(counts, n_rows, tm)
    n_items = items[0].shape[0]

    def by_expert(i, blk, ex, lo, hi):
        return (layer, ex[i], 0, 0)

    def by_block(i, blk, ex, lo, hi):
        return (blk[i], 0)

    grid_spec = pltpu.PrefetchScalarGridSpec(
        num_scalar_prefetch=4,
        grid=(n_items,),
        in_specs=[pl.BlockSpec((tm, D_MODEL), by_block),
                  pl.BlockSpec((1, 1, D_MODEL, 2 * D_FF), by_expert),
                  pl.BlockSpec((1, 1, D_FF, D_MODEL), by_expert),
                  pl.BlockSpec((1, 1, 1, D_FF), by_expert),
                  pl.BlockSpec((1, 1, 1, D_FF), by_expert),
                  pl.BlockSpec((1, 1, 1, D_MODEL), by_expert)],
        out_specs=pl.BlockSpec((tm, D_MODEL), by_block),
        scratch_shapes=[pltpu.VMEM((D_FF, D_MODEL), BF16),
                        pltpu.VMEM((D_FF, D_MODEL), BF16),
                        pltpu.VMEM((D_FF, D_MODEL), BF16)],
    )
    return pl.pallas_call(
        _moe_kernel,
        grid_spec=grid_spec,
        out_shape=jax.ShapeDtypeStruct((n_rows, D_MODEL), BF16),
        compiler_params=_params("arbitrary", vmem=MOE_VMEM_LIMIT),
        name="moe_ffn",
    )(*items, xs, w_gate_up, w_down, b_gate, b_up, b_down)


def _moe_items(counts, n_rows, tm):
    n_blocks = n_rows // tm
    start = jnp.cumsum(counts) - counts
    cuts = jnp.sort(jnp.concatenate([jnp.arange(n_blocks, dtype=jnp.int32) * tm, start.astype(jnp.int32)]))
    lo = cuts
    hi = jnp.concatenate([cuts[1:], jnp.array([n_rows], jnp.int32)])
    blk = jnp.minimum(lo // tm, n_blocks - 1)
    ex = jnp.clip(jnp.sum((start[None, :] <= lo[:, None]).astype(jnp.int32), axis=1) - 1, 0, N_EXPERTS - 1)
    return blk.astype(jnp.int32), ex.astype(jnp.int32), lo.astype(jnp.int32), hi.astype(jnp.int32)


def _combine_ln_kernel(x_ref, y_ref, gate_ref, g_ref, b_ref, o_ref):
    o_ref[...] = _combine_rows(x_ref, y_ref, gate_ref, g_ref, b_ref)


def _combine_ln(x1, y4, gates, ln_g, ln_b, tm):
    t = x1.shape[0]
    return pl.pallas_call(
        _combine_ln_kernel,
        grid=(t // tm,),
        in_specs=[pl.BlockSpec((tm, D_MODEL), lambda i: (i, 0)),
                  pl.BlockSpec((TOP_K, tm, D_MODEL), lambda i: (0, i, 0)),
                  pl.BlockSpec((2 * TOP_K, tm), lambda i: (0, i)),
                  pl.BlockSpec((1, D_MODEL), lambda i: (0, 0)),
                  pl.BlockSpec((1, D_MODEL), lambda i: (0, 0))],
        out_specs=pl.BlockSpec((tm, D_MODEL), lambda i: (i, 0)),
        out_shape=jax.ShapeDtypeStruct((t, D_MODEL), F32),
        compiler_params=_params("parallel"),
        name="combine_ln",
    )(x1, y4, gates, ln_g, ln_b)


def _moe_layer(layer, x1, x1b, meta_t, gates_t, counts, ln_g, ln_b, w_gate_up, b_gate, b_up, w_down, b_down):
    t = x1.shape[0]
    assert N_EXPERTS * t < 2 ** 31
    e = meta_t[0:TOP_K]
    pos = meta_t[TOP_K:2 * TOP_K]
    cnt = counts[:, 0].astype(jnp.int32)
    start = jnp.cumsum(cnt) - cnt
    experts = jnp.arange(N_EXPERTS, dtype=jnp.int32)
    dest = pos + jnp.sum(jnp.where(e[:, :, None] == experts, start, 0), axis=-1)
    keys = e * t + jnp.arange(t, dtype=jnp.int32)[None, :]
    tok_sorted = jnp.remainder(jnp.sort(keys.reshape(-1)), t)
    xs = x1b[tok_sorted]
    ys = _moe_sorted(layer, xs, cnt, w_gate_up, b_gate, b_up, w_down, b_down)
    y4 = ys[dest]
    return (x1, y4, gates_t, ln_g, ln_b)


def kernel(x, even_w_in, even_sgu_ln_g, even_sgu_ln_b, even_sgu_w, even_sgu_b, even_conv_w, even_a_log, even_dt_bias, even_gdn_norm, even_w_out, odd_w_in, odd_q_norm, odd_w_uq, odd_kv_norm, odd_w_ukv, odd_w_out, ln_mix_g, ln_mix_b, ln_ffn_g, ln_ffn_b, moe_w_router, moe_b_router, moe_w_gate_up, moe_b_gate_up, moe_w_down, moe_b_down):
    bsz, seq, d = x.shape
    t = bsz * seq
    xt = x.reshape(t, d)
    tm = _row_tile(t, ROWS)
    b_gate = moe_b_gate_up[:, :, None, 0::2]
    b_up = moe_b_gate_up[:, :, None, 1::2]
    b_down = moe_b_down[:, :, None, :]

    inv_freq = ROPE_THETA ** (-jnp.arange(0, C_ROPE, 2, dtype=F32) / C_ROPE)
    ang = jnp.arange(seq, dtype=F32)[:, None] * inv_freq[None, :]
    cs = jnp.concatenate([jnp.cos(ang), jnp.cos(ang), jnp.sin(ang), jnp.sin(ang)], axis=1)

    pos_chunk = np.arange(A_BLOCK) // CHUNK
    allowed = jnp.asarray(pos_chunk[None, :] <= pos_chunk[:, None])

    def rot_cols(w):
        half = C_ROPE // 2
        return jnp.concatenate([-w[..., half:], w[..., :half]], axis=-1)

    for layer in range(DEPTH):
        i = layer // 2
        if layer % 2 == 0:
            w_in = even_w_in[i]
            w_all = jnp.pad(w_in, ((0, 0), (0, LANES - 2 * B_HEADS))).astype(BF16)
            sw = jnp.where(allowed[None], even_sgu_w[i], 0).astype(BF16)
            sb = jnp.broadcast_to(even_sgu_b[i][:, :, None], (A_GROUPS, A_BLOCK, A_GROUP_DIM)).astype(F32)
            w_abt = w_in[:, 2 * A_WIDTH + 4 * B_WIDTH:].T.astype(BF16)
            xt, y_a, h_b, ab, abt = _even_in(xt, w_all, w_abt, even_sgu_ln_g[i][None], even_sgu_ln_b[i][None], sw, sb,
                                             bsz, seq, tm)
            hp = jnp.pad(jnp.stack([even_a_log[i], even_dt_bias[i]]), ((0, 0), (0, LANES - B_HEADS)))
            hpt = jnp.pad(jnp.stack([even_a_log[i], even_dt_bias[i]], axis=1), ((0, SUBLANES - B_HEADS), (0, 0)))
            y_b = _gdn(h_b, ab, abt, even_conv_w[i], hp, hpt, even_gdn_norm[i][None], bsz, seq)
            w_out = even_w_out[i].astype(BF16)
            a_list = [y_a, y_b]
            w_list = [w_out[0:A_WIDTH], w_out[A_WIDTH:]]
        else:
            w_in = odd_w_in[i]
            w_pe = w_in[:, Q_LORA + KV_LORA:]
            w_in2 = jnp.concatenate([w_in, rot_cols(w_pe)], axis=1).astype(BF16)
            w_uq = odd_w_uq[i].reshape(Q_LORA, C_HEADS, C_QK)
            w_uq2 = jnp.concatenate([w_uq, rot_cols(w_uq[..., C_NOPE:])], axis=-1)
            w_uq2 = w_uq2.reshape(Q_LORA, C_HEADS * (C_NOPE + 2 * C_ROPE)).astype(BF16)
            xt, q, k, v = _mla_up(xt, w_in2, odd_q_norm[i][None], odd_kv_norm[i][None], w_uq2,
                                  odd_w_ukv[i].astype(BF16), cs, bsz, seq, _row_tile(seq, ROWS))
            o = _attn(q, k, v)
            a_list = [o.reshape(t, C_HEADS * C_V)]
            w_list = [odd_w_out[i].astype(BF16)]

        w_r = moe_w_router[layer].T.astype(BF16)
        b_r = jnp.broadcast_to(moe_b_router[layer][:, None], (N_EXPERTS, LANES))
        x1, x1b, meta, gates, counts = _proj_ln_router(a_list, w_list, xt, ln_mix_g[layer][None],
                                                       ln_mix_b[layer][None], w_r, b_r, tm)
        xt = _moe_layer(layer, x1, x1b, meta, gates, counts, ln_ffn_g[layer][None], ln_ffn_b[layer][None],
                        moe_w_gate_up, b_gate, b_up, moe_w_down, b_down)
    return _combine_ln(*xt, tm).reshape(bsz, seq, d)
```
